```python
import math
import jax, jax.numpy as jnp
from jax import lax
import numpy as np

D_MODEL = 2048
BATCH = 4
SEQ = 2048
DEPTH = 4

PLE_DIM = 256
ATTN_HEADS = 8
ATTN_HEAD_DIM = 128
ATTN_WIDTH = ATTN_HEADS * ATTN_HEAD_DIM
MOBA_BLOCK = 256
MOBA_TOPK = 3
MOBA_Q_BLOCK = 128
REL_BUCKETS = 32
REL_MAX_DISTANCE = 128
SSM_D_INNER = D_MODEL
SSM_HEAD_DIM = 64
SSM_HEADS = SSM_D_INNER // SSM_HEAD_DIM
SSM_GROUPS = 4
SSM_STATE = 128
SSM_CONV = 4
SSM_CHUNK = 256
SSM_CONV_DIM = SSM_D_INNER + 2 * SSM_GROUPS * SSM_STATE
SC_WIDTH = D_MODEL // 2
SC_CONV = 3
N_BRANCHES = 3
D_FF = 4 * D_MODEL
IN_PROJ_WIDTH = 3 * ATTN_WIDTH + SSM_D_INNER + SSM_CONV_DIM + SSM_HEADS + 3 * SC_WIDTH + N_BRANCHES * D_MODEL
NORM_EPS = 1e-6
NEG_INF = -1e30

kernel_name = "hybrid_moba_ssd_shortconv_trunk"


def rms_norm(x, g):
    xf = x.astype(jnp.float32)
    y = xf * lax.rsqrt(jnp.mean(xf * xf, axis=-1, keepdims=True) + NORM_EPS)
    return (y * g.astype(jnp.float32)).astype(x.dtype)


def causal_depthwise_conv(x, w):
    width, ch = w.shape
    return lax.conv_general_dilated(
        x, w[:, None, :].astype(x.dtype), window_strides=(1,), padding=[(width - 1, 0)],
        dimension_numbers=("NWC", "WIO", "NWC"), feature_group_count=ch)


def t5_bucket(rel):
    n = jnp.maximum(rel, 0)
    max_exact = REL_BUCKETS // 2
    nf = jnp.maximum(n, 1).astype(jnp.float32)
    large = max_exact + (jnp.log(nf / max_exact) / math.log(REL_MAX_DISTANCE / max_exact)
                         * (REL_BUCKETS - max_exact)).astype(jnp.int32)
    large = jnp.minimum(large, REL_BUCKETS - 1)
    return jnp.where(n < max_exact, n, large)


def moba_attention(q, k, v, rel_bias):
    b, s, h, dh = q.shape
    s_pad = -(-s // MOBA_BLOCK) * MOBA_BLOCK
    pad = ((0, 0), (0, s_pad - s), (0, 0), (0, 0))
    q, k, v = [jnp.pad(t, pad).transpose(0, 2, 1, 3) for t in (q, k, v)]
    nb = s_pad // MOBA_BLOCK
    topk = min(MOBA_TOPK, nb)
    k_blk = k.reshape(b, h, nb, MOBA_BLOCK, dh)
    v_blk = v.reshape(b, h, nb, MOBA_BLOCK, dh)
    k_mean = jnp.mean(k_blk.astype(jnp.float32), axis=3)
    table_h = rel_bias.T.astype(jnp.float32)
    bidx = jnp.arange(b)[:, None, None, None]
    hidx = jnp.arange(h)[None, :, None, None]
    scale = dh ** -0.5
    blk_ar = jnp.arange(MOBA_BLOCK)

    def query_block(c):
        start = c * MOBA_Q_BLOCK
        q_c = lax.dynamic_slice_in_dim(q, start, MOBA_Q_BLOCK, axis=2)
        q_pos = start + jnp.arange(MOBA_Q_BLOCK)
        qblk = start // MOBA_BLOCK
        gate = jnp.einsum("bhqd,bhnd->bhqn", q_c.astype(jnp.float32), k_mean)
        gate = jnp.where(jnp.arange(nb) < qblk, gate, -jnp.inf)
        _, sel = lax.top_k(gate, topk)
        valid = sel < qblk
        k_sel = k_blk[bidx, hidx, sel]
        v_sel = v_blk[bidx, hidx, sel]
        k_pos = sel[..., None] * MOBA_BLOCK + blk_ar
        bias_sel = table_h[hidx[..., None], t5_bucket(q_pos[:, None, None] - k_pos)]
        s_sel = jnp.einsum("bhqd,bhqkld->bhqkl", q_c, k_sel).astype(jnp.float32) * scale + bias_sel
        s_sel = jnp.where(valid[..., None], s_sel, NEG_INF)
        k_own = lax.dynamic_index_in_dim(k_blk, qblk, axis=2, keepdims=False)
        v_own = lax.dynamic_index_in_dim(v_blk, qblk, axis=2, keepdims=False)
        rel_own = q_pos[:, None] - (qblk * MOBA_BLOCK + blk_ar)[None, :]
        bias_own = table_h[:, t5_bucket(rel_own)]
        s_own = jnp.einsum("bhqd,bhld->bhql", q_c, k_own).astype(jnp.float32) * scale + bias_own
        s_own = jnp.where(rel_own >= 0, s_own, NEG_INF)
        logits = jnp.concatenate(
            [s_sel.reshape(b, h, MOBA_Q_BLOCK, topk * MOBA_BLOCK), s_own], axis=-1)
        probs = jax.nn.softmax(logits, axis=-1).astype(v.dtype)
        p_sel = probs[..., :topk * MOBA_BLOCK].reshape(b, h, MOBA_Q_BLOCK, topk, MOBA_BLOCK)
        p_own = probs[..., topk * MOBA_BLOCK:]
        return (jnp.einsum("bhqkl,bhqkld->bhqd", p_sel, v_sel)
                + jnp.einsum("bhql,bhld->bhqd", p_own, v_own))

    outs = lax.map(query_block, jnp.arange(s_pad // MOBA_Q_BLOCK))
    return outs.transpose(1, 0, 3, 2, 4).reshape(b, s_pad, h * dh)[:, :s]


def ssd_chunked(x, dt, a_head, bmat, cmat):
    b, s, h, p = x.shape
    g, n = bmat.shape[2], bmat.shape[3]
    e = h // g
    s_pad = -(-s // SSM_CHUNK) * SSM_CHUNK
    ps = s_pad - s
    x = jnp.pad(x, ((0, 0), (0, ps), (0, 0), (0, 0)))
    dt = jnp.pad(dt, ((0, 0), (0, ps), (0, 0)))
    bmat = jnp.pad(bmat, ((0, 0), (0, ps), (0, 0), (0, 0)))
    cmat = jnp.pad(cmat, ((0, 0), (0, ps), (0, 0), (0, 0)))
    c, L = s_pad // SSM_CHUNK, SSM_CHUNK
    xdt = (x * dt[..., None]).reshape(b, c, L, g, e, p)
    a = (dt * a_head).reshape(b, c, L, g, e).transpose(0, 3, 4, 1, 2)
    bm = bmat.reshape(b, c, L, g, n)
    cm = cmat.reshape(b, c, L, g, n)
    a_cum = jnp.cumsum(a, axis=-1)
    causal = jnp.tril(jnp.ones((L, L), dtype=bool))
    seg = a_cum[..., :, None] - a_cum[..., None, :]
    decay_in = jnp.exp(jnp.where(causal, seg, -jnp.inf))
    cb = jnp.einsum("bclgn,bcsgn->bgcls", cm, bm)
    y_diag = jnp.einsum("bgcls,bgecls,bcsgep->bclgep", cb, decay_in, xdt)
    decay_states = jnp.exp(a_cum[..., -1:] - a_cum)
    states = jnp.einsum("bclgn,bgecl,bclgep->bcgepn", bm, decay_states, xdt)
    chunk_decay = jnp.exp(a_cum[..., -1])

    def step(hstate, inp):
        st, dc = inp
        return hstate * dc[..., None, None] + st, hstate

    h0 = jnp.zeros((b, g, e, p, n), dtype=x.dtype)
    _, prev = lax.scan(step, h0, (states.transpose(1, 0, 2, 3, 4, 5), chunk_decay.transpose(3, 0, 1, 2)))
    prev = prev.transpose(1, 0, 2, 3, 4, 5)
    y_off = jnp.einsum("bclgn,bcgepn,bgecl->bclgep", cm, prev, jnp.exp(a_cum))
    return (y_diag + y_off).reshape(b, s_pad, h, p)[:, :s]


def mamba2_mixer(z, xbc, dt_raw, conv_w, conv_b, dt_bias, a_log, d_skip, norm_g):
    b, s, _ = z.shape
    xbc = jax.nn.silu(causal_depthwise_conv(xbc, conv_w) + conv_b.astype(xbc.dtype))
    gn = SSM_GROUPS * SSM_STATE
    xs, bm, cm = jnp.split(xbc, [SSM_D_INNER, SSM_D_INNER + gn], axis=-1)
    xs_h = xs.reshape(b, s, SSM_HEADS, SSM_HEAD_DIM).astype(jnp.float32)
    dt = jax.nn.softplus(dt_raw.astype(jnp.float32) + dt_bias.astype(jnp.float32))
    a_head = -jnp.exp(a_log.astype(jnp.float32))
    y = ssd_chunked(xs_h, dt, a_head,
                    bm.reshape(b, s, SSM_GROUPS, SSM_STATE).astype(jnp.float32),
                    cm.reshape(b, s, SSM_GROUPS, SSM_STATE).astype(jnp.float32))
    y = y + d_skip.astype(jnp.float32)[:, None] * xs_h
    y = y.reshape(b, s, SSM_D_INNER) * jax.nn.silu(z.astype(jnp.float32))
    yg = y.reshape(b, s, SSM_GROUPS, SSM_D_INNER // SSM_GROUPS)
    yg = yg * lax.rsqrt(jnp.mean(yg * yg, axis=-1, keepdims=True) + NORM_EPS)
    return (yg.reshape(b, s, SSM_D_INNER) * norm_g.astype(jnp.float32)).astype(z.dtype)


def setup_inputs(seed: int = 0) -> dict:
    key = jax.random.key(seed)
    ks = jax.random.split(key, 32)

    def w(k, fan_in, fan_out):
        return jax.random.normal(k, (DEPTH, fan_in, fan_out), jnp.float32) * fan_in ** -0.5

    def gain(k, width):
        return 1.0 + 0.05 * jax.random.normal(k, (DEPTH, width), jnp.float32)

    u = jax.random.uniform(ks[8], (DEPTH, SSM_HEADS), jnp.float32)
    dt0 = jnp.exp(u * (math.log(0.1) - math.log(0.001)) + math.log(0.001))
    return {
        "x": jax.random.normal(ks[0], (BATCH, SEQ, D_MODEL), jnp.float32),
        "p": jax.random.normal(ks[1], (DEPTH, BATCH, SEQ, PLE_DIM), jnp.float32),
        "rel_bias": 0.5 * jax.random.normal(ks[2], (REL_BUCKETS, ATTN_HEADS), jnp.float32),
        "g_mix_pre": gain(ks[3], D_MODEL),
        "w_in": w(ks[4], D_MODEL, IN_PROJ_WIDTH),
        "ssm_conv_w": jax.random.normal(ks[5], (DEPTH, SSM_CONV, SSM_CONV_DIM), jnp.float32) * SSM_CONV ** -0.5,
        "ssm_conv_b": 0.01 * jax.random.normal(ks[6], (DEPTH, SSM_CONV_DIM), jnp.float32),
        "ssm_dt_bias": dt0 + jnp.log(-jnp.expm1(-dt0)),
        "ssm_a_log": jnp.log(jax.random.uniform(ks[9], (DEPTH, SSM_HEADS), jnp.float32, 1.0, 16.0)),
        "ssm_d": 1.0 + 0.1 * jax.random.normal(ks[10], (DEPTH, SSM_HEADS), jnp.float32),
        "ssm_norm_g": gain(ks[11], SSM_D_INNER),
        "sc_conv_w": jax.random.normal(ks[12], (DEPTH, SC_CONV, SC_WIDTH), jnp.float32) * SC_CONV ** -0.5,
        "w_br_attn": w(ks[13], ATTN_WIDTH, D_MODEL),
        "w_br_ssm": w(ks[14], SSM_D_INNER, D_MODEL),
        "w_br_conv": w(ks[15], SC_WIDTH, D_MODEL),
        "w_out": w(ks[16], D_MODEL, D_MODEL),
        "g_mix_post": gain(ks[17], D_MODEL),
        "g_mlp_pre": gain(ks[18], D_MODEL),
        "w_mlp_up": w(ks[19], D_MODEL, D_FF),
        "w_mlp_down": w(ks[20], D_FF, D_MODEL),
        "g_mlp_post": gain(ks[21], D_MODEL),
        "g_ple_pre": gain(ks[22], D_MODEL),
        "w_ple_gate": w(ks[23], D_MODEL, D_MODEL),
        "w_ple_proj": w(ks[24], PLE_DIM, D_MODEL),
        "g_ple_post": gain(ks[25], D_MODEL),
    }


def reference(x, p, rel_bias, g_mix_pre, w_in, ssm_conv_w, ssm_conv_b, ssm_dt_bias, ssm_a_log,
              ssm_d, ssm_norm_g, sc_conv_w, w_br_attn, w_br_ssm, w_br_conv, w_out, g_mix_post,
              g_mlp_pre, w_mlp_up, w_mlp_down, g_mlp_post, g_ple_pre, w_ple_gate, w_ple_proj,
              g_ple_post):
    b, s, _ = x.shape
    sizes = (ATTN_WIDTH, ATTN_WIDTH, ATTN_WIDTH, SSM_D_INNER, SSM_CONV_DIM, SSM_HEADS,
             SC_WIDTH, SC_WIDTH, SC_WIDTH, N_BRANCHES * D_MODEL)
    offsets = [int(o) for o in np.cumsum(sizes)[:-1]]
    for i in range(DEPTH):
        h = rms_norm(x, g_mix_pre[i])
        proj = jnp.einsum("bsd,de->bse", h, w_in[i])
        q, k, v, z, xbc, dt_raw, sc_b, sc_c, sc_h, gates = jnp.split(proj, offsets, axis=-1)
        y_attn = moba_attention(q.reshape(b, s, ATTN_HEADS, ATTN_HEAD_DIM),
                                k.reshape(b, s, ATTN_HEADS, ATTN_HEAD_DIM),
                                v.reshape(b, s, ATTN_HEADS, ATTN_HEAD_DIM), rel_bias)
        y_ssm = mamba2_mixer(z, xbc, dt_raw, ssm_conv_w[i], ssm_conv_b[i], ssm_dt_bias[i],
                             ssm_a_log[i], ssm_d[i], ssm_norm_g[i])
        y_sc = sc_b * causal_depthwise_conv(sc_c * sc_h, sc_conv_w[i])
        g_a, g_m, g_c = jnp.split(jax.nn.sigmoid(gates), N_BRANCHES, axis=-1)
        merged = (g_a * jnp.einsum("bse,ed->bsd", y_attn, w_br_attn[i])
                  + g_m * jnp.einsum("bse,ed->bsd", y_ssm, w_br_ssm[i])
                  + g_c * jnp.einsum("bse,ed->bsd", y_sc, w_br_conv[i]))
        x = x + rms_norm(jnp.einsum("bsd,de->bse", merged, w_out[i]), g_mix_post[i])
        hm = rms_norm(x, g_mlp_pre[i])
        u = jnp.square(jax.nn.relu(jnp.einsum("bsd,df->bsf", hm, w_mlp_up[i])))
        x = x + rms_norm(jnp.einsum("bsf,fd->bsd", u, w_mlp_down[i]), g_mlp_post[i])
        gate = jax.nn.sigmoid(jnp.einsum("bsd,de->bse", rms_norm(x, g_ple_pre[i]), w_ple_gate[i]))
        emb = jnp.einsum("bsk,kd->bsd", p[i].astype(x.dtype), w_ple_proj[i])
        x = x + rms_norm(gate * emb, g_ple_post[i])
    return x
```

```python
import functools
import math

import numpy as np
import jax
import jax.numpy as jnp
from jax import lax
from jax.experimental import pallas as pl
from jax.experimental.pallas import tpu as pltpu

F32 = jnp.float32
BF16 = jnp.bfloat16

NORM_EPS = 1e-6
NEG_INF = -1e30

D_MODEL = 2048
PLE_DIM = 256
ATTN_HEADS = 8
ATTN_HEAD_DIM = 128
ATTN_WIDTH = ATTN_HEADS * ATTN_HEAD_DIM
MOBA_BLOCK = 256
MOBA_TOPK = 3
REL_BUCKETS = 32
REL_MAX_DISTANCE = 128
SSM_D_INNER = D_MODEL
SSM_HEAD_DIM = 64
SSM_HEADS = SSM_D_INNER // SSM_HEAD_DIM
SSM_GROUPS = 4
SSM_STATE = 128
SSM_CONV = 4
SSM_CHUNK = 256
SSM_CONV_DIM = SSM_D_INNER + 2 * SSM_GROUPS * SSM_STATE
SC_WIDTH = D_MODEL // 2
SC_CONV = 3
D_FF = 4 * D_MODEL

LANES = 128
SUBLANES = 8
VMEM_LIMIT_BYTES = 56 * 1024 * 1024

XBC_OFF = 0
SC_OFF = XBC_OFF + SSM_CONV_DIM
GATE_OFF = SC_OFF + 3 * SC_WIDTH
Z_OFF = GATE_OFF + 3 * D_MODEL
QKV_OFF = Z_OFF + SSM_D_INNER
DT_OFF = QKV_OFF + 3 * ATTN_WIDTH
PROJ_TN = 512
PROJ_WIDTH = DT_OFF + PROJ_TN

_O_Q = 0
_O_Z = 3 * ATTN_WIDTH
_O_XBC = _O_Z + SSM_D_INNER
_O_DT = _O_XBC + SSM_CONV_DIM
_O_SC = _O_DT + SSM_HEADS
_O_GATE = _O_SC + 3 * SC_WIDTH
_O_END = _O_GATE + 3 * D_MODEL

_NT = (((1,), (1,)), ((), ()))


def _params(*sem):
    return pltpu.CompilerParams(dimension_semantics=sem, vmem_limit_bytes=VMEM_LIMIT_BYTES)


def _rms(x, g):
    return x * lax.rsqrt(jnp.mean(x * x, axis=-1, keepdims=True) + NORM_EPS) * g


def _sigmoid(x):
    return 1.0 / (1.0 + jnp.exp(-x))


def _inproj_body(x_ref, g_ref, w_ref, o_ref, h_ref):
    @pl.when(pl.program_id(1) == 0)
    def _():
        h_ref[...] = _rms(x_ref[...], g_ref[...]).astype(BF16)

    o_ref[...] = jnp.dot(h_ref[...], w_ref[...], preferred_element_type=F32)


def in_projection(x, g, w, layer, tm=1024, tn=PROJ_TN):
    t, d = x.shape
    n = w.shape[-1]
    tm = min(tm, t)
    return pl.pallas_call(
        _inproj_body,
        grid=(t // tm, n // tn),
        in_specs=[
            pl.BlockSpec((tm, d), lambda i, j: (i, 0)),
            pl.BlockSpec((None, 1, d), lambda i, j: (layer, 0, 0)),
            pl.BlockSpec((None, d, tn), lambda i, j: (layer, 0, j)),
        ],
        out_specs=pl.BlockSpec((tm, tn), lambda i, j: (i, j)),
        out_shape=jax.ShapeDtypeStruct((t, n), F32),
        scratch_shapes=[pltpu.VMEM((tm, d), BF16)],
        compiler_params=_params("parallel", "arbitrary"),
        name="in_projection",
    )(x, g, w)


def _bias_body(tbl_ref, o_ref):
    h = pl.program_id(0)
    blk = MOBA_BLOCK
    row = lax.broadcasted_iota(jnp.int32, (blk, blk), 0)
    col = lax.broadcasted_iota(jnp.int32, (blk, blk), 1)
    max_exact = REL_BUCKETS // 2
    for t in range(3):
        n = jnp.maximum(t * blk + row - col, 0)
        nf = jnp.maximum(n, 1).astype(F32)
        large = max_exact + (jnp.log(nf / max_exact) / math.log(REL_MAX_DISTANCE / max_exact)
                             * (REL_BUCKETS - max_exact)).astype(jnp.int32)
        large = jnp.minimum(large, REL_BUCKETS - 1)
        bucket = jnp.where(n < max_exact, n, large)
        acc = jnp.zeros((blk, blk), F32)
        for b in range(REL_BUCKETS):
            acc = jnp.where(bucket == b, tbl_ref[b, h], acc)
        o_ref[t] = acc


def bias_tiles(rel_bias):
    nh = rel_bias.shape[1]
    return pl.pallas_call(
        _bias_body,
        grid=(nh,),
        in_specs=[pl.BlockSpec(memory_space=pltpu.SMEM)],
        out_specs=pl.BlockSpec((None, 3, MOBA_BLOCK, MOBA_BLOCK), lambda h: (h, 0, 0, 0)),
        out_shape=jax.ShapeDtypeStruct((nh, 3, MOBA_BLOCK, MOBA_BLOCK), F32),
        compiler_params=_params("arbitrary"),
        name="bias_tiles",
    )(rel_bias)


def _attn_body(q_ref, k_ref, v_ref, bias_ref, o_ref, kb_ref, vb_ref, kmean_ref, m_ref, l_ref, acc_ref,
               *, nb):
    blk = MOBA_BLOCK
    qi = pl.program_id(2)
    scale = ATTN_HEAD_DIM ** -0.5

    @pl.when(qi == 0)
    def _():
        k = k_ref[...]
        kb_ref[...] = k.astype(BF16)
        vb_ref[...] = v_ref[...].astype(BF16)
        for j in range(nb):
            kmean_ref[j:j + 1, :] = jnp.mean(k[j * blk:(j + 1) * blk, :], axis=0, keepdims=True)

    q = q_ref[...]
    qb = q.astype(BF16)

    gate = lax.dot_general(q, kmean_ref[0:nb, :], _NT, precision=lax.Precision.HIGHEST,
                           preferred_element_type=F32)
    bidx = lax.broadcasted_iota(jnp.int32, (blk, nb), 1)
    past = bidx < qi
    selected = []
    for j in range(nb - 1):
        g_j = gate[:, j:j + 1]
        beats = ((gate > g_j) | ((gate == g_j) & (bidx < j))) & past
        rank = jnp.sum(beats.astype(jnp.int32), axis=1, keepdims=True)
        selected.append(rank < MOBA_TOPK)

    start = pl.multiple_of(qi * blk, blk)
    s = lax.dot_general(qb, kb_ref[pl.ds(start, blk), :], _NT, preferred_element_type=F32)
    s = s * scale + bias_ref[0]
    row = lax.broadcasted_iota(jnp.int32, (blk, blk), 0)
    col = lax.broadcasted_iota(jnp.int32, (blk, blk), 1)
    s = jnp.where(row >= col, s, NEG_INF)
    m = jnp.max(s, axis=1, keepdims=True)
    p = jnp.exp(s - m)
    m_ref[...] = m
    l_ref[...] = jnp.sum(p, axis=1, keepdims=True)
    acc_ref[...] = jnp.dot(p.astype(BF16), vb_ref[pl.ds(start, blk), :], preferred_element_type=F32)

    for j in range(nb - 1):
        @pl.when(j < qi)
        def _(j=j):
            s = lax.dot_general(qb, kb_ref[j * blk:(j + 1) * blk, :], _NT, preferred_element_type=F32)
            s = s * scale + bias_ref[jnp.minimum(qi - j, 2)]
            s = jnp.where(selected[j], s, NEG_INF)
            m_old = m_ref[...]
            m_new = jnp.maximum(m_old, jnp.max(s, axis=1, keepdims=True))
            alpha = jnp.exp(m_old - m_new)
            p = jnp.exp(s - m_new)
            l_ref[...] = alpha * l_ref[...] + jnp.sum(p, axis=1, keepdims=True)
            acc_ref[...] = alpha * acc_ref[...] + jnp.dot(
                p.astype(BF16), vb_ref[j * blk:(j + 1) * blk, :], preferred_element_type=F32)
            m_ref[...] = m_new

    o_ref[...] = (acc_ref[...] / l_ref[...]).astype(o_ref.dtype)


def moba_attention(proj, bias, batch, seq, qkv_off):
    t = proj.shape[0]
    blk, dh, nh = MOBA_BLOCK, ATTN_HEAD_DIM, ATTN_HEADS
    nb = seq // blk
    qc = qkv_off // dh
    return pl.pallas_call(
        functools.partial(_attn_body, nb=nb),
        grid=(batch, nh, nb),
        in_specs=[
            pl.BlockSpec((blk, dh), lambda b, h, i: (b * nb + i, qc + h)),
            pl.BlockSpec((seq, dh), lambda b, h, i: (b, qc + nh + h)),
            pl.BlockSpec((seq, dh), lambda b, h, i: (b, qc + 2 * nh + h)),
            pl.BlockSpec((None, 3, blk, blk), lambda b, h, i: (h, 0, 0, 0)),
        ],
        out_specs=pl.BlockSpec((blk, dh), lambda b, h, i: (b * nb + i, h)),
        out_shape=jax.ShapeDtypeStruct((t, nh * dh), BF16),
        scratch_shapes=[
            pltpu.VMEM((seq, dh), BF16),
            pltpu.VMEM((seq, dh), BF16),
            pltpu.VMEM((max(nb, SUBLANES), dh), F32),
            pltpu.VMEM((blk, 1), F32),
            pltpu.VMEM((blk, 1), F32),
            pltpu.VMEM((blk, dh), F32),
        ],
        compiler_params=_params("parallel", "parallel", "arbitrary"),
        name="moba_attention",
    )(proj, proj, proj, bias)


def _split_dot(v, e):
    hi = v.astype(BF16)
    lo = (v - hi.astype(F32)).astype(BF16)
    return (jnp.dot(hi, e, preferred_element_type=F32) + jnp.dot(lo, e, preferred_element_type=F32))


def _mixer_body(xbc_ref, sc_ref, z_ref, dt_ref, cw_ref, cb_ref, dtb_ref, alog_ref, dskip_ref, ng_ref,
                scw_ref, e_ref, yssm_ref, ysc_ref, ext_ref, ext2_ref, xc_ref, state_ref):
    c = pl.program_id(1)
    L = SSM_CHUNK
    halo = SUBLANES
    gw = SSM_D_INNER // SSM_GROUPS
    hpg = SSM_HEADS // SSM_GROUPS
    n_state = SSM_STATE

    @pl.when(c == 0)
    def _():
        ext_ref[0:halo, :] = jnp.zeros((halo, ext_ref.shape[1]), F32)
        ext2_ref[0:halo, :] = jnp.zeros((halo, ext2_ref.shape[1]), F32)
        state_ref[...] = jnp.zeros(state_ref.shape, F32)

    @pl.when(c > 0)
    def _():
        ext_ref[0:halo, :] = ext_ref[L:L + halo, :]
        ext2_ref[0:halo, :] = ext2_ref[L:L + halo, :]

    ext_ref[halo:halo + L, :] = xbc_ref[...]
    cchunk = 512
    for ci in range(SSM_CONV_DIM // cchunk):
        cs = slice(ci * cchunk, (ci + 1) * cchunk)
        v = cb_ref[:, cs]
        for k in range(SSM_CONV):
            off = halo - (SSM_CONV - 1) + k
            v = v + cw_ref[k:k + 1, cs] * ext_ref[off:off + L, cs]
        xc_ref[:, cs] = v * _sigmoid(v)

    ext2_ref[halo:halo + L, :] = sc_ref[:, SC_WIDTH:2 * SC_WIDTH] * sc_ref[:, 2 * SC_WIDTH:3 * SC_WIDTH]
    conv = None
    for k in range(SC_CONV):
        off = halo - (SC_CONV - 1) + k
        term = scw_ref[k:k + 1, :] * ext2_ref[off:off + L, :]
        conv = term if conv is None else conv + term
    ysc_ref[...] = (sc_ref[:, 0:SC_WIDTH] * conv).astype(ysc_ref.dtype)

    x_dt = dt_ref[...] + dtb_ref[...]
    dt = jnp.maximum(x_dt, 0.0) + jnp.log1p(jnp.exp(-jnp.abs(x_dt)))
    a = dt * (-jnp.exp(alog_ref[...]))
    rows = lax.broadcasted_iota(jnp.int32, (L, LANES), 0)
    a_cum = a
    sh = 1
    while sh < L:
        a_cum = a_cum + jnp.where(rows >= sh, pltpu.roll(a_cum, sh, axis=0), 0.0)
        sh *= 2
    a_cum_t = a_cum.T
    ea = jnp.exp(a_cum)
    ds = jnp.exp(a_cum[L - 1:L, :] - a_cum)

    row = lax.broadcasted_iota(jnp.int32, (L, L), 0)
    col = lax.broadcasted_iota(jnp.int32, (L, L), 1)
    causal = row >= col
    lane = lax.broadcasted_iota(jnp.int32, (L, LANES), 1)
    low_half = lane < SSM_HEAD_DIM

    for g in range(SSM_GROUPS):
        cs = slice(g * gw, (g + 1) * gw)
        b_g = xc_ref[:, SSM_D_INNER + g * n_state:SSM_D_INNER + (g + 1) * n_state]
        c_g = xc_ref[:, SSM_D_INNER + (SSM_GROUPS + g) * n_state:SSM_D_INNER + (SSM_GROUPS + g + 1) * n_state]
        b_gb = b_g.astype(BF16)
        c_gb = c_g.astype(BF16)
        cb = lax.dot_general(c_gb, b_gb, _NT, preferred_element_type=F32)
        e_g = e_ref[:, cs]
        xs_g = xc_ref[:, cs]
        ea_e = _split_dot(ea, e_g)
        xdt = xs_g * _split_dot(dt, e_g)
        xdt_b = xdt.astype(BF16)

        st = state_ref[g]
        y = jnp.dot(c_gb, st.astype(BF16), preferred_element_type=F32) * ea_e

        pieces = []
        for pe in range(hpg // 2):
            x_pair = xdt_b[:, pe * LANES:(pe + 1) * LANES]
            acc = None
            for half in range(2):
                h = g * hpg + 2 * pe + half
                seg = a_cum[:, h:h + 1] - a_cum_t[h:h + 1, :]
                decay = jnp.exp(jnp.where(causal, seg, -jnp.inf))
                m_h = (cb * decay).astype(BF16)
                keep = low_half if half == 0 else jnp.logical_not(low_half)
                x_h = jnp.where(keep, x_pair, jnp.zeros_like(x_pair))
                term = jnp.dot(m_h, x_h, preferred_element_type=F32)
                acc = term if acc is None else acc + term
            pieces.append(acc)
        y = y + jnp.concatenate(pieces, axis=1)

        xds = (xdt * _split_dot(ds, e_g)).astype(BF16)
        upd = jnp.dot(b_g.T.astype(BF16), xds, preferred_element_type=F32)
        state_ref[g] = st * ea_e[L - 1:L, :] + upd

        y = y + dskip_ref[:, cs] * xs_g
        z_g = z_ref[:, cs]
        y = y * (z_g * _sigmoid(z_g))
        y = y * lax.rsqrt(jnp.mean(y * y, axis=-1, keepdims=True) + NORM_EPS)
        yssm_ref[:, cs] = (y * ng_ref[:, cs]).astype(yssm_ref.dtype)


def _head_expand_matrix():
    e = np.zeros((LANES, SSM_D_INNER), np.float32)
    for h in range(SSM_HEADS):
        e[h, h * SSM_HEAD_DIM:(h + 1) * SSM_HEAD_DIM] = 1.0
    return jnp.asarray(e, BF16)


def ssd_and_short_conv(proj, conv_w, conv_b, dt_bias, a_log, d_skip, norm_g, sc_w, layer, batch, seq):
    t = proj.shape[0]
    L = SSM_CHUNK
    nc = seq // L
    row_map = lambda b, c: (b * nc + c, 0)

    def col_block(off, width):
        return pl.BlockSpec((L, width), lambda b, c: (b * nc + c, off // width))

    def param(rows, cols):
        return pl.BlockSpec((None, rows, cols), lambda b, c: (layer, 0, 0))

    return pl.pallas_call(
        _mixer_body,
        grid=(batch, nc),
        in_specs=[
            col_block(XBC_OFF, SSM_CONV_DIM),
            col_block(SC_OFF, 3 * SC_WIDTH),
            col_block(Z_OFF, SSM_D_INNER),
            col_block(DT_OFF, LANES),
            param(SSM_CONV, SSM_CONV_DIM),
            param(1, SSM_CONV_DIM),
            param(1, LANES),
            param(1, LANES),
            param(1, SSM_D_INNER),
            param(1, SSM_D_INNER),
            param(SC_CONV, SC_WIDTH),
            pl.BlockSpec((LANES, SSM_D_INNER), lambda b, c: (0, 0)),
        ],
        out_specs=[
            pl.BlockSpec((L, SSM_D_INNER), row_map),
            pl.BlockSpec((L, SC_WIDTH), row_map),
        ],
        out_shape=[
            jax.ShapeDtypeStruct((t, SSM_D_INNER), BF16),
            jax.ShapeDtypeStruct((t, SC_WIDTH), BF16),
        ],
        scratch_shapes=[
            pltpu.VMEM((L + 2 * SUBLANES, SSM_CONV_DIM), F32),
            pltpu.VMEM((L + 2 * SUBLANES, SC_WIDTH), F32),
            pltpu.VMEM((L, SSM_CONV_DIM), F32),
            pltpu.VMEM((SSM_GROUPS, SSM_STATE, SSM_D_INNER // SSM_GROUPS), F32),
        ],
        compiler_params=_params("parallel", "arbitrary"),
        name="ssd_short_conv",
    )(proj, proj, proj, proj, conv_w, conv_b, dt_bias, a_log, d_skip, norm_g, sc_w, _head_expand_matrix())


def _merge_body(ya_ref, ym_ref, yc_ref, ga_ref, gm_ref, gc_ref, wa_ref, wm_ref, wc_ref, o_ref):
    a = jnp.dot(ya_ref[...], wa_ref[...], preferred_element_type=F32)
    m = jnp.dot(ym_ref[...], wm_ref[...], preferred_element_type=F32)
    c = jnp.dot(yc_ref[...], wc_ref[...], preferred_element_type=F32)
    o = _sigmoid(ga_ref[...]) * a + _sigmoid(gm_ref[...]) * m + _sigmoid(gc_ref[...]) * c
    o_ref[...] = o.astype(o_ref.dtype)


def branch_merge(y_attn, y_ssm, y_sc, proj, w_a, w_m, w_c, layer, tm=1024, tn=512):
    t = y_attn.shape[0]
    d = w_a.shape[-1]
    tm = min(tm, t)
    gc = GATE_OFF // tn
    gs = d // tn

    def act(width):
        return pl.BlockSpec((tm, width), lambda i, j: (i, 0))

    def gate(k):
        return pl.BlockSpec((tm, tn), lambda i, j: (i, gc + k * gs + j))

    def weight(rows):
        return pl.BlockSpec((None, rows, tn), lambda i, j: (layer, 0, j))

    return pl.pallas_call(
        _merge_body,
        grid=(t // tm, d // tn),
        in_specs=[act(y_attn.shape[1]), act(y_ssm.shape[1]), act(y_sc.shape[1]),
                  gate(0), gate(1), gate(2),
                  weight(w_a.shape[1]), weight(w_m.shape[1]), weight(w_c.shape[1])],
        out_specs=pl.BlockSpec((tm, tn), lambda i, j: (i, j)),
        out_shape=jax.ShapeDtypeStruct((t, d), BF16),
        compiler_params=_params("parallel", "arbitrary"),
        name="branch_merge",
    )(y_attn, y_ssm, y_sc, proj, proj, proj, w_a, w_m, w_c)


def _outproj_body(m_ref, x_ref, w_ref, g_ref, o_ref):
    o = jnp.dot(m_ref[...], w_ref[...], preferred_element_type=F32)
    o_ref[...] = x_ref[...] + _rms(o, g_ref[...])


def out_projection(merged, x, w, g, layer, tm=512):
    t, d = x.shape
    tm = min(tm, t)
    return pl.pallas_call(
        _outproj_body,
        grid=(t // tm,),
        in_specs=[
            pl.BlockSpec((tm, d), lambda i: (i, 0)),
            pl.BlockSpec((tm, d), lambda i: (i, 0)),
            pl.BlockSpec((None, d, d), lambda i: (layer, 0, 0)),
            pl.BlockSpec((None, 1, d), lambda i: (layer, 0, 0)),
        ],
        out_specs=pl.BlockSpec((tm, d), lambda i: (i, 0)),
        out_shape=jax.ShapeDtypeStruct((t, d), F32),
        compiler_params=_params("parallel"),
        name="out_projection",
    )(merged, x, w, g)


def _mlp_body(x_ref, gpre_ref, w1_ref, w2_ref, gpost_ref, o_ref, h_ref, acc_ref):
    f = pl.program_id(1)

    @pl.when(f == 0)
    def _():
        h_ref[...] = _rms(x_ref[...], gpre_ref[...]).astype(BF16)

    u = jnp.maximum(jnp.dot(h_ref[...], w1_ref[...], preferred_element_type=F32), 0.0)
    part = jnp.dot((u * u).astype(BF16), w2_ref[...], preferred_element_type=F32)

    @pl.when(f == 0)
    def _():
        acc_ref[...] = part

    @pl.when(f > 0)
    def _():
        acc_ref[...] += part

    @pl.when(f == pl.num_programs(1) - 1)
    def _():
        o_ref[...] = x_ref[...] + _rms(acc_ref[...], gpost_ref[...])


def mlp(x, g_pre, w1, w2, g_post, layer, tm=512, tf=1024):
    t, d = x.shape
    ff = w1.shape[-1]
    tm = min(tm, t)
    return pl.pallas_call(
        _mlp_body,
        grid=(t // tm, ff // tf),
        in_specs=[
            pl.BlockSpec((tm, d), lambda i, f: (i, 0)),
            pl.BlockSpec((None, 1, d), lambda i, f: (layer, 0, 0)),
            pl.BlockSpec((None, d, tf), lambda i, f: (layer, 0, f)),
            pl.BlockSpec((None, tf, d), lambda i, f: (layer, f, 0)),
            pl.BlockSpec((None, 1, d), lambda i, f: (layer, 0, 0)),
        ],
        out_specs=pl.BlockSpec((tm, d), lambda i, f: (i, 0)),
        out_shape=jax.ShapeDtypeStruct((t, d), F32),
        scratch_shapes=[pltpu.VMEM((tm, d), BF16), pltpu.VMEM((tm, d), F32)],
        compiler_params=_params("parallel", "arbitrary"),
        name="mlp",
    )(x, g_pre, w1, w2, g_post)


def _ple_body(x_ref, p_ref, gpre_ref, wg_ref, wp_ref, gpost_ref, o_ref):
    x = x_ref[...]
    h = _rms(x, gpre_ref[...]).astype(BF16)
    gate = _sigmoid(jnp.dot(h, wg_ref[...], preferred_element_type=F32))
    emb = jnp.dot(p_ref[...].astype(BF16), wp_ref[...], preferred_element_type=F32)
    o_ref[...] = x + _rms(gate * emb, gpost_ref[...])


def per_layer_embedding(x, p, g_pre, w_gate, w_proj, g_post, layer, tm=512):
    t, d = x.shape
    pd = p.shape[-1]
    tm = min(tm, t)
    return pl.pallas_call(
        _ple_body,
        grid=(t // tm,),
        in_specs=[
            pl.BlockSpec((tm, d), lambda i: (i, 0)),
            pl.BlockSpec((None, tm, pd), lambda i: (layer, i, 0)),
            pl.BlockSpec((None, 1, d), lambda i: (layer, 0, 0)),
            pl.BlockSpec((None, d, d), lambda i: (layer, 0, 0)),
            pl.BlockSpec((None, pd, d), lambda i: (layer, 0, 0)),
            pl.BlockSpec((None, 1, d), lambda i: (layer, 0, 0)),
        ],
        out_specs=pl.BlockSpec((tm, d), lambda i: (i, 0)),
        out_shape=jax.ShapeDtypeStruct((t, d), F32),
        compiler_params=_params("parallel"),
        name="per_layer_embedding",
    )(x, p, g_pre, w_gate, w_proj, g_post)


def _reorder_in_proj(w_in):
    depth, d, _ = w_in.shape
    pad = jnp.zeros((depth, d, PROJ_WIDTH - DT_OFF - SSM_HEADS), w_in.dtype)
    parts = [w_in[..., _O_XBC:_O_DT], w_in[..., _O_SC:_O_GATE], w_in[..., _O_GATE:_O_END],
             w_in[..., _O_Z:_O_XBC], w_in[..., _O_Q:_O_Z], w_in[..., _O_DT:_O_SC], pad]
    return jnp.concatenate(parts, axis=-1).astype(BF16)


def _row(v):
    return v[:, None, :]


def _lane_pad(v):
    return jnp.pad(v, ((0, 0), (0, LANES - v.shape[-1])))[:, None, :]


def kernel(x, p, rel_bias, g_mix_pre, w_in, ssm_conv_w, ssm_conv_b, ssm_dt_bias, ssm_a_log, ssm_d, ssm_norm_g, sc_conv_w, w_br_attn, w_br_ssm, w_br_conv, w_out, g_mix_post, g_mlp_pre, w_mlp_up, w_mlp_down, g_mlp_post, g_ple_pre, w_ple_gate, w_ple_proj, g_ple_post):
    batch, seq, d = x.shape
    depth = w_in.shape[0]
    t = batch * seq

    w_in_r = _reorder_in_proj(w_in)
    w_a, w_m, w_c = w_br_attn.astype(BF16), w_br_ssm.astype(BF16), w_br_conv.astype(BF16)
    w_o = w_out.astype(BF16)
    w1, w2 = w_mlp_up.astype(BF16), w_mlp_down.astype(BF16)
    w_pg, w_pp = w_ple_gate.astype(BF16), w_ple_proj.astype(BF16)
    d_skip = jnp.repeat(ssm_d, SSM_HEAD_DIM, axis=-1)[:, None, :]
    dt_bias, a_log = _lane_pad(ssm_dt_bias), _lane_pad(ssm_a_log)
    conv_b, norm_g = _row(ssm_conv_b), _row(ssm_norm_g)
    g_mix_pre, g_mix_post = _row(g_mix_pre), _row(g_mix_post)
    g_mlp_pre, g_mlp_post = _row(g_mlp_pre), _row(g_mlp_post)
    g_ple_pre, g_ple_post = _row(g_ple_pre), _row(g_ple_post)

    bias = bias_tiles(rel_bias)
    xf = x.reshape(t, d)
    pf = p.reshape(depth, t, p.shape[-1])
    for i in range(depth):
        proj = in_projection(xf, g_mix_pre, w_in_r, i)
        y_attn = moba_attention(proj, bias, batch, seq, QKV_OFF)
        y_ssm, y_sc = ssd_and_short_conv(proj, ssm_conv_w, conv_b, dt_bias, a_log, d_skip, norm_g,
                                         sc_conv_w, i, batch, seq)
        merged = branch_merge(y_attn, y_ssm, y_sc, proj, w_a, w_m, w_c, i)
        xf = out_projection(merged, xf, w_o, g_mix_post, i)
        xf = mlp(xf, g_mlp_pre, w1, w2, g_mlp_post, i)
        xf = per_layer_embedding(xf, pf, g_ple_pre, w_pg, w_pp, g_ple_post, i)
    return xf.reshape(batch, seq, d)
```

```python
import functools
import math

import numpy as np
import jax
import jax.numpy as jnp
from jax import lax
from jax.experimental import pallas as pl
from jax.experimental.pallas import tpu as pltpu

F32 = jnp.float32
BF16 = jnp.bfloat16

NORM_EPS = 1e-6
NEG_INF = -1e30

D_MODEL = 2048
PLE_DIM = 256
ATTN_HEADS = 8
ATTN_HEAD_DIM = 128
ATTN_WIDTH = ATTN_HEADS * ATTN_HEAD_DIM
MOBA_BLOCK = 256
MOBA_TOPK = 3
REL_BUCKETS = 32
REL_MAX_DISTANCE = 128
SSM_D_INNER = D_MODEL
SSM_HEAD_DIM = 64
SSM_HEADS = SSM_D_INNER // SSM_HEAD_DIM
SSM_GROUPS = 4
SSM_STATE = 128
SSM_CONV = 4
SSM_CHUNK = 256
SSM_CONV_DIM = SSM_D_INNER + 2 * SSM_GROUPS * SSM_STATE
SC_WIDTH = D_MODEL // 2
SC_CONV = 3
D_FF = 4 * D_MODEL

LANES = 128
SUBLANES = 8
VMEM_LIMIT_BYTES = 56 * 1024 * 1024

XBC_OFF = 0
SC_OFF = XBC_OFF + SSM_CONV_DIM
GATE_OFF = SC_OFF + 3 * SC_WIDTH
Z_OFF = GATE_OFF + 3 * D_MODEL
QKV_OFF = Z_OFF + SSM_D_INNER
DT_OFF = QKV_OFF + 3 * ATTN_WIDTH
PROJ_TN = 512
PROJ_WIDTH = DT_OFF + PROJ_TN

_O_Q = 0
_O_Z = 3 * ATTN_WIDTH
_O_XBC = _O_Z + SSM_D_INNER
_O_DT = _O_XBC + SSM_CONV_DIM
_O_SC = _O_DT + SSM_HEADS
_O_GATE = _O_SC + 3 * SC_WIDTH
_O_END = _O_GATE + 3 * D_MODEL

_NT = (((1,), (1,)), ((), ()))


def _params(*sem):
    return pltpu.CompilerParams(dimension_semantics=sem, vmem_limit_bytes=VMEM_LIMIT_BYTES)


def _rms(x, g):
    return x * lax.rsqrt(jnp.mean(x * x, axis=-1, keepdims=True) + NORM_EPS) * g


def _sigmoid(x):
    return 1.0 / (1.0 + jnp.exp(-x))


_T = PROJ_TN
_HEAD_WIDTH = -(-(_O_DT + SSM_HEADS) // _T) * _T
_N_XBC, _N_TAIL, _N_Z, _N_QKV = SSM_CONV_DIM // _T, (_O_END - _O_SC) // _T, SSM_D_INNER // _T, 3 * ATTN_WIDTH // _T
_J_TAIL, _J_Z, _J_QKV, _J_DT = _N_XBC, _N_XBC + _N_TAIL, _N_XBC + _N_TAIL + _N_Z, _N_XBC + _N_TAIL + _N_Z + _N_QKV


def _head_tile(j):
    return jnp.where(j < _J_TAIL, _O_XBC // _T + j,
                     jnp.where(j < _J_Z, _O_XBC // _T + _N_XBC - 1,
                               jnp.where(j < _J_QKV, _O_Z // _T + (j - _J_Z),
                                         jnp.where(j < _J_DT, j - _J_QKV, _O_DT // _T))))


def _tail_tile(j):
    return jnp.clip(j - _J_TAIL, 0, _N_TAIL - 1)


def _inproj_body(x_ref, g_ref, wh_ref, wt_ref, o_ref, h_ref):
    j = pl.program_id(1)

    @pl.when(j == 0)
    def _():
        h_ref[...] = _rms(x_ref[...], g_ref[...]).astype(BF16)

    from_tail = (j >= _J_TAIL) & (j < _J_Z)

    @pl.when(from_tail)
    def _():
        o_ref[...] = jnp.dot(h_ref[...], wt_ref[...], preferred_element_type=F32)

    @pl.when(jnp.logical_not(from_tail))
    def _():
        o_ref[...] = jnp.dot(h_ref[...], wh_ref[...], preferred_element_type=F32)


def in_projection(x, g, w_head, w_tail, layer, tm=1024):
    t, d = x.shape
    tn = _T
    tm = min(tm, t)
    return pl.pallas_call(
        _inproj_body,
        grid=(t // tm, PROJ_WIDTH // tn),
        in_specs=[
            pl.BlockSpec((tm, d), lambda i, j: (i, 0)),
            pl.BlockSpec((None, 1, d), lambda i, j: (layer, 0, 0)),
            pl.BlockSpec((None, d, tn), lambda i, j: (layer, 0, _head_tile(j))),
            pl.BlockSpec((None, d, tn), lambda i, j: (layer, 0, _tail_tile(j))),
        ],
        out_specs=pl.BlockSpec((tm, tn), lambda i, j: (i, j)),
        out_shape=jax.ShapeDtypeStruct((t, PROJ_WIDTH), F32),
        scratch_shapes=[pltpu.VMEM((tm, d), BF16)],
        compiler_params=_params("parallel", "arbitrary"),
        name="in_projection",
    )(x, g, w_head, w_tail)


def _bias_body(tbl_ref, o_ref):
    h = pl.program_id(0)
    blk = MOBA_BLOCK
    key = lax.broadcasted_iota(jnp.int32, (blk, blk), 0)
    qry = lax.broadcasted_iota(jnp.int32, (blk, blk), 1)
    max_exact = REL_BUCKETS // 2
    for t in range(3):
        n = jnp.maximum(t * blk + qry - key, 0)
        nf = jnp.maximum(n, 1).astype(F32)
        large = max_exact + (jnp.log(nf / max_exact) / math.log(REL_MAX_DISTANCE / max_exact)
                             * (REL_BUCKETS - max_exact)).astype(jnp.int32)
        large = jnp.minimum(large, REL_BUCKETS - 1)
        bucket = jnp.where(n < max_exact, n, large)
        acc = jnp.zeros((blk, blk), F32)
        for b in range(REL_BUCKETS):
            acc = jnp.where(bucket == b, tbl_ref[b, h], acc)
        o_ref[t] = acc


def bias_tiles(rel_bias):
    nh = rel_bias.shape[1]
    return pl.pallas_call(
        _bias_body,
        grid=(nh,),
        in_specs=[pl.BlockSpec(memory_space=pltpu.SMEM)],
        out_specs=pl.BlockSpec((None, 3, MOBA_BLOCK, MOBA_BLOCK), lambda h: (h, 0, 0, 0)),
        out_shape=jax.ShapeDtypeStruct((nh, 3, MOBA_BLOCK, MOBA_BLOCK), F32),
        compiler_params=_params("arbitrary"),
        name="bias_tiles",
    )(rel_bias)


def _attn_body(q_ref, k_ref, v_ref, bias_ref, o_ref, kmean_ref, *, nb):
    blk = MOBA_BLOCK
    scale = ATTN_HEAD_DIM ** -0.5

    k = k_ref[...]
    kb = k.astype(BF16)
    v_t = v_ref[...].T.astype(BF16)
    for j in range(nb):
        kmean_ref[j:j + 1, :] = jnp.mean(k[j * blk:(j + 1) * blk, :], axis=0, keepdims=True)
    q_all = q_ref[...]
    gate_all = lax.dot_general(kmean_ref[...], q_all, _NT, precision=lax.Precision.HIGHEST,
                               preferred_element_type=F32)
    qb_all = (q_all * scale).astype(BF16)

    key = lax.broadcasted_iota(jnp.int32, (blk, blk), 0)
    qry = lax.broadcasted_iota(jnp.int32, (blk, blk), 1)
    causal = key <= qry
    bias_own = bias_ref[0]
    bias_prev = bias_ref[1]
    bias_far = bias_ref[2, 0:1, 0:1]
    bidx = lax.broadcasted_iota(jnp.int32, (gate_all.shape[0], blk), 0)

    for qi in range(nb):
        cols = slice(qi * blk, (qi + 1) * blk)
        nk = (qi + 1) * blk
        s = lax.dot_general(kb[0:nk, :], qb_all[cols, :], _NT, preferred_element_type=F32)

        keep = None
        if qi > MOBA_TOPK:
            gate = gate_all[:, cols]
            past = bidx < qi
            keep = []
            for j in range(qi):
                g_j = gate[j:j + 1, :]
                beats = ((gate > g_j) | ((gate == g_j) & (bidx < j))) & past
                keep.append(jnp.sum(beats.astype(F32), axis=0, keepdims=True) < MOBA_TOPK)

        pieces = []
        for j in range(qi + 1):
            s_j = s[j * blk:(j + 1) * blk, :]
            if j == qi:
                s_j = jnp.where(causal, s_j + bias_own, NEG_INF)
            elif j == qi - 1:
                s_j = s_j + bias_prev
                if keep is not None:
                    s_j = jnp.where(keep[j], s_j, NEG_INF)
            elif keep is not None:
                s_j = s_j + jnp.where(keep[j], bias_far, NEG_INF)
            else:
                s_j = s_j + bias_far
            pieces.append(s_j)

        m = pieces[0]
        for s_j in pieces[1:]:
            m = jnp.maximum(m, s_j)
        m = jnp.max(m, axis=0, keepdims=True)
        probs = [jnp.exp(s_j - m) for s_j in pieces]
        total = probs[0]
        for p_j in probs[1:]:
            total = total + p_j
        denom = jnp.sum(total, axis=0, keepdims=True)
        p_t = jnp.concatenate([p_j.astype(BF16) for p_j in probs], axis=0)
        o_t = jnp.dot(v_t[:, 0:nk], p_t, preferred_element_type=F32)
        o_ref[cols, :] = (o_t / denom).T.astype(o_ref.dtype)


def moba_attention(proj, bias, batch, seq, qkv_off):
    t = proj.shape[0]
    blk, dh, nh = MOBA_BLOCK, ATTN_HEAD_DIM, ATTN_HEADS
    nb = seq // blk
    qc = qkv_off // dh
    return pl.pallas_call(
        functools.partial(_attn_body, nb=nb),
        grid=(batch, nh),
        in_specs=[
            pl.BlockSpec((seq, dh), lambda b, h: (b, qc + h)),
            pl.BlockSpec((seq, dh), lambda b, h: (b, qc + nh + h)),
            pl.BlockSpec((seq, dh), lambda b, h: (b, qc + 2 * nh + h)),
            pl.BlockSpec((None, 3, blk, blk), lambda b, h: (h, 0, 0, 0)),
        ],
        out_specs=pl.BlockSpec((seq, dh), lambda b, h: (b, h)),
        out_shape=jax.ShapeDtypeStruct((t, nh * dh), BF16),
        scratch_shapes=[pltpu.VMEM((max(nb, SUBLANES), dh), F32)],
        compiler_params=_params("parallel", "parallel"),
        name="moba_attention",
    )(proj, proj, proj, bias)


def _split_dot(v, e):
    hi = v.astype(BF16)
    lo = (v - hi.astype(F32)).astype(BF16)
    return (jnp.dot(hi, e, preferred_element_type=F32) + jnp.dot(lo, e, preferred_element_type=F32))


def _mixer_body(xbc_ref, sc_ref, z_ref, dt_ref, cw_ref, cb_ref, dtb_ref, alog_ref, dskip_ref, ng_ref,
                scw_ref, e_ref, yssm_ref, ysc_ref, ext_ref, ext2_ref, xc_ref, state_ref):
    c = pl.program_id(1)
    L = SSM_CHUNK
    halo = SUBLANES
    gw = SSM_D_INNER // SSM_GROUPS
    hpg = SSM_HEADS // SSM_GROUPS
    n_state = SSM_STATE

    @pl.when(c == 0)
    def _():
        ext_ref[0:halo, :] = jnp.zeros((halo, ext_ref.shape[1]), F32)
        ext2_ref[0:halo, :] = jnp.zeros((halo, ext2_ref.shape[1]), F32)
        state_ref[...] = jnp.zeros(state_ref.shape, F32)

    @pl.when(c > 0)
    def _():
        ext_ref[0:halo, :] = ext_ref[L:L + halo, :]
        ext2_ref[0:halo, :] = ext2_ref[L:L + halo, :]

    ext_ref[halo:halo + L, :] = xbc_ref[...]
    cchunk = 512
    for ci in range(SSM_CONV_DIM // cchunk):
        cs = slice(ci * cchunk, (ci + 1) * cchunk)
        v = cb_ref[:, cs]
        for k in range(SSM_CONV):
            off = halo - (SSM_CONV - 1) + k
            v = v + cw_ref[k:k + 1, cs] * ext_ref[off:off + L, cs]
        xc_ref[:, cs] = v * _sigmoid(v)

    ext2_ref[halo:halo + L, :] = sc_ref[:, SC_WIDTH:2 * SC_WIDTH] * sc_ref[:, 2 * SC_WIDTH:3 * SC_WIDTH]
    conv = None
    for k in range(SC_CONV):
        off = halo - (SC_CONV - 1) + k
        term = scw_ref[k:k + 1, :] * ext2_ref[off:off + L, :]
        conv = term if conv is None else conv + term
    ysc_ref[...] = (sc_ref[:, 0:SC_WIDTH] * conv).astype(ysc_ref.dtype)

    rows = lax.broadcasted_iota(jnp.int32, (L, LANES), 0)
    head_lane = lax.broadcasted_iota(jnp.int32, (L, LANES), 1) < SSM_HEADS
    x_dt = jnp.where(head_lane, dt_ref[...], 0.0) + dtb_ref[...]
    dt = jnp.maximum(x_dt, 0.0) + jnp.log1p(jnp.exp(-jnp.abs(x_dt)))
    a = dt * (-jnp.exp(alog_ref[...]))
    a_cum = a
    sh = 1
    while sh < L:
        a_cum = a_cum + jnp.where(rows >= sh, pltpu.roll(a_cum, sh, axis=0), 0.0)
        sh *= 2
    a_cum_t = a_cum.T
    ea = jnp.exp(a_cum)
    ds = jnp.exp(a_cum[L - 1:L, :] - a_cum)

    row = lax.broadcasted_iota(jnp.int32, (L, L), 0)
    col = lax.broadcasted_iota(jnp.int32, (L, L), 1)
    causal = row >= col
    lane = lax.broadcasted_iota(jnp.int32, (L, LANES), 1)
    low_half = lane < SSM_HEAD_DIM

    for g in range(SSM_GROUPS):
        cs = slice(g * gw, (g + 1) * gw)
        b_g = xc_ref[:, SSM_D_INNER + g * n_state:SSM_D_INNER + (g + 1) * n_state]
        c_g = xc_ref[:, SSM_D_INNER + (SSM_GROUPS + g) * n_state:SSM_D_INNER + (SSM_GROUPS + g + 1) * n_state]
        b_gb = b_g.astype(BF16)
        c_gb = c_g.astype(BF16)
        cb = lax.dot_general(c_gb, b_gb, _NT, preferred_element_type=F32)
        e_g = e_ref[:, cs]
        xs_g = xc_ref[:, cs]
        ea_e = _split_dot(ea, e_g)
        xdt = xs_g * _split_dot(dt, e_g)
        xdt_b = xdt.astype(BF16)

        st = state_ref[g]
        y = jnp.dot(c_gb, st.astype(BF16), preferred_element_type=F32) * ea_e

        pieces = []
        for pe in range(hpg // 2):
            x_pair = xdt_b[:, pe * LANES:(pe + 1) * LANES]
            acc = None
            for half in range(2):
                h = g * hpg + 2 * pe + half
                seg = a_cum[:, h:h + 1] - a_cum_t[h:h + 1, :]
                decay = jnp.exp(jnp.where(causal, seg, -jnp.inf))
                m_h = (cb * decay).astype(BF16)
                keep = low_half if half == 0 else jnp.logical_not(low_half)
                x_h = jnp.where(keep, x_pair, jnp.zeros_like(x_pair))
                term = jnp.dot(m_h, x_h, preferred_element_type=F32)
                acc = term if acc is None else acc + term
            pieces.append(acc)
        y = y + jnp.concatenate(pieces, axis=1)

        xds = (xdt * _split_dot(ds, e_g)).astype(BF16)
        upd = jnp.dot(b_g.T.astype(BF16), xds, preferred_element_type=F32)
        state_ref[g] = st * ea_e[L - 1:L, :] + upd

        y = y + dskip_ref[:, cs] * xs_g
        z_g = z_ref[:, cs]
        y = y * (z_g * _sigmoid(z_g))
        y = y * lax.rsqrt(jnp.mean(y * y, axis=-1, keepdims=True) + NORM_EPS)
        yssm_ref[:, cs] = (y * ng_ref[:, cs]).astype(yssm_ref.dtype)


def _head_expand_matrix():
    e = np.zeros((LANES, SSM_D_INNER), np.float32)
    for h in range(SSM_HEADS):
        e[h, h * SSM_HEAD_DIM:(h + 1) * SSM_HEAD_DIM] = 1.0
    return jnp.asarray(e, BF16)


def ssd_and_short_conv(proj, conv_w, conv_b, dt_bias, a_log, d_skip, norm_g, sc_w, layer, batch, seq):
    t = proj.shape[0]
    L = SSM_CHUNK
    nc = seq // L
    row_map = lambda b, c: (b * nc + c, 0)

    def col_block(off, width):
        return pl.BlockSpec((L, width), lambda b, c: (b * nc + c, off // width))

    def param(rows, cols):
        return pl.BlockSpec((None, rows, cols), lambda b, c: (layer, 0, 0))

    return pl.pallas_call(
        _mixer_body,
        grid=(batch, nc),
        in_specs=[
            col_block(XBC_OFF, SSM_CONV_DIM),
            col_block(SC_OFF, 3 * SC_WIDTH),
            col_block(Z_OFF, SSM_D_INNER),
            col_block(DT_OFF, LANES),
            param(SSM_CONV, SSM_CONV_DIM),
            param(1, SSM_CONV_DIM),
            param(1, LANES),
            param(1, LANES),
            param(1, SSM_D_INNER),
            param(1, SSM_D_INNER),
            param(SC_CONV, SC_WIDTH),
            pl.BlockSpec((LANES, SSM_D_INNER), lambda b, c: (0, 0)),
        ],
        out_specs=[
            pl.BlockSpec((L, SSM_D_INNER), row_map),
            pl.BlockSpec((L, SC_WIDTH), row_map),
        ],
        out_shape=[
            jax.ShapeDtypeStruct((t, SSM_D_INNER), BF16),
            jax.ShapeDtypeStruct((t, SC_WIDTH), BF16),
        ],
        scratch_shapes=[
            pltpu.VMEM((L + 2 * SUBLANES, SSM_CONV_DIM), F32),
            pltpu.VMEM((L + 2 * SUBLANES, SC_WIDTH), F32),
            pltpu.VMEM((L, SSM_CONV_DIM), F32),
            pltpu.VMEM((SSM_GROUPS, SSM_STATE, SSM_D_INNER // SSM_GROUPS), F32),
        ],
        compiler_params=_params("parallel", "arbitrary"),
        name="ssd_short_conv",
    )(proj, proj, proj, proj, conv_w, conv_b, dt_bias, a_log, d_skip, norm_g, sc_w, _head_expand_matrix())


def _merge_body(ya_ref, ym_ref, yc_ref, ga_ref, gm_ref, gc_ref, wa_ref, wm_ref, wc_ref, o_ref):
    a = jnp.dot(ya_ref[...], wa_ref[...], preferred_element_type=F32)
    m = jnp.dot(ym_ref[...], wm_ref[...], preferred_element_type=F32)
    c = jnp.dot(yc_ref[...], wc_ref[...], preferred_element_type=F32)
    o = _sigmoid(ga_ref[...]) * a + _sigmoid(gm_ref[...]) * m + _sigmoid(gc_ref[...]) * c
    o_ref[...] = o.astype(o_ref.dtype)


def branch_merge(y_attn, y_ssm, y_sc, proj, w_a, w_m, w_c, layer, tm=1024, tn=512):
    t = y_attn.shape[0]
    d = w_a.shape[-1]
    tm = min(tm, t)
    gc = GATE_OFF // tn
    gs = d // tn

    def act(width):
        return pl.BlockSpec((tm, width), lambda i, j: (i, 0))

    def gate(k):
        return pl.BlockSpec((tm, tn), lambda i, j: (i, gc + k * gs + j))

    def weight(rows):
        return pl.BlockSpec((None, rows, tn), lambda i, j: (layer, 0, j))

    return pl.pallas_call(
        _merge_body,
        grid=(t // tm, d // tn),
        in_specs=[act(y_attn.shape[1]), act(y_ssm.shape[1]), act(y_sc.shape[1]),
                  gate(0), gate(1), gate(2),
                  weight(w_a.shape[1]), weight(w_m.shape[1]), weight(w_c.shape[1])],
        out_specs=pl.BlockSpec((tm, tn), lambda i, j: (i, j)),
        out_shape=jax.ShapeDtypeStruct((t, d), BF16),
        compiler_params=_params("parallel", "arbitrary"),
        name="branch_merge",
    )(y_attn, y_ssm, y_sc, proj, proj, proj, w_a, w_m, w_c)


def _outproj_body(m_ref, x_ref, w_ref, g_ref, o_ref):
    o = jnp.dot(m_ref[...], w_ref[...], preferred_element_type=F32)
    o_ref[...] = x_ref[...] + _rms(o, g_ref[...])


def out_projection(merged, x, w, g, layer, tm=512):
    t, d = x.shape
    tm = min(tm, t)
    return pl.pallas_call(
        _outproj_body,
        grid=(t // tm,),
        in_specs=[
            pl.BlockSpec((tm, d), lambda i: (i, 0)),
            pl.BlockSpec((tm, d), lambda i: (i, 0)),
            pl.BlockSpec((None, d, d), lambda i: (layer, 0, 0)),
            pl.BlockSpec((None, 1, d), lambda i: (layer, 0, 0)),
        ],
        out_specs=pl.BlockSpec((tm, d), lambda i: (i, 0)),
        out_shape=jax.ShapeDtypeStruct((t, d), F32),
        compiler_params=_params("parallel"),
        name="out_projection",
    )(merged, x, w, g)


def _mlp_body(x_ref, gpre_ref, w1_ref, w2_ref, gpost_ref, o_ref, h_ref, acc_ref):
    f = pl.program_id(1)

    @pl.when(f == 0)
    def _():
        h_ref[...] = _rms(x_ref[...], gpre_ref[...]).astype(BF16)

    u = jnp.maximum(jnp.dot(h_ref[...], w1_ref[...], preferred_element_type=F32), 0.0)
    part = jnp.dot((u * u).astype(BF16), w2_ref[...], preferred_element_type=F32)

    @pl.when(f == 0)
    def _():
        acc_ref[...] = part

    @pl.when(f > 0)
    def _():
        acc_ref[...] += part

    @pl.when(f == pl.num_programs(1) - 1)
    def _():
        o_ref[...] = x_ref[...] + _rms(acc_ref[...], gpost_ref[...])


def mlp(x, g_pre, w1, w2, g_post, layer, tm=512, tf=1024):
    t, d = x.shape
    ff = w1.shape[-1]
    tm = min(tm, t)
    return pl.pallas_call(
        _mlp_body,
        grid=(t // tm, ff // tf),
        in_specs=[
            pl.BlockSpec((tm, d), lambda i, f: (i, 0)),
            pl.BlockSpec((None, 1, d), lambda i, f: (layer, 0, 0)),
            pl.BlockSpec((None, d, tf), lambda i, f: (layer, 0, f)),
            pl.BlockSpec((None, tf, d), lambda i, f: (layer, f, 0)),
            pl.BlockSpec((None, 1, d), lambda i, f: (layer, 0, 0)),
        ],
        out_specs=pl.BlockSpec((tm, d), lambda i, f: (i, 0)),
        out_shape=jax.ShapeDtypeStruct((t, d), F32),
        scratch_shapes=[pltpu.VMEM((tm, d), BF16), pltpu.VMEM((tm, d), F32)],
        compiler_params=_params("parallel", "arbitrary"),
        name="mlp",
    )(x, g_pre, w1, w2, g_post)


def _ple_body(x_ref, p_ref, gpre_ref, wg_ref, wp_ref, gpost_ref, o_ref):
    x = x_ref[...]
    h = _rms(x, gpre_ref[...]).astype(BF16)
    gate = _sigmoid(jnp.dot(h, wg_ref[...], preferred_element_type=F32))
    emb = jnp.dot(p_ref[...].astype(BF16), wp_ref[...], preferred_element_type=F32)
    o_ref[...] = x + _rms(gate * emb, gpost_ref[...])


def per_layer_embedding(x, p, g_pre, w_gate, w_proj, g_post, layer, tm=512):
    t, d = x.shape
    pd = p.shape[-1]
    tm = min(tm, t)
    return pl.pallas_call(
        _ple_body,
        grid=(t // tm,),
        in_specs=[
            pl.BlockSpec((tm, d), lambda i: (i, 0)),
            pl.BlockSpec((None, tm, pd), lambda i: (layer, i, 0)),
            pl.BlockSpec((None, 1, d), lambda i: (layer, 0, 0)),
            pl.BlockSpec((None, d, d), lambda i: (layer, 0, 0)),
            pl.BlockSpec((None, pd, d), lambda i: (layer, 0, 0)),
            pl.BlockSpec((None, 1, d), lambda i: (layer, 0, 0)),
        ],
        out_specs=pl.BlockSpec((tm, d), lambda i: (i, 0)),
        out_shape=jax.ShapeDtypeStruct((t, d), F32),
        compiler_params=_params("parallel"),
        name="per_layer_embedding",
    )(x, p, g_pre, w_gate, w_proj, g_post)


def _split_in_proj(w_in):
    return w_in[..., :_HEAD_WIDTH].astype(BF16), w_in[..., _O_SC:_O_END].astype(BF16)


def _row(v):
    return v[:, None, :]


def _lane_pad(v):
    return jnp.pad(v, ((0, 0), (0, LANES - v.shape[-1])))[:, None, :]


def kernel(x, p, rel_bias, g_mix_pre, w_in, ssm_conv_w, ssm_conv_b, ssm_dt_bias, ssm_a_log, ssm_d, ssm_norm_g, sc_conv_w, w_br_attn, w_br_ssm, w_br_conv, w_out, g_mix_post, g_mlp_pre, w_mlp_up, w_mlp_down, g_mlp_post, g_ple_pre, w_ple_gate, w_ple_proj, g_ple_post):
    batch, seq, d = x.shape
    depth = w_in.shape[0]
    t = batch * seq

    w_head, w_tail = _split_in_proj(w_in)
    w_a, w_m, w_c = w_br_attn.astype(BF16), w_br_ssm.astype(BF16), w_br_conv.astype(BF16)
    w_o = w_out.astype(BF16)
    w1, w2 = w_mlp_up.astype(BF16), w_mlp_down.astype(BF16)
    w_pg, w_pp = w_ple_gate.astype(BF16), w_ple_proj.astype(BF16)
    d_skip = jnp.repeat(ssm_d, SSM_HEAD_DIM, axis=-1)[:, None, :]
    dt_bias, a_log = _lane_pad(ssm_dt_bias), _lane_pad(ssm_a_log)
    conv_b, norm_g = _row(ssm_conv_b), _row(ssm_norm_g)
    g_mix_pre, g_mix_post = _row(g_mix_pre), _row(g_mix_post)
    g_mlp_pre, g_mlp_post = _row(g_mlp_pre), _row(g_mlp_post)
    g_ple_pre, g_ple_post = _row(g_ple_pre), _row(g_ple_post)

    bias = bias_tiles(rel_bias)
    xf = x.reshape(t, d)
    pf = p.reshape(depth, t, p.shape[-1])
    for i in range(depth):
        proj = in_projection(xf, g_mix_pre, w_head, w_tail, i)
        y_attn = moba_attention(proj, bias, batch, seq, QKV_OFF)
        y_ssm, y_sc = ssd_and_short_conv(proj, ssm_conv_w, conv_b, dt_bias, a_log, d_skip, norm_g,
                                         sc_conv_w, i, batch, seq)
        merged = branch_merge(y_attn, y_ssm, y_sc, proj, w_a, w_m, w_c, i)
        xf = out_projection(merged, xf, w_o, g_mix_post, i)
        xf = mlp(xf, g_mlp_pre, w1, w2, g_mlp_post, i)
        xf = per_layer_embedding(xf, pf, g_ple_pre, w_pg, w_pp, g_ple_post, i)
    return xf.reshape(batch, seq, d)
```

```python
import functools
import math

import numpy as np
import jax
import jax.numpy as jnp
from jax import lax
from jax.experimental import pallas as pl
from jax.experimental.pallas import tpu as pltpu

F32 = jnp.float32
BF16 = jnp.bfloat16

NORM_EPS = 1e-6
NEG_INF = -1e30

D_MODEL = 2048
PLE_DIM = 256
ATTN_HEADS = 8
ATTN_HEAD_DIM = 128
ATTN_WIDTH = ATTN_HEADS * ATTN_HEAD_DIM
MOBA_BLOCK = 256
MOBA_TOPK = 3
REL_BUCKETS = 32
REL_MAX_DISTANCE = 128
SSM_D_INNER = D_MODEL
SSM_HEAD_DIM = 64
SSM_HEADS = SSM_D_INNER // SSM_HEAD_DIM
SSM_GROUPS = 4
SSM_STATE = 128
SSM_CONV = 4
SSM_CHUNK = 256
SSM_CONV_DIM = SSM_D_INNER + 2 * SSM_GROUPS * SSM_STATE
SC_WIDTH = D_MODEL // 2
SC_CONV = 3
D_FF = 4 * D_MODEL

LANES = 128
SUBLANES = 8
VMEM_LIMIT_BYTES = 56 * 1024 * 1024

XBC_OFF = 0
SC_OFF = XBC_OFF + SSM_CONV_DIM
GATE_OFF = SC_OFF + 3 * SC_WIDTH
Z_OFF = GATE_OFF + 3 * D_MODEL
QKV_OFF = Z_OFF + SSM_D_INNER
DT_OFF = QKV_OFF + 3 * ATTN_WIDTH
PROJ_TN = 512
PROJ_WIDTH = DT_OFF + PROJ_TN

_O_Q = 0
_O_Z = 3 * ATTN_WIDTH
_O_XBC = _O_Z + SSM_D_INNER
_O_DT = _O_XBC + SSM_CONV_DIM
_O_SC = _O_DT + SSM_HEADS
_O_GATE = _O_SC + 3 * SC_WIDTH
_O_END = _O_GATE + 3 * D_MODEL

_NT = (((1,), (1,)), ((), ()))


def _params(*sem):
    return pltpu.CompilerParams(dimension_semantics=sem, vmem_limit_bytes=VMEM_LIMIT_BYTES)


def _rms(x, g):
    return x * lax.rsqrt(jnp.mean(x * x, axis=-1, keepdims=True) + NORM_EPS) * g


def _sigmoid(x):
    return 1.0 / (1.0 + jnp.exp(-x))


def _row_chunks(n, parts=4):
    step = n // parts
    return [slice(k * step, (k + 1) * step) for k in range(parts)]


_T = PROJ_TN
_N_XBC, _N_TAIL, _N_Z, _N_QKV = SSM_CONV_DIM // _T, (_O_END - _O_SC) // _T, SSM_D_INNER // _T, 3 * ATTN_WIDTH // _T
_J_TAIL, _J_Z, _J_QKV, _J_DT = _N_XBC, _N_XBC + _N_TAIL, _N_XBC + _N_TAIL + _N_Z, _N_XBC + _N_TAIL + _N_Z + _N_QKV


def _head_tile(j):
    return jnp.where(j < _J_TAIL, _O_XBC // _T + j,
                     jnp.where(j < _J_Z, _O_XBC // _T + _N_XBC - 1,
                               jnp.where(j < _J_QKV, _O_Z // _T + (j - _J_Z),
                                         jnp.where(j < _J_DT, j - _J_QKV, _O_DT // _T))))


def _tail_tile(j):
    return jnp.clip(j - _J_TAIL, 0, _N_TAIL - 1)


def _norm_body(x_ref, g_ref, o_ref):
    o_ref[...] = _rms(x_ref[...], g_ref[...]).astype(o_ref.dtype)


def pre_norm(x, g, layer, tm=512):
    t, d = x.shape
    tm = min(tm, t)
    return pl.pallas_call(
        _norm_body,
        grid=(t // tm,),
        in_specs=[pl.BlockSpec((tm, d), lambda i: (i, 0)),
                  pl.BlockSpec((None, 1, d), lambda i: (layer, 0, 0))],
        out_specs=pl.BlockSpec((tm, d), lambda i: (i, 0)),
        out_shape=jax.ShapeDtypeStruct((t, d), BF16),
        compiler_params=_params("parallel"),
        name="pre_norm",
    )(x, g)


def _inproj_body(h_ref, wh_ref, wt_ref, o_ref):
    j = pl.program_id(1)
    from_tail = (j >= _J_TAIL) & (j < _J_Z)

    @pl.when(from_tail)
    def _():
        o_ref[...] = jnp.dot(h_ref[...], wt_ref[...].astype(BF16), preferred_element_type=F32)

    @pl.when(jnp.logical_not(from_tail))
    def _():
        o_ref[...] = jnp.dot(h_ref[...], wh_ref[...].astype(BF16), preferred_element_type=F32)


def in_projection(h, w_in, w_tail, layer, tm=2048):
    t, d = h.shape
    tn = _T
    tm = min(tm, t)
    return pl.pallas_call(
        _inproj_body,
        grid=(t // tm, PROJ_WIDTH // tn),
        in_specs=[
            pl.BlockSpec((tm, d), lambda i, j: (i, 0)),
            pl.BlockSpec((None, d, tn), lambda i, j: (layer, 0, _head_tile(j))),
            pl.BlockSpec((None, d, tn), lambda i, j: (layer, 0, _tail_tile(j))),
        ],
        out_specs=pl.BlockSpec((tm, tn), lambda i, j: (i, j)),
        out_shape=jax.ShapeDtypeStruct((t, PROJ_WIDTH), F32),
        compiler_params=_params("parallel", "arbitrary"),
        name="in_projection",
    )(h, w_in, w_tail)


def _bias_body(tbl_ref, o_ref):
    h = pl.program_id(0)
    blk = MOBA_BLOCK
    key = lax.broadcasted_iota(jnp.int32, (blk, blk), 0)
    qry = lax.broadcasted_iota(jnp.int32, (blk, blk), 1)
    max_exact = REL_BUCKETS // 2
    for t in range(3):
        n = jnp.maximum(t * blk + qry - key, 0)
        nf = jnp.maximum(n, 1).astype(F32)
        large = max_exact + (jnp.log(nf / max_exact) / math.log(REL_MAX_DISTANCE / max_exact)
                             * (REL_BUCKETS - max_exact)).astype(jnp.int32)
        large = jnp.minimum(large, REL_BUCKETS - 1)
        bucket = jnp.where(n < max_exact, n, large)
        acc = jnp.zeros((blk, blk), F32)
        for b in range(REL_BUCKETS):
            acc = jnp.where(bucket == b, tbl_ref[b, h], acc)
        o_ref[t] = acc


def bias_tiles(rel_bias):
    nh = rel_bias.shape[1]
    return pl.pallas_call(
        _bias_body,
        grid=(nh,),
        in_specs=[pl.BlockSpec(memory_space=pltpu.SMEM)],
        out_specs=pl.BlockSpec((None, 3, MOBA_BLOCK, MOBA_BLOCK), lambda h: (h, 0, 0, 0)),
        out_shape=jax.ShapeDtypeStruct((nh, 3, MOBA_BLOCK, MOBA_BLOCK), F32),
        compiler_params=_params("arbitrary"),
        name="bias_tiles",
    )(rel_bias)


def _attn_body(q_ref, k_ref, v_ref, bias_ref, o_ref, kmean_ref, *, nb):
    blk = MOBA_BLOCK
    scale = ATTN_HEAD_DIM ** -0.5

    k = k_ref[...]
    kb = k.astype(BF16)
    v_t = v_ref[...].T.astype(BF16)
    for j in range(nb):
        kmean_ref[j:j + 1, :] = jnp.mean(k[j * blk:(j + 1) * blk, :], axis=0, keepdims=True)
    q_all = q_ref[...]
    gate_all = lax.dot_general(kmean_ref[...], q_all, _NT, precision=lax.Precision.HIGHEST,
                               preferred_element_type=F32)
    qb_all = (q_all * scale).astype(BF16)

    key = lax.broadcasted_iota(jnp.int32, (blk, blk), 0)
    qry = lax.broadcasted_iota(jnp.int32, (blk, blk), 1)
    causal = key <= qry
    bias_own = bias_ref[0]
    bias_prev = bias_ref[1]
    bias_far = bias_ref[2, 0:1, 0:1]
    bidx = lax.broadcasted_iota(jnp.int32, (gate_all.shape[0], blk), 0)

    for qi in range(nb):
        cols = slice(qi * blk, (qi + 1) * blk)
        nk = (qi + 1) * blk
        s = lax.dot_general(kb[0:nk, :], qb_all[cols, :], _NT, preferred_element_type=F32)

        keep = None
        if qi > MOBA_TOPK:
            gate = gate_all[:, cols]
            past = bidx < qi
            keep = []
            for j in range(qi):
                g_j = gate[j:j + 1, :]
                beats = ((gate > g_j) | ((gate == g_j) & (bidx < j))) & past
                keep.append(jnp.sum(beats.astype(F32), axis=0, keepdims=True) < MOBA_TOPK)

        pieces = []
        for j in range(qi + 1):
            s_j = s[j * blk:(j + 1) * blk, :]
            if j == qi:
                s_j = jnp.where(causal, s_j + bias_own, NEG_INF)
            elif j == qi - 1:
                s_j = s_j + bias_prev
                if keep is not None:
                    s_j = jnp.where(keep[j], s_j, NEG_INF)
            elif keep is not None:
                s_j = s_j + jnp.where(keep[j], bias_far, NEG_INF)
            else:
                s_j = s_j + bias_far
            pieces.append(s_j)

        m = pieces[0]
        for s_j in pieces[1:]:
            m = jnp.maximum(m, s_j)
        m = jnp.max(m, axis=0, keepdims=True)
        probs = [jnp.exp(s_j - m) for s_j in pieces]
        total = probs[0]
        for p_j in probs[1:]:
            total = total + p_j
        denom = jnp.sum(total, axis=0, keepdims=True)
        p_t = jnp.concatenate([p_j.astype(BF16) for p_j in probs], axis=0)
        o_t = jnp.dot(v_t[:, 0:nk], p_t, preferred_element_type=F32)
        o_ref[cols, :] = (o_t / denom).T.astype(o_ref.dtype)


def moba_attention(proj, bias, batch, seq, qkv_off):
    t = proj.shape[0]
    blk, dh, nh = MOBA_BLOCK, ATTN_HEAD_DIM, ATTN_HEADS
    nb = seq // blk
    qc = qkv_off // dh
    return pl.pallas_call(
        functools.partial(_attn_body, nb=nb),
        grid=(batch, nh),
        in_specs=[
            pl.BlockSpec((seq, dh), lambda b, h: (b, qc + h)),
            pl.BlockSpec((seq, dh), lambda b, h: (b, qc + nh + h)),
            pl.BlockSpec((seq, dh), lambda b, h: (b, qc + 2 * nh + h)),
            pl.BlockSpec((None, 3, blk, blk), lambda b, h: (h, 0, 0, 0)),
        ],
        out_specs=pl.BlockSpec((seq, dh), lambda b, h: (b, h)),
        out_shape=jax.ShapeDtypeStruct((t, nh * dh), BF16),
        scratch_shapes=[pltpu.VMEM((max(nb, SUBLANES), dh), F32)],
        compiler_params=_params("parallel", "parallel"),
        name="moba_attention",
    )(proj, proj, proj, bias)


def _split_dot(v, e):
    hi = v.astype(BF16)
    lo = (v - hi.astype(F32)).astype(BF16)
    return (jnp.dot(hi, e, preferred_element_type=F32) + jnp.dot(lo, e, preferred_element_type=F32))


def _mixer_body(xbc_ref, sc_ref, z_ref, dt_ref, cw_ref, cb_ref, dtb_ref, alog_ref, dskip_ref, ng_ref,
                scw_ref, e_ref, yssm_ref, ysc_ref, ext_ref, ext2_ref, xc_ref, state_ref):
    c = pl.program_id(1)
    L = SSM_CHUNK
    halo = SUBLANES
    gw = SSM_D_INNER // SSM_GROUPS
    hpg = SSM_HEADS // SSM_GROUPS
    n_state = SSM_STATE

    @pl.when(c == 0)
    def _():
        ext_ref[0:halo, :] = jnp.zeros((halo, ext_ref.shape[1]), F32)
        ext2_ref[0:halo, :] = jnp.zeros((halo, ext2_ref.shape[1]), F32)
        state_ref[...] = jnp.zeros(state_ref.shape, F32)

    @pl.when(c > 0)
    def _():
        ext_ref[0:halo, :] = ext_ref[L:L + halo, :]
        ext2_ref[0:halo, :] = ext2_ref[L:L + halo, :]

    ext_ref[halo:halo + L, :] = xbc_ref[...]
    cchunk = 512
    for ci in range(SSM_CONV_DIM // cchunk):
        cs = slice(ci * cchunk, (ci + 1) * cchunk)
        v = cb_ref[:, cs]
        for k in range(SSM_CONV):
            off = halo - (SSM_CONV - 1) + k
            v = v + cw_ref[k:k + 1, cs] * ext_ref[off:off + L, cs]
        xc_ref[:, cs] = v * _sigmoid(v)

    ext2_ref[halo:halo + L, :] = sc_ref[:, SC_WIDTH:2 * SC_WIDTH] * sc_ref[:, 2 * SC_WIDTH:3 * SC_WIDTH]
    conv = None
    for k in range(SC_CONV):
        off = halo - (SC_CONV - 1) + k
        term = scw_ref[k:k + 1, :] * ext2_ref[off:off + L, :]
        conv = term if conv is None else conv + term
    ysc_ref[...] = (sc_ref[:, 0:SC_WIDTH] * conv).astype(ysc_ref.dtype)

    rows = lax.broadcasted_iota(jnp.int32, (L, LANES), 0)
    head_lane = lax.broadcasted_iota(jnp.int32, (L, LANES), 1) < SSM_HEADS
    x_dt = jnp.where(head_lane, dt_ref[...], 0.0) + dtb_ref[...]
    dt = jnp.maximum(x_dt, 0.0) + jnp.log1p(jnp.exp(-jnp.abs(x_dt)))
    a = dt * (-jnp.exp(alog_ref[...]))
    a_cum = a
    sh = 1
    while sh < L:
        a_cum = a_cum + jnp.where(rows >= sh, pltpu.roll(a_cum, sh, axis=0), 0.0)
        sh *= 2
    a_cum_t = a_cum.T
    ea = jnp.exp(a_cum)
    ds = jnp.exp(a_cum[L - 1:L, :] - a_cum)

    row = lax.broadcasted_iota(jnp.int32, (L, L), 0)
    col = lax.broadcasted_iota(jnp.int32, (L, L), 1)
    causal = row >= col
    lane = lax.broadcasted_iota(jnp.int32, (L, LANES), 1)
    low_half = lane < SSM_HEAD_DIM

    for g in range(SSM_GROUPS):
        cs = slice(g * gw, (g + 1) * gw)
        b_g = xc_ref[:, SSM_D_INNER + g * n_state:SSM_D_INNER + (g + 1) * n_state]
        c_g = xc_ref[:, SSM_D_INNER + (SSM_GROUPS + g) * n_state:SSM_D_INNER + (SSM_GROUPS + g + 1) * n_state]
        b_gb = b_g.astype(BF16)
        c_gb = c_g.astype(BF16)
        cb = lax.dot_general(c_gb, b_gb, _NT, preferred_element_type=F32)
        e_g = e_ref[:, cs]
        xs_g = xc_ref[:, cs]
        ea_e = _split_dot(ea, e_g)
        xdt = xs_g * _split_dot(dt, e_g)
        xdt_b = xdt.astype(BF16)

        st = state_ref[g]
        y = jnp.dot(c_gb, st.astype(BF16), preferred_element_type=F32) * ea_e

        pieces = []
        for pe in range(hpg // 2):
            x_pair = xdt_b[:, pe * LANES:(pe + 1) * LANES]
            acc = None
            for half in range(2):
                h = g * hpg + 2 * pe + half
                seg = a_cum[:, h:h + 1] - a_cum_t[h:h + 1, :]
                decay = jnp.exp(jnp.where(causal, seg, -jnp.inf))
                m_h = (cb * decay).astype(BF16)
                keep = low_half if half == 0 else jnp.logical_not(low_half)
                x_h = jnp.where(keep, x_pair, jnp.zeros_like(x_pair))
                term = jnp.dot(m_h, x_h, preferred_element_type=F32)
                acc = term if acc is None else acc + term
            pieces.append(acc)
        y = y + jnp.concatenate(pieces, axis=1)

        xds = (xdt * _split_dot(ds, e_g)).astype(BF16)
        upd = jnp.dot(b_g.T.astype(BF16), xds, preferred_element_type=F32)
        state_ref[g] = st * ea_e[L - 1:L, :] + upd

        y = y + dskip_ref[:, cs] * xs_g
        z_g = z_ref[:, cs]
        y = y * (z_g * _sigmoid(z_g))
        y = y * lax.rsqrt(jnp.mean(y * y, axis=-1, keepdims=True) + NORM_EPS)
        yssm_ref[:, cs] = (y * ng_ref[:, cs]).astype(yssm_ref.dtype)


def _head_expand_matrix():
    e = np.zeros((LANES, SSM_D_INNER), np.float32)
    for h in range(SSM_HEADS):
        e[h, h * SSM_HEAD_DIM:(h + 1) * SSM_HEAD_DIM] = 1.0
    return jnp.asarray(e, BF16)


def ssd_and_short_conv(proj, conv_w, conv_b, dt_bias, a_log, d_skip, norm_g, sc_w, layer, batch, seq):
    t = proj.shape[0]
    L = SSM_CHUNK
    nc = seq // L
    row_map = lambda b, c: (b * nc + c, 0)

    def col_block(off, width):
        return pl.BlockSpec((L, width), lambda b, c: (b * nc + c, off // width))

    def param(rows, cols):
        return pl.BlockSpec((None, rows, cols), lambda b, c: (layer, 0, 0))

    return pl.pallas_call(
        _mixer_body,
        grid=(batch, nc),
        in_specs=[
            col_block(XBC_OFF, SSM_CONV_DIM),
            col_block(SC_OFF, 3 * SC_WIDTH),
            col_block(Z_OFF, SSM_D_INNER),
            col_block(DT_OFF, LANES),
            param(SSM_CONV, SSM_CONV_DIM),
            param(1, SSM_CONV_DIM),
            param(1, LANES),
            param(1, LANES),
            param(1, SSM_D_INNER),
            param(1, SSM_D_INNER),
            param(SC_CONV, SC_WIDTH),
            pl.BlockSpec((LANES, SSM_D_INNER), lambda b, c: (0, 0)),
        ],
        out_specs=[
            pl.BlockSpec((L, SSM_D_INNER), row_map),
            pl.BlockSpec((L, SC_WIDTH), row_map),
        ],
        out_shape=[
            jax.ShapeDtypeStruct((t, SSM_D_INNER), BF16),
            jax.ShapeDtypeStruct((t, SC_WIDTH), BF16),
        ],
        scratch_shapes=[
            pltpu.VMEM((L + 2 * SUBLANES, SSM_CONV_DIM), F32),
            pltpu.VMEM((L + 2 * SUBLANES, SC_WIDTH), F32),
            pltpu.VMEM((L, SSM_CONV_DIM), F32),
            pltpu.VMEM((SSM_GROUPS, SSM_STATE, SSM_D_INNER // SSM_GROUPS), F32),
        ],
        compiler_params=_params("parallel", "arbitrary"),
        name="ssd_short_conv",
    )(proj, proj, proj, proj, conv_w, conv_b, dt_bias, a_log, d_skip, norm_g, sc_w, _head_expand_matrix())


def _merge_body(ya_ref, ym_ref, yc_ref, ga_ref, gm_ref, gc_ref, wa_ref, wm_ref, wc_ref, o_ref):
    a = jnp.dot(ya_ref[...], wa_ref[...].astype(BF16), preferred_element_type=F32)
    m = jnp.dot(ym_ref[...], wm_ref[...].astype(BF16), preferred_element_type=F32)
    c = jnp.dot(yc_ref[...], wc_ref[...].astype(BF16), preferred_element_type=F32)
    o = _sigmoid(ga_ref[...]) * a + _sigmoid(gm_ref[...]) * m + _sigmoid(gc_ref[...]) * c
    o_ref[...] = o.astype(o_ref.dtype)


def branch_merge(y_attn, y_ssm, y_sc, proj, w_a, w_m, w_c, layer, tm=1024, tn=512):
    t = y_attn.shape[0]
    d = w_a.shape[-1]
    tm = min(tm, t)
    gc = GATE_OFF // tn
    gs = d // tn

    def act(width):
        return pl.BlockSpec((tm, width), lambda i, j: (i, 0))

    def gate(k):
        return pl.BlockSpec((tm, tn), lambda i, j: (i, gc + k * gs + j))

    def weight(rows):
        return pl.BlockSpec((None, rows, tn), lambda i, j: (layer, 0, j))

    return pl.pallas_call(
        _merge_body,
        grid=(t // tm, d // tn),
        in_specs=[act(y_attn.shape[1]), act(y_ssm.shape[1]), act(y_sc.shape[1]),
                  gate(0), gate(1), gate(2),
                  weight(w_a.shape[1]), weight(w_m.shape[1]), weight(w_c.shape[1])],
        out_specs=pl.BlockSpec((tm, tn), lambda i, j: (i, j)),
        out_shape=jax.ShapeDtypeStruct((t, d), BF16),
        compiler_params=_params("parallel", "arbitrary"),
        name="branch_merge",
    )(y_attn, y_ssm, y_sc, proj, proj, proj, w_a, w_m, w_c)


def _outproj_body(m_ref, x_ref, w_ref, gpost_ref, gnext_ref, o_ref, h_ref, wb_ref):
    @pl.when(pl.program_id(0) == 0)
    def _():
        wb_ref[...] = w_ref[...].astype(BF16)

    for rows in _row_chunks(m_ref.shape[0]):
        o = jnp.dot(m_ref[rows, :], wb_ref[...], preferred_element_type=F32)
        x = x_ref[rows, :] + _rms(o, gpost_ref[...])
        o_ref[rows, :] = x
        h_ref[rows, :] = _rms(x, gnext_ref[...]).astype(h_ref.dtype)


def out_projection(merged, x, w, g_post, g_next, layer, tm=512):
    t, d = x.shape
    tm = min(tm, t)
    rows = pl.BlockSpec((tm, d), lambda i: (i, 0))
    gain = pl.BlockSpec((None, 1, d), lambda i: (layer, 0, 0))
    return pl.pallas_call(
        _outproj_body,
        grid=(t // tm,),
        in_specs=[rows, rows, pl.BlockSpec((None, d, d), lambda i: (layer, 0, 0), pipeline_mode=pl.Buffered(1)),
                  gain, gain],
        out_specs=[rows, rows],
        out_shape=[jax.ShapeDtypeStruct((t, d), F32), jax.ShapeDtypeStruct((t, d), BF16)],
        scratch_shapes=[pltpu.VMEM((d, d), BF16)],
        compiler_params=_params("arbitrary"),
        name="out_projection",
    )(merged, x, w, g_post, g_next)


def _mlp_body(h_ref, x_ref, w1_ref, w2_ref, gpost_ref, o_ref):
    f = pl.program_id(1)

    @pl.when(f == 0)
    def _():
        o_ref[...] = jnp.zeros(o_ref.shape, F32)

    u = jnp.maximum(jnp.dot(h_ref[...], w1_ref[...], preferred_element_type=F32), 0.0)
    o_ref[...] += jnp.dot((u * u).astype(BF16), w2_ref[...], preferred_element_type=F32)

    @pl.when(f == pl.num_programs(1) - 1)
    def _():
        o_ref[...] = x_ref[...] + _rms(o_ref[...], gpost_ref[...])


def mlp(h, x, w1, w2, g_post, layer, tm=1024, tf=512):
    t, d = x.shape
    ff = w1.shape[-1]
    tm = min(tm, t)
    return pl.pallas_call(
        _mlp_body,
        grid=(t // tm, ff // tf),
        in_specs=[
            pl.BlockSpec((tm, d), lambda i, f: (i, 0)),
            pl.BlockSpec((tm, d), lambda i, f: (i, 0), pipeline_mode=pl.Buffered(1)),
            pl.BlockSpec((None, d, tf), lambda i, f: (layer, 0, f)),
            pl.BlockSpec((None, tf, d), lambda i, f: (layer, f, 0)),
            pl.BlockSpec((None, 1, d), lambda i, f: (layer, 0, 0)),
        ],
        out_specs=pl.BlockSpec((tm, d), lambda i, f: (i, 0)),
        out_shape=jax.ShapeDtypeStruct((t, d), F32),
        compiler_params=_params("parallel", "arbitrary"),
        name="mlp",
    )(h, x, w1, w2, g_post)


def _ple_body(x_ref, p_ref, gpre_ref, wg_ref, wp_ref, gpost_ref, *rest):
    wgb_ref, wpb_ref = rest[-2:]

    @pl.when(pl.program_id(0) == 0)
    def _():
        wgb_ref[...] = wg_ref[...].astype(BF16)
        wpb_ref[...] = wp_ref[...].astype(BF16)

    for rows in _row_chunks(x_ref.shape[0]):
        x = x_ref[rows, :]
        h = _rms(x, gpre_ref[...]).astype(BF16)
        gate = _sigmoid(jnp.dot(h, wgb_ref[...], preferred_element_type=F32))
        emb = jnp.dot(p_ref[rows, :].astype(BF16), wpb_ref[...], preferred_element_type=F32)
        x = x + _rms(gate * emb, gpost_ref[...])
        if len(rest) == 3:
            o_ref = rest[0]
        else:
            gnext_ref, o_ref, h_ref = rest[:3]
            h_ref[rows, :] = _rms(x, gnext_ref[...]).astype(h_ref.dtype)
        o_ref[rows, :] = x


def per_layer_embedding(x, p, g_pre, w_gate, w_proj, g_post, layer, g_next=None, tm=512):
    t, d = x.shape
    pd = p.shape[-1]
    tm = min(tm, t)
    rows = pl.BlockSpec((tm, d), lambda i: (i, 0))
    gain = pl.BlockSpec((None, 1, d), lambda i: (layer, 0, 0))
    in_specs = [
        rows,
        pl.BlockSpec((None, tm, pd), lambda i: (layer, i, 0)),
        gain,
        pl.BlockSpec((None, d, d), lambda i: (layer, 0, 0), pipeline_mode=pl.Buffered(1)),
        pl.BlockSpec((None, pd, d), lambda i: (layer, 0, 0), pipeline_mode=pl.Buffered(1)),
        gain,
    ]
    args = [x, p, g_pre, w_gate, w_proj, g_post]
    out_specs = [rows]
    out_shape = [jax.ShapeDtypeStruct((t, d), F32)]
    if g_next is not None:
        in_specs.append(pl.BlockSpec((None, 1, d), lambda i: (layer + 1, 0, 0)))
        args.append(g_next)
        out_specs.append(rows)
        out_shape.append(jax.ShapeDtypeStruct((t, d), BF16))
    return pl.pallas_call(
        _ple_body,
        grid=(t // tm,),
        in_specs=in_specs,
        out_specs=out_specs,
        out_shape=out_shape,
        scratch_shapes=[pltpu.VMEM((d, d), BF16), pltpu.VMEM((pd, d), BF16)],
        compiler_params=_params("arbitrary"),
        name="per_layer_embedding",
    )(*args)


def _row(v):
    return v[:, None, :]


def _lane_pad(v):
    return jnp.pad(v, ((0, 0), (0, LANES - v.shape[-1])))[:, None, :]


def kernel(x, p, rel_bias, g_mix_pre, w_in, ssm_conv_w, ssm_conv_b, ssm_dt_bias, ssm_a_log, ssm_d, ssm_norm_g, sc_conv_w, w_br_attn, w_br_ssm, w_br_conv, w_out, g_mix_post, g_mlp_pre, w_mlp_up, w_mlp_down, g_mlp_post, g_ple_pre, w_ple_gate, w_ple_proj, g_ple_post):
    batch, seq, d = x.shape
    depth = w_in.shape[0]
    t = batch * seq

    w_tail = w_in[..., _O_SC:_O_END]
    w1, w2 = w_mlp_up.astype(BF16), w_mlp_down.astype(BF16)
    d_skip = jnp.repeat(ssm_d, SSM_HEAD_DIM, axis=-1)[:, None, :]
    dt_bias, a_log = _lane_pad(ssm_dt_bias), _lane_pad(ssm_a_log)
    conv_b, norm_g = _row(ssm_conv_b), _row(ssm_norm_g)
    g_mix_pre, g_mix_post = _row(g_mix_pre), _row(g_mix_post)
    g_mlp_pre, g_mlp_post = _row(g_mlp_pre), _row(g_mlp_post)
    g_ple_pre, g_ple_post = _row(g_ple_pre), _row(g_ple_post)

    bias = bias_tiles(rel_bias)
    xf = x.reshape(t, d)
    pf = p.reshape(depth, t, p.shape[-1])
    h = pre_norm(xf, g_mix_pre, 0)
    for i in range(depth):
        proj = in_projection(h, w_in, w_tail, i)
        y_attn = moba_attention(proj, bias, batch, seq, QKV_OFF)
        y_ssm, y_sc = ssd_and_short_conv(proj, ssm_conv_w, conv_b, dt_bias, a_log, d_skip, norm_g,
                                         sc_conv_w, i, batch, seq)
        merged = branch_merge(y_attn, y_ssm, y_sc, proj, w_br_attn, w_br_ssm, w_br_conv, i)
        xf, h_mlp = out_projection(merged, xf, w_out, g_mix_post, g_mlp_pre, i)
        xf = mlp(h_mlp, xf, w1, w2, g_mlp_post, i)
        if i + 1 < depth:
            xf, h = per_layer_embedding(xf, pf, g_ple_pre, w_ple_gate, w_ple_proj, g_ple_post, i, g_next=g_mix_pre)
        else:
            (xf,) = per_layer_embedding(xf, pf, g_ple_pre, w_ple_gate, w_ple_proj, g_ple_post, i)
    return xf.reshape(batch, seq, d)
```

```python
import functools
import math

import numpy as np
import jax
import jax.numpy as jnp
from jax import lax
from jax.experimental import pallas as pl
from jax.experimental.pallas import tpu as pltpu

F32 = jnp.float32
BF16 = jnp.bfloat16

NORM_EPS = 1e-6
NEG_INF = -1e30

D_MODEL = 2048
PLE_DIM = 256
ATTN_HEADS = 8
ATTN_HEAD_DIM = 128
ATTN_WIDTH = ATTN_HEADS * ATTN_HEAD_DIM
MOBA_BLOCK = 256
MOBA_TOPK = 3
REL_BUCKETS = 32
REL_MAX_DISTANCE = 128
SSM_D_INNER = D_MODEL
SSM_HEAD_DIM = 64
SSM_HEADS = SSM_D_INNER // SSM_HEAD_DIM
SSM_GROUPS = 4
SSM_STATE = 128
SSM_CONV = 4
SSM_CHUNK = 256
SSM_CONV_DIM = SSM_D_INNER + 2 * SSM_GROUPS * SSM_STATE
SC_WIDTH = D_MODEL // 2
SC_CONV = 3
D_FF = 4 * D_MODEL

LANES = 128
SUBLANES = 8
VMEM_LIMIT_BYTES = 60 * 1024 * 1024

XBC_OFF = 0
SC_OFF = XBC_OFF + SSM_CONV_DIM
GATE_OFF = SC_OFF + 3 * SC_WIDTH
Z_OFF = GATE_OFF + 3 * D_MODEL
QKV_OFF = Z_OFF + SSM_D_INNER
DT_OFF = QKV_OFF + 3 * ATTN_WIDTH
PROJ_TN = 512
PROJ_WIDTH = DT_OFF + PROJ_TN

_O_Q = 0
_O_Z = 3 * ATTN_WIDTH
_O_XBC = _O_Z + SSM_D_INNER
_O_DT = _O_XBC + SSM_CONV_DIM
_O_SC = _O_DT + SSM_HEADS
_O_GATE = _O_SC + 3 * SC_WIDTH
_O_END = _O_GATE + 3 * D_MODEL

_NT = (((1,), (1,)), ((), ()))


def _params(*sem):
    return pltpu.CompilerParams(dimension_semantics=sem, vmem_limit_bytes=VMEM_LIMIT_BYTES)


def _rms(x, g):
    return x * lax.rsqrt(jnp.mean(x * x, axis=-1, keepdims=True) + NORM_EPS) * g


def _sigmoid(x):
    return 1.0 / (1.0 + jnp.exp(-x))


def _row_chunks(n, parts=4):
    step = n // parts
    return [slice(k * step, (k + 1) * step) for k in range(parts)]


_T = PROJ_TN
_N_XBC, _N_TAIL, _N_Z, _N_QKV = SSM_CONV_DIM // _T, (_O_END - _O_SC) // _T, SSM_D_INNER // _T, 3 * ATTN_WIDTH // _T
_J_TAIL, _J_Z, _J_QKV, _J_DT = _N_XBC, _N_XBC + _N_TAIL, _N_XBC + _N_TAIL + _N_Z, _N_XBC + _N_TAIL + _N_Z + _N_QKV
assert all(o % SUBLANES == 0 for o in (_O_XBC, _O_SC, _O_Z, _O_Q, _O_DT)) and _T % SUBLANES == 0


def _src_row(j):
    s, ts = SUBLANES, _T // SUBLANES
    groups = jnp.where(j < _J_TAIL, _O_XBC // s + ts * j,
                       jnp.where(j < _J_Z, _O_SC // s + ts * (j - _J_TAIL),
                                 jnp.where(j < _J_QKV, _O_Z // s + ts * (j - _J_Z),
                                           jnp.where(j < _J_DT, _O_Q // s + ts * (j - _J_QKV), _O_DT // s))))
    return groups * s


def _norm_body(x_ref, g_ref, o_ref):
    o_ref[...] = _rms(x_ref[...], g_ref[...]).astype(o_ref.dtype)


def pre_norm(x, g, layer, tm=512):
    t, d = x.shape
    tm = min(tm, t)
    return pl.pallas_call(
        _norm_body,
        grid=(t // tm,),
        in_specs=[pl.BlockSpec((tm, d), lambda i: (i, 0)),
                  pl.BlockSpec((None, 1, d), lambda i: (layer, 0, 0))],
        out_specs=pl.BlockSpec((tm, d), lambda i: (i, 0)),
        out_shape=jax.ShapeDtypeStruct((t, d), BF16),
        compiler_params=_params("parallel"),
        name="pre_norm",
    )(x, g)


def _inproj_body(h_ref, wt_ref, o_ref):
    o_ref[...] = lax.dot_general(h_ref[...], wt_ref[0].astype(BF16), _NT, preferred_element_type=F32)


def in_projection(h, w_t, layer, tm=2048):
    t, d = h.shape
    tn = _T
    tm = min(tm, t)
    return pl.pallas_call(
        _inproj_body,
        grid=(t // tm, PROJ_WIDTH // tn),
        in_specs=[
            pl.BlockSpec((tm, d), lambda i, j: (i, 0)),
            pl.BlockSpec((pl.Element(1), pl.Element(tn), pl.Element(d)), lambda i, j: (layer, _src_row(j), 0)),
        ],
        out_specs=pl.BlockSpec((tm, tn), lambda i, j: (i, j)),
        out_shape=jax.ShapeDtypeStruct((t, PROJ_WIDTH), F32),
        compiler_params=_params("parallel", "arbitrary"),
        name="in_projection",
    )(h, w_t)


def _bias_body(tbl_ref, o_ref):
    h = pl.program_id(0)
    blk = MOBA_BLOCK
    key = lax.broadcasted_iota(jnp.int32, (blk, blk), 0)
    qry = lax.broadcasted_iota(jnp.int32, (blk, blk), 1)
    max_exact = REL_BUCKETS // 2
    for t in range(3):
        n = jnp.maximum(t * blk + qry - key, 0)
        nf = jnp.maximum(n, 1).astype(F32)
        large = max_exact + (jnp.log(nf / max_exact) / math.log(REL_MAX_DISTANCE / max_exact)
                             * (REL_BUCKETS - max_exact)).astype(jnp.int32)
        large = jnp.minimum(large, REL_BUCKETS - 1)
        bucket = jnp.where(n < max_exact, n, large)
        acc = jnp.zeros((blk, blk), F32)
        for b in range(REL_BUCKETS):
            acc = jnp.where(bucket == b, tbl_ref[b, h], acc)
        o_ref[t] = acc


def bias_tiles(rel_bias):
    nh = rel_bias.shape[1]
    return pl.pallas_call(
        _bias_body,
        grid=(nh,),
        in_specs=[pl.BlockSpec(memory_space=pltpu.SMEM)],
        out_specs=pl.BlockSpec((None, 3, MOBA_BLOCK, MOBA_BLOCK), lambda h: (h, 0, 0, 0)),
        out_shape=jax.ShapeDtypeStruct((nh, 3, MOBA_BLOCK, MOBA_BLOCK), F32),
        compiler_params=_params("arbitrary"),
        name="bias_tiles",
    )(rel_bias)


def _attn_body(q_ref, k_ref, v_ref, bias_ref, o_ref, kmean_ref, *, nb):
    blk = MOBA_BLOCK
    scale = ATTN_HEAD_DIM ** -0.5

    k = k_ref[...]
    kb = k.astype(BF16)
    v_t = v_ref[...].T.astype(BF16)
    for j in range(nb):
        kmean_ref[j:j + 1, :] = jnp.mean(k[j * blk:(j + 1) * blk, :], axis=0, keepdims=True)
    q_all = q_ref[...]
    gate_all = lax.dot_general(kmean_ref[...], q_all, _NT, precision=lax.Precision.HIGHEST,
                               preferred_element_type=F32)
    qb_all = (q_all * scale).astype(BF16)

    key = lax.broadcasted_iota(jnp.int32, (blk, blk), 0)
    qry = lax.broadcasted_iota(jnp.int32, (blk, blk), 1)
    causal = key <= qry
    bias_own = bias_ref[0]
    bias_prev = bias_ref[1]
    bias_far = bias_ref[2, 0:1, 0:1]
    bidx = lax.broadcasted_iota(jnp.int32, (gate_all.shape[0], blk), 0)

    for qi in range(nb):
        cols = slice(qi * blk, (qi + 1) * blk)
        nk = (qi + 1) * blk
        s = lax.dot_general(kb[0:nk, :], qb_all[cols, :], _NT, preferred_element_type=F32)

        keep = None
        if qi > MOBA_TOPK:
            gate = gate_all[:, cols]
            past = bidx < qi
            keep = []
            for j in range(qi):
                g_j = gate[j:j + 1, :]
                beats = ((gate > g_j) | ((gate == g_j) & (bidx < j))) & past
                keep.append(jnp.sum(beats.astype(F32), axis=0, keepdims=True) < MOBA_TOPK)

        pieces = []
        for j in range(qi + 1):
            s_j = s[j * blk:(j + 1) * blk, :]
            if j == qi:
                s_j = jnp.where(causal, s_j + bias_own, NEG_INF)
            elif j == qi - 1:
                s_j = s_j + bias_prev
                if keep is not None:
                    s_j = jnp.where(keep[j], s_j, NEG_INF)
            elif keep is not None:
                s_j = s_j + jnp.where(keep[j], bias_far, NEG_INF)
            else:
                s_j = s_j + bias_far
            pieces.append(s_j)

        m = pieces[0]
        for s_j in pieces[1:]:
            m = jnp.maximum(m, s_j)
        m = jnp.max(m, axis=0, keepdims=True)
        probs = [jnp.exp(s_j - m) for s_j in pieces]
        total = probs[0]
        for p_j in probs[1:]:
            total = total + p_j
        denom = jnp.sum(total, axis=0, keepdims=True)
        p_t = jnp.concatenate([p_j.astype(BF16) for p_j in probs], axis=0)
        o_t = jnp.dot(v_t[:, 0:nk], p_t, preferred_element_type=F32)
        o_ref[cols, :] = (o_t / denom).T.astype(o_ref.dtype)


def moba_attention(proj, bias, batch, seq, qkv_off):
    t = proj.shape[0]
    blk, dh, nh = MOBA_BLOCK, ATTN_HEAD_DIM, ATTN_HEADS
    nb = seq // blk
    qc = qkv_off // dh
    return pl.pallas_call(
        functools.partial(_attn_body, nb=nb),
        grid=(batch, nh),
        in_specs=[
            pl.BlockSpec((seq, dh), lambda b, h: (b, qc + h)),
            pl.BlockSpec((seq, dh), lambda b, h: (b, qc + nh + h)),
            pl.BlockSpec((seq, dh), lambda b, h: (b, qc + 2 * nh + h)),
            pl.BlockSpec((None, 3, blk, blk), lambda b, h: (h, 0, 0, 0)),
        ],
        out_specs=pl.BlockSpec((seq, dh), lambda b, h: (b, h)),
        out_shape=jax.ShapeDtypeStruct((t, nh * dh), BF16),
        scratch_shapes=[pltpu.VMEM((max(nb, SUBLANES), dh), F32)],
        compiler_params=_params("parallel", "parallel"),
        name="moba_attention",
    )(proj, proj, proj, bias)


def _split_dot(v, e):
    hi = v.astype(BF16)
    lo = (v - hi.astype(F32)).astype(BF16)
    return (jnp.dot(hi, e, preferred_element_type=F32) + jnp.dot(lo, e, preferred_element_type=F32))


def _mixer_body(xbc_ref, sc_ref, z_ref, dt_ref, cw_ref, cb_ref, dtb_ref, alog_ref, dskip_ref, ng_ref,
                scw_ref, e_ref, yssm_ref, ysc_ref, ext_ref, ext2_ref, xc_ref, state_ref):
    c = pl.program_id(1)
    L = SSM_CHUNK
    halo = SUBLANES
    gw = SSM_D_INNER // SSM_GROUPS
    hpg = SSM_HEADS // SSM_GROUPS
    n_state = SSM_STATE

    @pl.when(c == 0)
    def _():
        ext_ref[0:halo, :] = jnp.zeros((halo, ext_ref.shape[1]), F32)
        ext2_ref[0:halo, :] = jnp.zeros((halo, ext2_ref.shape[1]), F32)
        state_ref[...] = jnp.zeros(state_ref.shape, F32)

    @pl.when(c > 0)
    def _():
        ext_ref[0:halo, :] = ext_ref[L:L + halo, :]
        ext2_ref[0:halo, :] = ext2_ref[L:L + halo, :]

    ext_ref[halo:halo + L, :] = xbc_ref[...]
    cchunk = 512
    for ci in range(SSM_CONV_DIM // cchunk):
        cs = slice(ci * cchunk, (ci + 1) * cchunk)
        v = cb_ref[:, cs]
        for k in range(SSM_CONV):
            off = halo - (SSM_CONV - 1) + k
            v = v + cw_ref[k:k + 1, cs] * ext_ref[off:off + L, cs]
        xc_ref[:, cs] = v * _sigmoid(v)

    ext2_ref[halo:halo + L, :] = sc_ref[:, SC_WIDTH:2 * SC_WIDTH] * sc_ref[:, 2 * SC_WIDTH:3 * SC_WIDTH]
    conv = None
    for k in range(SC_CONV):
        off = halo - (SC_CONV - 1) + k
        term = scw_ref[k:k + 1, :] * ext2_ref[off:off + L, :]
        conv = term if conv is None else conv + term
    ysc_ref[...] = (sc_ref[:, 0:SC_WIDTH] * conv).astype(ysc_ref.dtype)

    rows = lax.broadcasted_iota(jnp.int32, (L, LANES), 0)
    head_lane = lax.broadcasted_iota(jnp.int32, (L, LANES), 1) < SSM_HEADS
    x_dt = jnp.where(head_lane, dt_ref[...], 0.0) + dtb_ref[...]
    dt = jnp.maximum(x_dt, 0.0) + jnp.log1p(jnp.exp(-jnp.abs(x_dt)))
    a = dt * (-jnp.exp(alog_ref[...]))
    a_cum = a
    sh = 1
    while sh < L:
        a_cum = a_cum + jnp.where(rows >= sh, pltpu.roll(a_cum, sh, axis=0), 0.0)
        sh *= 2
    a_cum_t = a_cum.T
    ea = jnp.exp(a_cum)
    ds = jnp.exp(a_cum[L - 1:L, :] - a_cum)

    row = lax.broadcasted_iota(jnp.int32, (L, L), 0)
    col = lax.broadcasted_iota(jnp.int32, (L, L), 1)
    causal = row >= col
    lane = lax.broadcasted_iota(jnp.int32, (L, LANES), 1)
    low_half = lane < SSM_HEAD_DIM

    for g in range(SSM_GROUPS):
        cs = slice(g * gw, (g + 1) * gw)
        b_g = xc_ref[:, SSM_D_INNER + g * n_state:SSM_D_INNER + (g + 1) * n_state]
        c_g = xc_ref[:, SSM_D_INNER + (SSM_GROUPS + g) * n_state:SSM_D_INNER + (SSM_GROUPS + g + 1) * n_state]
        b_gb = b_g.astype(BF16)
        c_gb = c_g.astype(BF16)
        cb = lax.dot_general(c_gb, b_gb, _NT, preferred_element_type=F32)
        e_g = e_ref[:, cs]
        xs_g = xc_ref[:, cs]
        ea_e = _split_dot(ea, e_g)
        xdt = xs_g * _split_dot(dt, e_g)
        xdt_b = xdt.astype(BF16)

        st = state_ref[g]
        y = jnp.dot(c_gb, st.astype(BF16), preferred_element_type=F32) * ea_e

        pieces = []
        for pe in range(hpg // 2):
            x_pair = xdt_b[:, pe * LANES:(pe + 1) * LANES]
            acc = None
            for half in range(2):
                h = g * hpg + 2 * pe + half
                seg = a_cum[:, h:h + 1] - a_cum_t[h:h + 1, :]
                decay = jnp.exp(jnp.where(causal, seg, -jnp.inf))
                m_h = (cb * decay).astype(BF16)
                keep = low_half if half == 0 else jnp.logical_not(low_half)
                x_h = jnp.where(keep, x_pair, jnp.zeros_like(x_pair))
                term = jnp.dot(m_h, x_h, preferred_element_type=F32)
                acc = term if acc is None else acc + term
            pieces.append(acc)
        y = y + jnp.concatenate(pieces, axis=1)

        xds = (xdt * _split_dot(ds, e_g)).astype(BF16)
        upd = jnp.dot(b_g.T.astype(BF16), xds, preferred_element_type=F32)
        state_ref[g] = st * ea_e[L - 1:L, :] + upd

        y = y + dskip_ref[:, cs] * xs_g
        z_g = z_ref[:, cs]
        y = y * (z_g * _sigmoid(z_g))
        y = y * lax.rsqrt(jnp.mean(y * y, axis=-1, keepdims=True) + NORM_EPS)
        yssm_ref[:, cs] = (y * ng_ref[:, cs]).astype(yssm_ref.dtype)


def _head_expand_matrix():
    e = np.zeros((LANES, SSM_D_INNER), np.float32)
    for h in range(SSM_HEADS):
        e[h, h * SSM_HEAD_DIM:(h + 1) * SSM_HEAD_DIM] = 1.0
    return jnp.asarray(e, BF16)


def ssd_and_short_conv(proj, conv_w, conv_b, dt_bias, a_log, d_skip, norm_g, sc_w, layer, batch, seq):
    t = proj.shape[0]
    L = SSM_CHUNK
    nc = seq // L
    row_map = lambda b, c: (b * nc + c, 0)

    def col_block(off, width):
        return pl.BlockSpec((L, width), lambda b, c: (b * nc + c, off // width))

    def param(rows, cols):
        return pl.BlockSpec((None, rows, cols), lambda b, c: (layer, 0, 0))

    return pl.pallas_call(
        _mixer_body,
        grid=(batch, nc),
        in_specs=[
            col_block(XBC_OFF, SSM_CONV_DIM),
            col_block(SC_OFF, 3 * SC_WIDTH),
            col_block(Z_OFF, SSM_D_INNER),
            col_block(DT_OFF, LANES),
            param(SSM_CONV, SSM_CONV_DIM),
            param(1, SSM_CONV_DIM),
            param(1, LANES),
            param(1, LANES),
            param(1, SSM_D_INNER),
            param(1, SSM_D_INNER),
            param(SC_CONV, SC_WIDTH),
            pl.BlockSpec((LANES, SSM_D_INNER), lambda b, c: (0, 0)),
        ],
        out_specs=[
            pl.BlockSpec((L, SSM_D_INNER), row_map),
            pl.BlockSpec((L, SC_WIDTH), row_map),
        ],
        out_shape=[
            jax.ShapeDtypeStruct((t, SSM_D_INNER), BF16),
            jax.ShapeDtypeStruct((t, SC_WIDTH), BF16),
        ],
        scratch_shapes=[
            pltpu.VMEM((L + 2 * SUBLANES, SSM_CONV_DIM), F32),
            pltpu.VMEM((L + 2 * SUBLANES, SC_WIDTH), F32),
            pltpu.VMEM((L, SSM_CONV_DIM), F32),
            pltpu.VMEM((SSM_GROUPS, SSM_STATE, SSM_D_INNER // SSM_GROUPS), F32),
        ],
        compiler_params=_params("parallel", "arbitrary"),
        name="ssd_short_conv",
    )(proj, proj, proj, proj, conv_w, conv_b, dt_bias, a_log, d_skip, norm_g, sc_w, _head_expand_matrix())


def _merge_body(ya_ref, ym_ref, yc_ref, ga_ref, gm_ref, gc_ref, wa_ref, wm_ref, wc_ref, o_ref):
    a = jnp.dot(ya_ref[...], wa_ref[...].astype(BF16), preferred_element_type=F32)
    m = jnp.dot(ym_ref[...], wm_ref[...].astype(BF16), preferred_element_type=F32)
    c = jnp.dot(yc_ref[...], wc_ref[...].astype(BF16), preferred_element_type=F32)
    o = _sigmoid(ga_ref[...]) * a + _sigmoid(gm_ref[...]) * m + _sigmoid(gc_ref[...]) * c
    o_ref[...] = o.astype(o_ref.dtype)


def branch_merge(y_attn, y_ssm, y_sc, proj, w_a, w_m, w_c, layer, tm=1024, tn=512):
    t = y_attn.shape[0]
    d = w_a.shape[-1]
    tm = min(tm, t)
    gc = GATE_OFF // tn
    gs = d // tn

    def act(width):
        return pl.BlockSpec((tm, width), lambda i, j: (i, 0))

    def gate(k):
        return pl.BlockSpec((tm, tn), lambda i, j: (i, gc + k * gs + j))

    def weight(rows):
        return pl.BlockSpec((None, rows, tn), lambda i, j: (layer, 0, j))

    return pl.pallas_call(
        _merge_body,
        grid=(t // tm, d // tn),
        in_specs=[act(y_attn.shape[1]), act(y_ssm.shape[1]), act(y_sc.shape[1]),
                  gate(0), gate(1), gate(2),
                  weight(w_a.shape[1]), weight(w_m.shape[1]), weight(w_c.shape[1])],
        out_specs=pl.BlockSpec((tm, tn), lambda i, j: (i, j)),
        out_shape=jax.ShapeDtypeStruct((t, d), BF16),
        compiler_params=_params("parallel", "arbitrary"),
        name="branch_merge",
    )(y_attn, y_ssm, y_sc, proj, proj, proj, w_a, w_m, w_c)


def _outproj_body(m_ref, x_ref, w_ref, gpost_ref, gnext_ref, o_ref, h_ref, wb_ref):
    @pl.when(pl.program_id(0) == 0)
    def _():
        wb_ref[...] = w_ref[...].astype(BF16)

    for rows in _row_chunks(m_ref.shape[0]):
        o = jnp.dot(m_ref[rows, :], wb_ref[...], preferred_element_type=F32)
        x = x_ref[rows, :] + _rms(o, gpost_ref[...])
        o_ref[rows, :] = x
        h_ref[rows, :] = _rms(x, gnext_ref[...]).astype(h_ref.dtype)


def out_projection(merged, x, w, g_post, g_next, layer, tm=512):
    t, d = x.shape
    tm = min(tm, t)
    rows = pl.BlockSpec((tm, d), lambda i: (i, 0))
    gain = pl.BlockSpec((None, 1, d), lambda i: (layer, 0, 0))
    return pl.pallas_call(
        _outproj_body,
        grid=(t // tm,),
        in_specs=[rows, rows, pl.BlockSpec((None, d, d), lambda i: (layer, 0, 0), pipeline_mode=pl.Buffered(1)),
                  gain, gain],
        out_specs=[rows, rows],
        out_shape=[jax.ShapeDtypeStruct((t, d), F32), jax.ShapeDtypeStruct((t, d), BF16)],
        scratch_shapes=[pltpu.VMEM((d, d), BF16)],
        compiler_params=_params("arbitrary"),
        name="out_projection",
    )(merged, x, w, g_post, g_next)


def _mlp_body(h_ref, x_ref, w1_ref, w2_ref, gpost_ref, o_ref):
    f = pl.program_id(1)

    @pl.when(f == 0)
    def _():
        o_ref[...] = jnp.zeros(o_ref.shape, F32)

    u = jnp.maximum(jnp.dot(h_ref[...], w1_ref[...].astype(BF16), preferred_element_type=F32), 0.0)
    o_ref[...] += jnp.dot((u * u).astype(BF16), w2_ref[...].astype(BF16), preferred_element_type=F32)

    @pl.when(f == pl.num_programs(1) - 1)
    def _():
        o_ref[...] = x_ref[...] + _rms(o_ref[...], gpost_ref[...])


def mlp(h, x, w1, w2, g_post, layer, tm=1024, tf=512):
    t, d = x.shape
    ff = w1.shape[-1]
    tm = min(tm, t)
    return pl.pallas_call(
        _mlp_body,
        grid=(t // tm, ff // tf),
        in_specs=[
            pl.BlockSpec((tm, d), lambda i, f: (i, 0)),
            pl.BlockSpec((tm, d), lambda i, f: (i, 0), pipeline_mode=pl.Buffered(1)),
            pl.BlockSpec((None, d, tf), lambda i, f: (layer, 0, f)),
            pl.BlockSpec((None, tf, d), lambda i, f: (layer, f, 0)),
            pl.BlockSpec((None, 1, d), lambda i, f: (layer, 0, 0)),
        ],
        out_specs=pl.BlockSpec((tm, d), lambda i, f: (i, 0)),
        out_shape=jax.ShapeDtypeStruct((t, d), F32),
        compiler_params=_params("parallel", "arbitrary"),
        name="mlp",
    )(h, x, w1, w2, g_post)


def _ple_body(x_ref, p_ref, gpre_ref, wg_ref, wp_ref, gpost_ref, *rest):
    wgb_ref, wpb_ref = rest[-2:]

    @pl.when(pl.program_id(0) == 0)
    def _():
        wgb_ref[...] = wg_ref[...].astype(BF16)
        wpb_ref[...] = wp_ref[...].astype(BF16)

    for rows in _row_chunks(x_ref.shape[0]):
        x = x_ref[rows, :]
        h = _rms(x, gpre_ref[...]).astype(BF16)
        gate = _sigmoid(jnp.dot(h, wgb_ref[...], preferred_element_type=F32))
        emb = jnp.dot(p_ref[rows, :].astype(BF16), wpb_ref[...], preferred_element_type=F32)
        x = x + _rms(gate * emb, gpost_ref[...])
        if len(rest) == 3:
            o_ref = rest[0]
        else:
            gnext_ref, o_ref, h_ref = rest[:3]
            h_ref[rows, :] = _rms(x, gnext_ref[...]).astype(h_ref.dtype)
        o_ref[rows, :] = x


def per_layer_embedding(x, p, g_pre, w_gate, w_proj, g_post, layer, g_next=None, tm=512):
    t, d = x.shape
    pd = p.shape[-1]
    tm = min(tm, t)
    rows = pl.BlockSpec((tm, d), lambda i: (i, 0))
    gain = pl.BlockSpec((None, 1, d), lambda i: (layer, 0, 0))
    in_specs = [
        rows,
        pl.BlockSpec((None, tm, pd), lambda i: (layer, i, 0)),
        gain,
        pl.BlockSpec((None, d, d), lambda i: (layer, 0, 0), pipeline_mode=pl.Buffered(1)),
        pl.BlockSpec((None, pd, d), lambda i: (layer, 0, 0), pipeline_mode=pl.Buffered(1)),
        gain,
    ]
    args = [x, p, g_pre, w_gate, w_proj, g_post]
    out_specs = [rows]
    out_shape = [jax.ShapeDtypeStruct((t, d), F32)]
    if g_next is not None:
        in_specs.append(pl.BlockSpec((None, 1, d), lambda i: (layer + 1, 0, 0)))
        args.append(g_next)
        out_specs.append(rows)
        out_shape.append(jax.ShapeDtypeStruct((t, d), BF16))
    return pl.pallas_call(
        _ple_body,
        grid=(t // tm,),
        in_specs=in_specs,
        out_specs=out_specs,
        out_shape=out_shape,
        scratch_shapes=[pltpu.VMEM((d, d), BF16), pltpu.VMEM((pd, d), BF16)],
        compiler_params=_params("arbitrary"),
        name="per_layer_embedding",
    )(*args)


def _row(v):
    return v[:, None, :]


def _lane_pad(v):
    return jnp.pad(v, ((0, 0), (0, LANES - v.shape[-1])))[:, None, :]


def kernel(x, p, rel_bias, g_mix_pre, w_in, ssm_conv_w, ssm_conv_b, ssm_dt_bias, ssm_a_log, ssm_d, ssm_norm_g, sc_conv_w, w_br_attn, w_br_ssm, w_br_conv, w_out, g_mix_post, g_mlp_pre, w_mlp_up, w_mlp_down, g_mlp_post, g_ple_pre, w_ple_gate, w_ple_proj, g_ple_post):
    batch, seq, d = x.shape
    depth = w_in.shape[0]
    t = batch * seq

    w_in_t = jnp.swapaxes(w_in, 1, 2)
    d_skip = jnp.repeat(ssm_d, SSM_HEAD_DIM, axis=-1)[:, None, :]
    dt_bias, a_log = _lane_pad(ssm_dt_bias), _lane_pad(ssm_a_log)
    conv_b, norm_g = _row(ssm_conv_b), _row(ssm_norm_g)
    g_mix_pre, g_mix_post = _row(g_mix_pre), _row(g_mix_post)
    g_mlp_pre, g_mlp_post = _row(g_mlp_pre), _row(g_mlp_post)
    g_ple_pre, g_ple_post = _row(g_ple_pre), _row(g_ple_post)

    bias = bias_tiles(rel_bias)
    xf = x.reshape(t, d)
    pf = p.reshape(depth, t, p.shape[-1])
    h = pre_norm(xf, g_mix_pre, 0)
    for i in range(depth):
        proj = in_projection(h, w_in_t, i)
        y_attn = moba_attention(proj, bias, batch, seq, QKV_OFF)
        y_ssm, y_sc = ssd_and_short_conv(proj, ssm_conv_w, conv_b, dt_bias, a_log, d_skip, norm_g,
                                         sc_conv_w, i, batch, seq)
        merged = branch_merge(y_attn, y_ssm, y_sc, proj, w_br_attn, w_br_ssm, w_br_conv, i)
        xf, h_mlp = out_projection(merged, xf, w_out, g_mix_post, g_mlp_pre, i)
        xf = mlp(h_mlp, xf, w_mlp_up, w_mlp_down, g_mlp_post, i)
        if i + 1 < depth:
            xf, h = per_layer_embedding(xf, pf, g_ple_pre, w_ple_gate, w_ple_proj, g_ple_post, i, g_next=g_mix_pre)
        else:
            (xf,) = per_layer_embedding(xf, pf, g_ple_pre, w_ple_gate, w_ple_proj, g_ple_post, i)
    return xf.reshape(batch, seq, d)
```

```python
import functools
import math

import numpy as np
import jax
import jax.numpy as jnp
from jax import lax
from jax.experimental import pallas as pl
from jax.experimental.pallas import tpu as pltpu

F32 = jnp.float32
BF16 = jnp.bfloat16

NORM_EPS = 1e-6
NEG_INF = -1e30

D_MODEL = 2048
PLE_DIM = 256
ATTN_HEADS = 8
ATTN_HEAD_DIM = 128
ATTN_WIDTH = ATTN_HEADS * ATTN_HEAD_DIM
MOBA_BLOCK = 256
MOBA_TOPK = 3
REL_BUCKETS = 32
REL_MAX_DISTANCE = 128
SSM_D_INNER = D_MODEL
SSM_HEAD_DIM = 64
SSM_HEADS = SSM_D_INNER // SSM_HEAD_DIM
SSM_GROUPS = 4
SSM_STATE = 128
SSM_CONV = 4
SSM_CHUNK = 256
SSD_SUB = 128
SSM_CONV_DIM = SSM_D_INNER + 2 * SSM_GROUPS * SSM_STATE
SC_WIDTH = D_MODEL // 2
SC_CONV = 3
D_FF = 4 * D_MODEL

LANES = 128
SUBLANES = 8
VMEM_LIMIT_BYTES = 60 * 1024 * 1024

XBC_OFF = 0
SC_OFF = XBC_OFF + SSM_CONV_DIM
GATE_OFF = SC_OFF + 3 * SC_WIDTH
Z_OFF = GATE_OFF + 3 * D_MODEL
QKV_OFF = Z_OFF + SSM_D_INNER
PROJ_WIDTH = QKV_OFF + 3 * ATTN_WIDTH
PROJ_TN = 1024

_O_Q = 0
_O_Z = 3 * ATTN_WIDTH
_O_XBC = _O_Z + SSM_D_INNER
_O_DT = _O_XBC + SSM_CONV_DIM
_O_SC = _O_DT + SSM_HEADS
_O_GATE = _O_SC + 3 * SC_WIDTH
_O_END = _O_GATE + 3 * D_MODEL

_NT = (((1,), (1,)), ((), ()))


def _params(*sem):
    return pltpu.CompilerParams(dimension_semantics=sem, vmem_limit_bytes=VMEM_LIMIT_BYTES)


def _rms(x, g):
    return x * lax.rsqrt(jnp.mean(x * x, axis=-1, keepdims=True) + NORM_EPS) * g


def _sigmoid(x):
    return 1.0 / (1.0 + jnp.exp(-x))


def _row_chunks(n, parts=4):
    step = n // parts
    return [slice(k * step, (k + 1) * step) for k in range(parts)]


_T = PROJ_TN
_N_XBC, _N_TAIL, _N_Z, _N_QKV = SSM_CONV_DIM // _T, (_O_END - _O_SC) // _T, SSM_D_INNER // _T, 3 * ATTN_WIDTH // _T
_J_TAIL, _J_Z, _J_QKV = _N_XBC, _N_XBC + _N_TAIL, _N_XBC + _N_TAIL + _N_Z
assert (_J_QKV + _N_QKV) * _T == PROJ_WIDTH
assert all(o % SUBLANES == 0 for o in (_O_XBC, _O_SC, _O_Z, _O_Q, _O_DT)) and _T % SUBLANES == 0


def _src_row(j):
    s, ts = SUBLANES, _T // SUBLANES
    groups = jnp.where(j < _J_TAIL, _O_XBC // s + ts * j,
                       jnp.where(j < _J_Z, _O_SC // s + ts * (j - _J_TAIL),
                                 jnp.where(j < _J_QKV, _O_Z // s + ts * (j - _J_Z), _O_Q // s + ts * (j - _J_QKV))))
    return groups * s


def _norm_body(x_ref, g_ref, o_ref):
    o_ref[...] = _rms(x_ref[...], g_ref[...]).astype(o_ref.dtype)


def pre_norm(x, g, layer, tm=512):
    t, d = x.shape
    tm = min(tm, t)
    return pl.pallas_call(
        _norm_body,
        grid=(t // tm,),
        in_specs=[pl.BlockSpec((tm, d), lambda i: (i, 0)),
                  pl.BlockSpec((None, 1, d), lambda i: (layer, 0, 0))],
        out_specs=pl.BlockSpec((tm, d), lambda i: (i, 0)),
        out_shape=jax.ShapeDtypeStruct((t, d), BF16),
        compiler_params=_params("parallel"),
        name="pre_norm",
    )(x, g)


def _inproj_body(h_ref, wt_ref, o_ref):
    o_ref[...] = lax.dot_general(h_ref[...], wt_ref[0].astype(BF16), _NT, preferred_element_type=F32)


def in_projection(h, w_t, layer, tm=2048):
    t, d = h.shape
    tn = _T
    tm = min(tm, t)
    return pl.pallas_call(
        _inproj_body,
        grid=(t // tm, PROJ_WIDTH // tn),
        in_specs=[
            pl.BlockSpec((tm, d), lambda i, j: (i, 0)),
            pl.BlockSpec((pl.Element(1), pl.Element(tn), pl.Element(d)), lambda i, j: (layer, _src_row(j), 0)),
        ],
        out_specs=pl.BlockSpec((tm, tn), lambda i, j: (i, j)),
        out_shape=jax.ShapeDtypeStruct((t, PROJ_WIDTH), F32),
        compiler_params=_params("parallel", "arbitrary"),
        name="in_projection",
    )(h, w_t)


def dt_projection(h, w_t, layer, tm=2048):
    t, d = h.shape
    tm = min(tm, t)
    return pl.pallas_call(
        _inproj_body,
        grid=(t // tm,),
        in_specs=[
            pl.BlockSpec((tm, d), lambda i: (i, 0)),
            pl.BlockSpec((pl.Element(1), pl.Element(LANES), pl.Element(d)), lambda i: (layer, _O_DT, 0)),
        ],
        out_specs=pl.BlockSpec((tm, LANES), lambda i: (i, 0)),
        out_shape=jax.ShapeDtypeStruct((t, LANES), F32),
        compiler_params=_params("parallel"),
        name="dt_projection",
    )(h, w_t)


def _bias_body(tbl_ref, o_ref):
    h = pl.program_id(0)
    blk = MOBA_BLOCK
    key = lax.broadcasted_iota(jnp.int32, (blk, blk), 0)
    qry = lax.broadcasted_iota(jnp.int32, (blk, blk), 1)
    max_exact = REL_BUCKETS // 2
    for t in range(3):
        n = jnp.maximum(t * blk + qry - key, 0)
        nf = jnp.maximum(n, 1).astype(F32)
        large = max_exact + (jnp.log(nf / max_exact) / math.log(REL_MAX_DISTANCE / max_exact)
                             * (REL_BUCKETS - max_exact)).astype(jnp.int32)
        large = jnp.minimum(large, REL_BUCKETS - 1)
        bucket = jnp.where(n < max_exact, n, large)
        acc = jnp.zeros((blk, blk), F32)
        for b in range(REL_BUCKETS):
            acc = jnp.where(bucket == b, tbl_ref[b, h], acc)
        o_ref[t] = acc


def bias_tiles(rel_bias):
    nh = rel_bias.shape[1]
    return pl.pallas_call(
        _bias_body,
        grid=(nh,),
        in_specs=[pl.BlockSpec(memory_space=pltpu.SMEM)],
        out_specs=pl.BlockSpec((None, 3, MOBA_BLOCK, MOBA_BLOCK), lambda h: (h, 0, 0, 0)),
        out_shape=jax.ShapeDtypeStruct((nh, 3, MOBA_BLOCK, MOBA_BLOCK), F32),
        compiler_params=_params("arbitrary"),
        name="bias_tiles",
    )(rel_bias)


def _attn_body(q_ref, k_ref, v_ref, bias_ref, o_ref, kmean_ref, *, nb):
    blk = MOBA_BLOCK
    scale = ATTN_HEAD_DIM ** -0.5

    k = k_ref[...]
    kb = k.astype(BF16)
    v_t = v_ref[...].T.astype(BF16)
    for j in range(nb):
        kmean_ref[j:j + 1, :] = jnp.mean(k[j * blk:(j + 1) * blk, :], axis=0, keepdims=True)
    q_all = q_ref[...]
    gate_all = lax.dot_general(kmean_ref[...], q_all, _NT, precision=lax.Precision.HIGHEST,
                               preferred_element_type=F32)
    qb_all = (q_all * scale).astype(BF16)

    key = lax.broadcasted_iota(jnp.int32, (blk, blk), 0)
    qry = lax.broadcasted_iota(jnp.int32, (blk, blk), 1)
    causal = key <= qry
    bias_own = bias_ref[0]
    bias_prev = bias_ref[1]
    bias_far = bias_ref[2, 0:1, 0:1]
    bidx = lax.broadcasted_iota(jnp.int32, (gate_all.shape[0], blk), 0)

    for qi in range(nb):
        cols = slice(qi * blk, (qi + 1) * blk)
        nk = (qi + 1) * blk
        s = lax.dot_general(kb[0:nk, :], qb_all[cols, :], _NT, preferred_element_type=F32)

        keep = None
        if qi > MOBA_TOPK:
            gate = gate_all[:, cols]
            past = bidx < qi
            keep = []
            for j in range(qi):
                g_j = gate[j:j + 1, :]
                beats = ((gate > g_j) | ((gate == g_j) & (bidx < j))) & past
                keep.append(jnp.sum(beats.astype(F32), axis=0, keepdims=True) < MOBA_TOPK)

        pieces = []
        for j in range(qi + 1):
            s_j = s[j * blk:(j + 1) * blk, :]
            if j == qi:
                s_j = jnp.where(causal, s_j + bias_own, NEG_INF)
            elif j == qi - 1:
                s_j = s_j + bias_prev
                if keep is not None:
                    s_j = jnp.where(keep[j], s_j, NEG_INF)
            elif keep is not None:
                s_j = s_j + jnp.where(keep[j], bias_far, NEG_INF)
            else:
                s_j = s_j + bias_far
            pieces.append(s_j)

        m = pieces[0]
        for s_j in pieces[1:]:
            m = jnp.maximum(m, s_j)
        m = jnp.max(m, axis=0, keepdims=True)
        probs = [jnp.exp(s_j - m) for s_j in pieces]
        total = probs[0]
        for p_j in probs[1:]:
            total = total + p_j
        denom = jnp.sum(total, axis=0, keepdims=True)
        p_t = jnp.concatenate([p_j.astype(BF16) for p_j in probs], axis=0)
        o_t = jnp.dot(v_t[:, 0:nk], p_t, preferred_element_type=F32)
        o_ref[cols, :] = (o_t / denom).T.astype(o_ref.dtype)


def moba_attention(proj, bias, batch, seq, qkv_off):
    t = proj.shape[0]
    blk, dh, nh = MOBA_BLOCK, ATTN_HEAD_DIM, ATTN_HEADS
    nb = seq // blk
    qc = qkv_off // dh
    return pl.pallas_call(
        functools.partial(_attn_body, nb=nb),
        grid=(batch, nh),
        in_specs=[
            pl.BlockSpec((seq, dh), lambda b, h: (b, qc + h)),
            pl.BlockSpec((seq, dh), lambda b, h: (b, qc + nh + h)),
            pl.BlockSpec((seq, dh), lambda b, h: (b, qc + 2 * nh + h)),
            pl.BlockSpec((None, 3, blk, blk), lambda b, h: (h, 0, 0, 0)),
        ],
        out_specs=pl.BlockSpec((seq, dh), lambda b, h: (b, h)),
        out_shape=jax.ShapeDtypeStruct((t, nh * dh), BF16),
        scratch_shapes=[pltpu.VMEM((max(nb, SUBLANES), dh), F32)],
        compiler_params=_params("parallel", "parallel"),
        name="moba_attention",
    )(proj, proj, proj, bias)


def _hi_lo(v):
    hi = v.astype(BF16)
    lo = (v - hi.astype(F32)).astype(BF16)
    return jnp.concatenate([hi, lo], axis=1)


def _causal_taps(ext_ref, cs, w_ref, rows):
    width = w_ref.shape[0]
    cols = w_ref[:, cs].shape[1]
    shape = (rows // SUBLANES, SUBLANES, cols)
    prev = ext_ref[0:rows, cs].reshape(shape)
    cur = ext_ref[SUBLANES:SUBLANES + rows, cs].reshape(shape)
    sub = lax.broadcasted_iota(jnp.int32, shape, 1)
    acc = w_ref[width - 1:width, cs].reshape(1, 1, cols) * cur
    for j in range(1, width):
        shifted = pltpu.roll(jnp.where(sub >= SUBLANES - j, prev, cur), j, 1)
        acc = acc + w_ref[width - 1 - j:width - j, cs].reshape(1, 1, cols) * shifted
    return acc.reshape(rows, cols)


def _mixer_body(xbc_ref, sc_ref, z_ref, dt_ref, cw_ref, cb_ref, dtb_ref, alog_ref, dskip_ref, ng_ref,
                scw_ref, e_ref, yssm_ref, ysc_ref, ext_ref, ext2_ref, xc_ref, state_ref):
    c = pl.program_id(1)
    L = SSM_CHUNK
    sub = SSD_SUB
    halo = SUBLANES
    gw = SSM_D_INNER // SSM_GROUPS
    hpg = SSM_HEADS // SSM_GROUPS
    n_state = SSM_STATE

    @pl.when(c == 0)
    def _():
        ext_ref[0:halo, :] = jnp.zeros((halo, ext_ref.shape[1]), F32)
        ext2_ref[0:halo, :] = jnp.zeros((halo, ext2_ref.shape[1]), F32)
        state_ref[...] = jnp.zeros(state_ref.shape, F32)

    @pl.when(c > 0)
    def _():
        ext_ref[0:halo, :] = ext_ref[L:L + halo, :]
        ext2_ref[0:halo, :] = ext2_ref[L:L + halo, :]

    ext_ref[halo:halo + L, :] = xbc_ref[...]
    cchunk = 512
    for ci in range(SSM_CONV_DIM // cchunk):
        cs = slice(ci * cchunk, (ci + 1) * cchunk)
        v = _causal_taps(ext_ref, cs, cw_ref, L) + cb_ref[:, cs]
        xc_ref[:, cs] = v * _sigmoid(v)

    ext2_ref[halo:halo + L, :] = sc_ref[:, SC_WIDTH:2 * SC_WIDTH] * sc_ref[:, 2 * SC_WIDTH:3 * SC_WIDTH]
    for ci in range(SC_WIDTH // cchunk):
        cs = slice(ci * cchunk, (ci + 1) * cchunk)
        ysc_ref[:, cs] = (sc_ref[:, cs] * _causal_taps(ext2_ref, cs, scw_ref, L)).astype(ysc_ref.dtype)

    rows = lax.broadcasted_iota(jnp.int32, (L, LANES), 0)
    head_lane = lax.broadcasted_iota(jnp.int32, (L, LANES), 1) < SSM_HEADS
    x_dt = jnp.where(head_lane, dt_ref[...], 0.0) + dtb_ref[...]
    dt = jnp.maximum(x_dt, 0.0) + jnp.log1p(jnp.exp(-jnp.abs(x_dt)))
    a = dt * (-jnp.exp(alog_ref[...]))
    pos = rows % sub
    a_cum = a
    sh = 1
    while sh < sub:
        a_cum = a_cum + jnp.where(pos >= sh, pltpu.roll(a_cum, sh, axis=0), 0.0)
        sh *= 2
    a_cum_t = a_cum.T
    ea = jnp.exp(a_cum)
    a_end = a_cum[L - 1:L, :]
    for s in range(L // sub - 2, -1, -1):
        a_end = jnp.where(rows < (s + 1) * sub, a_cum[(s + 1) * sub - 1:(s + 1) * sub, :], a_end)
    ds = jnp.exp(a_end - a_cum)
    dt_hl, ea_hl, ds_hl = _hi_lo(dt), _hi_lo(ea), _hi_lo(ds)

    row = lax.broadcasted_iota(jnp.int32, (sub, sub), 0)
    col = lax.broadcasted_iota(jnp.int32, (sub, sub), 1)
    causal = row >= col
    low_half = lax.broadcasted_iota(jnp.int32, (sub, LANES), 1) < SSM_HEAD_DIM

    for g in range(SSM_GROUPS):
        cs = slice(g * gw, (g + 1) * gw)
        b_cols = slice(SSM_D_INNER + g * n_state, SSM_D_INNER + (g + 1) * n_state)
        c_cols = slice(SSM_D_INNER + (SSM_GROUPS + g) * n_state, SSM_D_INNER + (SSM_GROUPS + g + 1) * n_state)
        e_g = e_ref[:, cs]
        xs_g = xc_ref[:, cs]
        ea_e = jnp.dot(ea_hl, e_g, preferred_element_type=F32)
        xdt = xs_g * jnp.dot(dt_hl, e_g, preferred_element_type=F32)
        xdt_b = xdt.astype(BF16)
        xds_b = (xdt * jnp.dot(ds_hl, e_g, preferred_element_type=F32)).astype(BF16)

        st = state_ref[g]
        y_rows = []
        for s in range(L // sub):
            rs = slice(s * sub, (s + 1) * sub)
            b_s = xc_ref[rs, b_cols]
            c_sb = xc_ref[rs, c_cols].astype(BF16)
            cb = lax.dot_general(c_sb, b_s.astype(BF16), _NT, preferred_element_type=F32)
            y = jnp.dot(c_sb, st.astype(BF16), preferred_element_type=F32) * ea_e[rs, :]

            pieces = []
            for pe in range(hpg // 2):
                x_pair = xdt_b[rs, pe * LANES:(pe + 1) * LANES]
                acc = None
                for half in range(2):
                    h = g * hpg + 2 * pe + half
                    seg = a_cum[rs, h:h + 1] - a_cum_t[h:h + 1, rs]
                    decay = jnp.exp(jnp.where(causal, seg, -jnp.inf))
                    m_h = (cb * decay).astype(BF16)
                    keep = low_half if half == 0 else jnp.logical_not(low_half)
                    x_h = jnp.where(keep, x_pair, jnp.zeros_like(x_pair))
                    term = jnp.dot(m_h, x_h, preferred_element_type=F32)
                    acc = term if acc is None else acc + term
                pieces.append(acc)
            y_rows.append(y + jnp.concatenate(pieces, axis=1))

            upd = jnp.dot(b_s.T.astype(BF16), xds_b[rs, :], preferred_element_type=F32)
            st = st * ea_e[(s + 1) * sub - 1:(s + 1) * sub, :] + upd
        state_ref[g] = st

        y = jnp.concatenate(y_rows, axis=0) + dskip_ref[:, cs] * xs_g
        z_g = z_ref[:, cs]
        y = y * (z_g * _sigmoid(z_g))
        y = y * lax.rsqrt(jnp.mean(y * y, axis=-1, keepdims=True) + NORM_EPS)
        yssm_ref[:, cs] = (y * ng_ref[:, cs]).astype(yssm_ref.dtype)


def _head_expand_matrix():
    e = np.zeros((2 * LANES, SSM_D_INNER), np.float32)
    for h in range(SSM_HEADS):
        e[h, h * SSM_HEAD_DIM:(h + 1) * SSM_HEAD_DIM] = 1.0
        e[LANES + h, h * SSM_HEAD_DIM:(h + 1) * SSM_HEAD_DIM] = 1.0
    return jnp.asarray(e, BF16)


def ssd_and_short_conv(proj, dt_raw, conv_w, conv_b, dt_bias, a_log, d_skip, norm_g, sc_w, layer, batch, seq):
    t = proj.shape[0]
    L = SSM_CHUNK
    nc = seq // L
    row_map = lambda b, c: (b * nc + c, 0)

    def col_block(off, width):
        return pl.BlockSpec((L, width), lambda b, c: (b * nc + c, off // width))

    def param(rows, cols):
        return pl.BlockSpec((None, rows, cols), lambda b, c: (layer, 0, 0))

    return pl.pallas_call(
        _mixer_body,
        grid=(batch, nc),
        in_specs=[
            col_block(XBC_OFF, SSM_CONV_DIM),
            col_block(SC_OFF, 3 * SC_WIDTH),
            col_block(Z_OFF, SSM_D_INNER),
            pl.BlockSpec((L, LANES), row_map),
            param(SSM_CONV, SSM_CONV_DIM),
            param(1, SSM_CONV_DIM),
            param(1, LANES),
            param(1, LANES),
            param(1, SSM_D_INNER),
            param(1, SSM_D_INNER),
            param(SC_CONV, SC_WIDTH),
            pl.BlockSpec((2 * LANES, SSM_D_INNER), lambda b, c: (0, 0)),
        ],
        out_specs=[
            pl.BlockSpec((L, SSM_D_INNER), row_map),
            pl.BlockSpec((L, SC_WIDTH), row_map),
        ],
        out_shape=[
            jax.ShapeDtypeStruct((t, SSM_D_INNER), BF16),
            jax.ShapeDtypeStruct((t, SC_WIDTH), BF16),
        ],
        scratch_shapes=[
            pltpu.VMEM((L + 2 * SUBLANES, SSM_CONV_DIM), F32),
            pltpu.VMEM((L + 2 * SUBLANES, SC_WIDTH), F32),
            pltpu.VMEM((L, SSM_CONV_DIM), F32),
            pltpu.VMEM((SSM_GROUPS, SSM_STATE, SSM_D_INNER // SSM_GROUPS), F32),
        ],
        compiler_params=_params("parallel", "arbitrary"),
        name="ssd_short_conv",
    )(proj, proj, proj, dt_raw, conv_w, conv_b, dt_bias, a_log, d_skip, norm_g, sc_w, _head_expand_matrix())


def _merge_body(ya_ref, ym_ref, yc_ref, ga_ref, gm_ref, gc_ref, wa_ref, wm_ref, wc_ref, o_ref):
    a = jnp.dot(ya_ref[...], wa_ref[...].astype(BF16), preferred_element_type=F32)
    m = jnp.dot(ym_ref[...], wm_ref[...].astype(BF16), preferred_element_type=F32)
    c = jnp.dot(yc_ref[...], wc_ref[...].astype(BF16), preferred_element_type=F32)
    o = _sigmoid(ga_ref[...]) * a + _sigmoid(gm_ref[...]) * m + _sigmoid(gc_ref[...]) * c
    o_ref[...] = o.astype(o_ref.dtype)


def branch_merge(y_attn, y_ssm, y_sc, proj, w_a, w_m, w_c, layer, tm=2048, tn=256):
    t = y_attn.shape[0]
    d = w_a.shape[-1]
    tm = min(tm, t)
    gc = GATE_OFF // tn
    gs = d // tn

    def act(width):
        return pl.BlockSpec((tm, width), lambda i, j: (i, 0), pipeline_mode=pl.Buffered(1))

    def gate(k):
        return pl.BlockSpec((tm, tn), lambda i, j: (i, gc + k * gs + j))

    def weight(rows):
        return pl.BlockSpec((None, rows, tn), lambda i, j: (layer, 0, j))

    return pl.pallas_call(
        _merge_body,
        grid=(t // tm, d // tn),
        in_specs=[act(y_attn.shape[1]), act(y_ssm.shape[1]), act(y_sc.shape[1]),
                  gate(0), gate(1), gate(2),
                  weight(w_a.shape[1]), weight(w_m.shape[1]), weight(w_c.shape[1])],
        out_specs=pl.BlockSpec((tm, tn), lambda i, j: (i, j)),
        out_shape=jax.ShapeDtypeStruct((t, d), BF16),
        compiler_params=_params("parallel", "arbitrary"),
        name="branch_merge",
    )(y_attn, y_ssm, y_sc, proj, proj, proj, w_a, w_m, w_c)


def _outproj_body(m_ref, x_ref, w_ref, gpost_ref, gnext_ref, o_ref, h_ref, wb_ref):
    @pl.when(pl.program_id(0) == 0)
    def _():
        wb_ref[...] = w_ref[...].astype(BF16)

    for rows in _row_chunks(m_ref.shape[0]):
        o = jnp.dot(m_ref[rows, :], wb_ref[...], preferred_element_type=F32)
        x = x_ref[rows, :] + _rms(o, gpost_ref[...])
        o_ref[rows, :] = x
        h_ref[rows, :] = _rms(x, gnext_ref[...]).astype(h_ref.dtype)


def out_projection(merged, x, w, g_post, g_next, layer, tm=512):
    t, d = x.shape
    tm = min(tm, t)
    rows = pl.BlockSpec((tm, d), lambda i: (i, 0))
    gain = pl.BlockSpec((None, 1, d), lambda i: (layer, 0, 0))
    return pl.pallas_call(
        _outproj_body,
        grid=(t // tm,),
        in_specs=[rows, rows, pl.BlockSpec((None, d, d), lambda i: (layer, 0, 0), pipeline_mode=pl.Buffered(1)),
                  gain, gain],
        out_specs=[rows, rows],
        out_shape=[jax.ShapeDtypeStruct((t, d), F32), jax.ShapeDtypeStruct((t, d), BF16)],
        scratch_shapes=[pltpu.VMEM((d, d), BF16)],
        compiler_params=_params("arbitrary"),
        name="out_projection",
    )(merged, x, w, g_post, g_next)


def _mlp_body(h_ref, x_ref, w1_ref, w2_ref, gpost_ref, o_ref):
    f = pl.program_id(1)

    @pl.when(f == 0)
    def _():
        o_ref[...] = jnp.zeros(o_ref.shape, F32)

    u = jnp.maximum(jnp.dot(h_ref[...], w1_ref[...].astype(BF16), preferred_element_type=F32), 0.0)
    o_ref[...] += jnp.dot((u * u).astype(BF16), w2_ref[...].astype(BF16), preferred_element_type=F32)

    @pl.when(f == pl.num_programs(1) - 1)
    def _():
        o_ref[...] = x_ref[...] + _rms(o_ref[...], gpost_ref[...])


def mlp(h, x, w1, w2, g_post, layer, tm=1024, tf=512):
    t, d = x.shape
    ff = w1.shape[-1]
    tm = min(tm, t)
    return pl.pallas_call(
        _mlp_body,
        grid=(t // tm, ff // tf),
        in_specs=[
            pl.BlockSpec((tm, d), lambda i, f: (i, 0)),
            pl.BlockSpec((tm, d), lambda i, f: (i, 0), pipeline_mode=pl.Buffered(1)),
            pl.BlockSpec((None, d, tf), lambda i, f: (layer, 0, f)),
            pl.BlockSpec((None, tf, d), lambda i, f: (layer, f, 0)),
            pl.BlockSpec((None, 1, d), lambda i, f: (layer, 0, 0)),
        ],
        out_specs=pl.BlockSpec((tm, d), lambda i, f: (i, 0)),
        out_shape=jax.ShapeDtypeStruct((t, d), F32),
        compiler_params=_params("parallel", "arbitrary"),
        name="mlp",
    )(h, x, w1, w2, g_post)


def _ple_body(x_ref, p_ref, gpre_ref, wg_ref, wp_ref, gpost_ref, *rest):
    wgb_ref, wpb_ref = rest[-2:]

    @pl.when(pl.program_id(0) == 0)
    def _():
        wgb_ref[...] = wg_ref[...].astype(BF16)
        wpb_ref[...] = wp_ref[...].astype(BF16)

    for rows in _row_chunks(x_ref.shape[0]):
        x = x_ref[rows, :]
        h = _rms(x, gpre_ref[...]).astype(BF16)
        gate = _sigmoid(jnp.dot(h, wgb_ref[...], preferred_element_type=F32))
        emb = jnp.dot(p_ref[rows, :].astype(BF16), wpb_ref[...], preferred_element_type=F32)
        x = x + _rms(gate * emb, gpost_ref[...])
        if len(rest) == 3:
            o_ref = rest[0]
        else:
            gnext_ref, o_ref, h_ref = rest[:3]
            h_ref[rows, :] = _rms(x, gnext_ref[...]).astype(h_ref.dtype)
        o_ref[rows, :] = x


def per_layer_embedding(x, p, g_pre, w_gate, w_proj, g_post, layer, g_next=None, tm=512):
    t, d = x.shape
    pd = p.shape[-1]
    tm = min(tm, t)
    rows = pl.BlockSpec((tm, d), lambda i: (i, 0))
    gain = pl.BlockSpec((None, 1, d), lambda i: (layer, 0, 0))
    in_specs = [
        rows,
        pl.BlockSpec((None, tm, pd), lambda i: (layer, i, 0)),
        gain,
        pl.BlockSpec((None, d, d), lambda i: (layer, 0, 0), pipeline_mode=pl.Buffered(1)),
        pl.BlockSpec((None, pd, d), lambda i: (layer, 0, 0), pipeline_mode=pl.Buffered(1)),
        gain,
    ]
    args = [x, p, g_pre, w_gate, w_proj, g_post]
    out_specs = [rows]
    out_shape = [jax.ShapeDtypeStruct((t, d), F32)]
    if g_next is not None:
        in_specs.append(pl.BlockSpec((None, 1, d), lambda i: (layer + 1, 0, 0)))
        args.append(g_next)
        out_specs.append(rows)
        out_shape.append(jax.ShapeDtypeStruct((t, d), BF16))
    return pl.pallas_call(
        _ple_body,
        grid=(t // tm,),
        in_specs=in_specs,
        out_specs=out_specs,
        out_shape=out_shape,
        scratch_shapes=[pltpu.VMEM((d, d), BF16), pltpu.VMEM((pd, d), BF16)],
        compiler_params=_params("arbitrary"),
        name="per_layer_embedding",
    )(*args)


def _row(v):
    return v[:, None, :]


def _lane_pad(v):
    return jnp.pad(v, ((0, 0), (0, LANES - v.shape[-1])))[:, None, :]


def kernel(x, p, rel_bias, g_mix_pre, w_in, ssm_conv_w, ssm_conv_b, ssm_dt_bias, ssm_a_log, ssm_d, ssm_norm_g, sc_conv_w, w_br_attn, w_br_ssm, w_br_conv, w_out, g_mix_post, g_mlp_pre, w_mlp_up, w_mlp_down, g_mlp_post, g_ple_pre, w_ple_gate, w_ple_proj, g_ple_post):
    batch, seq, d = x.shape
    depth = w_in.shape[0]
    t = batch * seq

    w_in_t = jnp.swapaxes(w_in, 1, 2)
    d_skip = jnp.repeat(ssm_d, SSM_HEAD_DIM, axis=-1)[:, None, :]
    dt_bias, a_log = _lane_pad(ssm_dt_bias), _lane_pad(ssm_a_log)
    conv_b, norm_g = _row(ssm_conv_b), _row(ssm_norm_g)
    g_mix_pre, g_mix_post = _row(g_mix_pre), _row(g_mix_post)
    g_mlp_pre, g_mlp_post = _row(g_mlp_pre), _row(g_mlp_post)
    g_ple_pre, g_ple_post = _row(g_ple_pre), _row(g_ple_post)

    bias = bias_tiles(rel_bias)
    xf = x.reshape(t, d)
    pf = p.reshape(depth, t, p.shape[-1])
    h = pre_norm(xf, g_mix_pre, 0)
    for i in range(depth):
        proj = in_projection(h, w_in_t, i)
        y_attn = moba_attention(proj, bias, batch, seq, QKV_OFF)
        dt_raw = dt_projection(h, w_in_t, i)
        y_ssm, y_sc = ssd_and_short_conv(proj, dt_raw, ssm_conv_w, conv_b, dt_bias, a_log, d_skip, norm_g,
                                         sc_conv_w, i, batch, seq)
        merged = branch_merge(y_attn, y_ssm, y_sc, proj, w_br_attn, w_br_ssm, w_br_conv, i)
        xf, h_mlp = out_projection(merged, xf, w_out, g_mix_post, g_mlp_pre, i)
        xf = mlp(h_mlp, xf, w_mlp_up, w_mlp_down, g_mlp_post, i)
        if i + 1 < depth:
            xf, h = per_layer_embedding(xf, pf, g_ple_pre, w_ple_gate, w_ple_proj, g_ple_post, i, g_next=g_mix_pre)
        else:
            (xf,) = per_layer_embedding(xf, pf, g_ple_pre, w_ple_gate, w_ple_proj, g_ple_post, i)
    return xf.reshape(batch, seq, d)
```

```python
import functools
import math

import numpy as np
import jax
import jax.numpy as jnp
from jax import lax
from jax.experimental import pallas as pl
from jax.experimental.pallas import tpu as pltpu

F32 = jnp.float32
BF16 = jnp.bfloat16

NORM_EPS = 1e-6
NEG_INF = -1e30
LOG2E = math.log2(math.e)

D_MODEL = 2048
PLE_DIM = 256
ATTN_HEADS = 8
ATTN_HEAD_DIM = 128
ATTN_WIDTH = ATTN_HEADS * ATTN_HEAD_DIM
MOBA_BLOCK = 256
MOBA_TOPK = 3
REL_BUCKETS = 32
REL_MAX_DISTANCE = 128
SSM_D_INNER = D_MODEL
SSM_HEAD_DIM = 64
SSM_HEADS = SSM_D_INNER // SSM_HEAD_DIM
SSM_GROUPS = 4
SSM_STATE = 128
SSM_CONV = 4
SSM_CHUNK = 256
SSD_SUB = 128
SSM_CONV_DIM = SSM_D_INNER + 2 * SSM_GROUPS * SSM_STATE
SC_WIDTH = D_MODEL // 2
SC_CONV = 3
D_FF = 4 * D_MODEL

LANES = 128
SUBLANES = 8
VMEM_LIMIT_BYTES = 60 * 1024 * 1024

XBC_OFF = 0
SC_OFF = XBC_OFF + SSM_CONV_DIM
GATE_OFF = SC_OFF + 3 * SC_WIDTH
Z_OFF = GATE_OFF + 3 * D_MODEL
QKV_OFF = Z_OFF + SSM_D_INNER
PROJ_WIDTH = QKV_OFF + 3 * ATTN_WIDTH
PROJ_TN = 1024

_O_Q = 0
_O_Z = 3 * ATTN_WIDTH
_O_XBC = _O_Z + SSM_D_INNER
_O_DT = _O_XBC + SSM_CONV_DIM
_O_SC = _O_DT + SSM_HEADS
_O_GATE = _O_SC + 3 * SC_WIDTH
_O_END = _O_GATE + 3 * D_MODEL

_NT = (((1,), (1,)), ((), ()))


def _params(*sem):
    return pltpu.CompilerParams(dimension_semantics=sem, vmem_limit_bytes=VMEM_LIMIT_BYTES)


def _rms(x, g):
    return x * lax.rsqrt(jnp.mean(x * x, axis=-1, keepdims=True) + NORM_EPS) * g


def _sigmoid(x):
    return 1.0 / (1.0 + jnp.exp(-x))


def _row_chunks(n, parts=4):
    step = n // parts
    return [slice(k * step, (k + 1) * step) for k in range(parts)]


_T = PROJ_TN
_N_XBC, _N_TAIL, _N_Z, _N_QKV = SSM_CONV_DIM // _T, (_O_END - _O_SC) // _T, SSM_D_INNER // _T, 3 * ATTN_WIDTH // _T
_J_TAIL, _J_Z, _J_QKV = _N_XBC, _N_XBC + _N_TAIL, _N_XBC + _N_TAIL + _N_Z
assert (_J_QKV + _N_QKV) * _T == PROJ_WIDTH
assert all(o % SUBLANES == 0 for o in (_O_XBC, _O_SC, _O_Z, _O_Q, _O_DT)) and _T % SUBLANES == 0


def _src_row(j):
    s, ts = SUBLANES, _T // SUBLANES
    groups = jnp.where(j < _J_TAIL, _O_XBC // s + ts * j,
                       jnp.where(j < _J_Z, _O_SC // s + ts * (j - _J_TAIL),
                                 jnp.where(j < _J_QKV, _O_Z // s + ts * (j - _J_Z), _O_Q // s + ts * (j - _J_QKV))))
    return groups * s


def _norm_body(x_ref, g_ref, o_ref):
    o_ref[...] = _rms(x_ref[...], g_ref[...]).astype(o_ref.dtype)


def pre_norm(x, g, layer, tm=512):
    t, d = x.shape
    tm = min(tm, t)
    return pl.pallas_call(
        _norm_body,
        grid=(t // tm,),
        in_specs=[pl.BlockSpec((tm, d), lambda i: (i, 0)),
                  pl.BlockSpec((None, 1, d), lambda i: (layer, 0, 0))],
        out_specs=pl.BlockSpec((tm, d), lambda i: (i, 0)),
        out_shape=jax.ShapeDtypeStruct((t, d), BF16),
        compiler_params=_params("parallel"),
        name="pre_norm",
    )(x, g)


def _inproj_body(h_ref, wt_ref, o_ref):
    o_ref[...] = lax.dot_general(h_ref[...], wt_ref[0].astype(BF16), _NT, preferred_element_type=F32)


def in_projection(h, w_t, layer, tm=2048):
    t, d = h.shape
    tn = _T
    tm = min(tm, t)
    return pl.pallas_call(
        _inproj_body,
        grid=(t // tm, PROJ_WIDTH // tn),
        in_specs=[
            pl.BlockSpec((tm, d), lambda i, j: (i, 0)),
            pl.BlockSpec((pl.Element(1), pl.Element(tn), pl.Element(d)), lambda i, j: (layer, _src_row(j), 0)),
        ],
        out_specs=pl.BlockSpec((tm, tn), lambda i, j: (i, j)),
        out_shape=jax.ShapeDtypeStruct((t, PROJ_WIDTH), F32),
        compiler_params=_params("parallel", "arbitrary"),
        name="in_projection",
    )(h, w_t)


def dt_projection(h, w_t, layer, tm=2048):
    t, d = h.shape
    tm = min(tm, t)
    return pl.pallas_call(
        _inproj_body,
        grid=(t // tm,),
        in_specs=[
            pl.BlockSpec((tm, d), lambda i: (i, 0)),
            pl.BlockSpec((pl.Element(1), pl.Element(LANES), pl.Element(d)), lambda i: (layer, _O_DT, 0)),
        ],
        out_specs=pl.BlockSpec((tm, LANES), lambda i: (i, 0)),
        out_shape=jax.ShapeDtypeStruct((t, LANES), F32),
        compiler_params=_params("parallel"),
        name="dt_projection",
    )(h, w_t)


def _bias_body(tbl_ref, o_ref):
    h = pl.program_id(0)
    blk = MOBA_BLOCK
    key = lax.broadcasted_iota(jnp.int32, (blk, blk), 0)
    qry = lax.broadcasted_iota(jnp.int32, (blk, blk), 1)
    max_exact = REL_BUCKETS // 2
    for t in range(3):
        n = jnp.maximum(t * blk + qry - key, 0)
        nf = jnp.maximum(n, 1).astype(F32)
        large = max_exact + (jnp.log(nf / max_exact) / math.log(REL_MAX_DISTANCE / max_exact)
                             * (REL_BUCKETS - max_exact)).astype(jnp.int32)
        large = jnp.minimum(large, REL_BUCKETS - 1)
        bucket = jnp.where(n < max_exact, n, large)
        acc = jnp.zeros((blk, blk), F32)
        for b in range(REL_BUCKETS):
            acc = jnp.where(bucket == b, tbl_ref[b, h], acc)
        o_ref[t] = acc


def bias_tiles(rel_bias):
    nh = rel_bias.shape[1]
    return pl.pallas_call(
        _bias_body,
        grid=(nh,),
        in_specs=[pl.BlockSpec(memory_space=pltpu.SMEM)],
        out_specs=pl.BlockSpec((None, 3, MOBA_BLOCK, MOBA_BLOCK), lambda h: (h, 0, 0, 0)),
        out_shape=jax.ShapeDtypeStruct((nh, 3, MOBA_BLOCK, MOBA_BLOCK), F32),
        compiler_params=_params("arbitrary"),
        name="bias_tiles",
    )(rel_bias)


def _attn_body(q_ref, k_ref, v_ref, bias_ref, o_ref, kmean_ref, *, nb):
    blk = MOBA_BLOCK
    scale = ATTN_HEAD_DIM ** -0.5

    k = k_ref[...]
    kb = k.astype(BF16)
    v_t = v_ref[...].T.astype(BF16)
    for j in range(nb):
        kmean_ref[j:j + 1, :] = jnp.mean(k[j * blk:(j + 1) * blk, :], axis=0, keepdims=True)
    q_all = q_ref[...]
    gate_all = lax.dot_general(kmean_ref[...], q_all, _NT, precision=lax.Precision.HIGHEST,
                               preferred_element_type=F32)
    qb_all = (q_all * (scale * LOG2E)).astype(BF16)

    key = lax.broadcasted_iota(jnp.int32, (blk, blk), 0)
    qry = lax.broadcasted_iota(jnp.int32, (blk, blk), 1)
    causal = key <= qry
    bias_own = bias_ref[0] * LOG2E
    bias_prev = bias_ref[1] * LOG2E
    bias_far = bias_ref[2, 0:1, 0:1] * LOG2E
    bidx = lax.broadcasted_iota(jnp.int32, (gate_all.shape[0], blk), 0)

    def scores(qi):
        return lax.dot_general(kb[0:(qi + 1) * blk, :], qb_all[qi * blk:(qi + 1) * blk, :], _NT,
                               preferred_element_type=F32)

    def finish(cols, nk, p_t, denom):
        o_t = jnp.dot(v_t[:, 0:nk], p_t, preferred_element_type=F32)
        o_ref[cols, :] = (o_t / denom).T.astype(o_ref.dtype)

    pending = None
    s_next = scores(0)
    for qi in range(nb):
        cols = slice(qi * blk, (qi + 1) * blk)
        nk = (qi + 1) * blk
        s = s_next
        if qi + 1 < nb:
            s_next = scores(qi + 1)

        keep = None
        if qi > MOBA_TOPK:
            gate = gate_all[:, cols]
            past = bidx < qi
            keep = []
            for j in range(qi):
                g_j = gate[j:j + 1, :]
                beats = ((gate > g_j) | ((gate == g_j) & (bidx < j))) & past
                keep.append(jnp.sum(beats.astype(F32), axis=0, keepdims=True) < MOBA_TOPK)

        pieces = []
        for j in range(qi + 1):
            s_j = s[j * blk:(j + 1) * blk, :]
            if j == qi:
                s_j = jnp.where(causal, s_j + bias_own, NEG_INF)
            elif j == qi - 1:
                s_j = s_j + bias_prev
                if keep is not None:
                    s_j = jnp.where(keep[j], s_j, NEG_INF)
            elif keep is not None:
                s_j = s_j + jnp.where(keep[j], bias_far, NEG_INF)
            else:
                s_j = s_j + bias_far
            pieces.append(s_j)

        m = pieces[0]
        for s_j in pieces[1:]:
            m = jnp.maximum(m, s_j)
        m = jnp.max(m, axis=0, keepdims=True)
        probs = [jnp.exp2(s_j - m) for s_j in pieces]
        total = probs[0]
        for p_j in probs[1:]:
            total = total + p_j
        denom = jnp.sum(total, axis=0, keepdims=True)
        p_t = jnp.concatenate([p_j.astype(BF16) for p_j in probs], axis=0)
        if pending is not None:
            finish(*pending)
        pending = (cols, nk, p_t, denom)
    finish(*pending)


def moba_attention(proj, bias, batch, seq, qkv_off):
    t = proj.shape[0]
    blk, dh, nh = MOBA_BLOCK, ATTN_HEAD_DIM, ATTN_HEADS
    nb = seq // blk
    qc = qkv_off // dh
    return pl.pallas_call(
        functools.partial(_attn_body, nb=nb),
        grid=(batch, nh),
        in_specs=[
            pl.BlockSpec((seq, dh), lambda b, h: (b, qc + h)),
            pl.BlockSpec((seq, dh), lambda b, h: (b, qc + nh + h)),
            pl.BlockSpec((seq, dh), lambda b, h: (b, qc + 2 * nh + h)),
            pl.BlockSpec((None, 3, blk, blk), lambda b, h: (h, 0, 0, 0)),
        ],
        out_specs=pl.BlockSpec((seq, dh), lambda b, h: (b, h)),
        out_shape=jax.ShapeDtypeStruct((t, nh * dh), BF16),
        scratch_shapes=[pltpu.VMEM((max(nb, SUBLANES), dh), F32)],
        compiler_params=_params("parallel", "parallel"),
        name="moba_attention",
    )(proj, proj, proj, bias)


def _hi_lo(v):
    hi = v.astype(BF16)
    lo = (v - hi.astype(F32)).astype(BF16)
    return jnp.concatenate([hi, lo], axis=1)


def _causal_taps(ext_ref, cs, w_ref, rows):
    width = w_ref.shape[0]
    cols = w_ref[:, cs].shape[1]
    shape = (rows // SUBLANES, SUBLANES, cols)
    prev = ext_ref[0:rows, cs].reshape(shape)
    cur = ext_ref[SUBLANES:SUBLANES + rows, cs].reshape(shape)
    sub = lax.broadcasted_iota(jnp.int32, shape, 1)
    acc = w_ref[width - 1:width, cs].reshape(1, 1, cols) * cur
    for j in range(1, width):
        shifted = pltpu.roll(jnp.where(sub >= SUBLANES - j, prev, cur), j, 1)
        acc = acc + w_ref[width - 1 - j:width - j, cs].reshape(1, 1, cols) * shifted
    return acc.reshape(rows, cols)


def _mixer_body(xbc_ref, sc_ref, z_ref, dt_ref, cw_ref, cb_ref, dtb_ref, alog_ref, dskip_ref, ng_ref,
                scw_ref, e_ref, yssm_ref, ysc_ref, ext_ref, ext2_ref, xc_ref, state_ref):
    c = pl.program_id(1)
    L = SSM_CHUNK
    sub = SSD_SUB
    halo = SUBLANES
    gw = SSM_D_INNER // SSM_GROUPS
    hpg = SSM_HEADS // SSM_GROUPS
    n_state = SSM_STATE

    @pl.when(c == 0)
    def _():
        ext_ref[0:halo, :] = jnp.zeros((halo, ext_ref.shape[1]), F32)
        ext2_ref[0:halo, :] = jnp.zeros((halo, ext2_ref.shape[1]), F32)
        state_ref[...] = jnp.zeros(state_ref.shape, F32)

    @pl.when(c > 0)
    def _():
        ext_ref[0:halo, :] = ext_ref[L:L + halo, :]
        ext2_ref[0:halo, :] = ext2_ref[L:L + halo, :]

    ext_ref[halo:halo + L, :] = xbc_ref[...]
    cchunk = 512
    for ci in range(SSM_CONV_DIM // cchunk):
        cs = slice(ci * cchunk, (ci + 1) * cchunk)
        v = _causal_taps(ext_ref, cs, cw_ref, L) + cb_ref[:, cs]
        xc_ref[:, cs] = v * _sigmoid(v)

    ext2_ref[halo:halo + L, :] = sc_ref[:, SC_WIDTH:2 * SC_WIDTH] * sc_ref[:, 2 * SC_WIDTH:3 * SC_WIDTH]
    for ci in range(SC_WIDTH // cchunk):
        cs = slice(ci * cchunk, (ci + 1) * cchunk)
        ysc_ref[:, cs] = (sc_ref[:, cs] * _causal_taps(ext2_ref, cs, scw_ref, L)).astype(ysc_ref.dtype)

    rows = lax.broadcasted_iota(jnp.int32, (L, LANES), 0)
    head_lane = lax.broadcasted_iota(jnp.int32, (L, LANES), 1) < SSM_HEADS
    x_dt = jnp.where(head_lane, dt_ref[...], 0.0) + dtb_ref[...]
    dt = jnp.maximum(x_dt, 0.0) + jnp.log1p(jnp.exp(-jnp.abs(x_dt)))
    a = dt * (-jnp.exp(alog_ref[...]))
    pos = rows % sub
    a_cum = a
    sh = 1
    while sh < sub:
        a_cum = a_cum + jnp.where(pos >= sh, pltpu.roll(a_cum, sh, axis=0), 0.0)
        sh *= 2
    a_cum_t = a_cum.T
    ea = jnp.exp(a_cum)
    a_end = a_cum[L - 1:L, :]
    for s in range(L // sub - 2, -1, -1):
        a_end = jnp.where(rows < (s + 1) * sub, a_cum[(s + 1) * sub - 1:(s + 1) * sub, :], a_end)
    ds = jnp.exp(a_end - a_cum)
    dt_hl, ea_hl, ds_hl = _hi_lo(dt), _hi_lo(ea), _hi_lo(ds)

    row = lax.broadcasted_iota(jnp.int32, (sub, sub), 0)
    col = lax.broadcasted_iota(jnp.int32, (sub, sub), 1)
    causal = row >= col
    low_half = lax.broadcasted_iota(jnp.int32, (sub, LANES), 1) < SSM_HEAD_DIM

    for g in range(SSM_GROUPS):
        cs = slice(g * gw, (g + 1) * gw)
        b_cols = slice(SSM_D_INNER + g * n_state, SSM_D_INNER + (g + 1) * n_state)
        c_cols = slice(SSM_D_INNER + (SSM_GROUPS + g) * n_state, SSM_D_INNER + (SSM_GROUPS + g + 1) * n_state)
        e_g = e_ref[:, cs]
        xs_g = xc_ref[:, cs]
        ea_e = jnp.dot(ea_hl, e_g, preferred_element_type=F32)
        xdt = xs_g * jnp.dot(dt_hl, e_g, preferred_element_type=F32)
        xdt_b = xdt.astype(BF16)
        xds_b = (xdt * jnp.dot(ds_hl, e_g, preferred_element_type=F32)).astype(BF16)

        st = state_ref[g]
        y_rows = []
        for s in range(L // sub):
            rs = slice(s * sub, (s + 1) * sub)
            b_s = xc_ref[rs, b_cols]
            c_sb = xc_ref[rs, c_cols].astype(BF16)
            cb = lax.dot_general(c_sb, b_s.astype(BF16), _NT, preferred_element_type=F32)
            y = jnp.dot(c_sb, st.astype(BF16), preferred_element_type=F32) * ea_e[rs, :]

            pieces = []
            for pe in range(hpg // 2):
                x_pair = xdt_b[rs, pe * LANES:(pe + 1) * LANES]
                acc = None
                for half in range(2):
                    h = g * hpg + 2 * pe + half
                    seg = a_cum[rs, h:h + 1] - a_cum_t[h:h + 1, rs]
                    decay = jnp.exp(jnp.where(causal, seg, -jnp.inf))
                    m_h = (cb * decay).astype(BF16)
                    keep = low_half if half == 0 else jnp.logical_not(low_half)
                    x_h = jnp.where(keep, x_pair, jnp.zeros_like(x_pair))
                    term = jnp.dot(m_h, x_h, preferred_element_type=F32)
                    acc = term if acc is None else acc + term
                pieces.append(acc)
            y_rows.append(y + jnp.concatenate(pieces, axis=1))

            upd = jnp.dot(b_s.T.astype(BF16), xds_b[rs, :], preferred_element_type=F32)
            st = st * ea_e[(s + 1) * sub - 1:(s + 1) * sub, :] + upd
        state_ref[g] = st

        y = jnp.concatenate(y_rows, axis=0) + dskip_ref[:, cs] * xs_g
        z_g = z_ref[:, cs]
        y = y * (z_g * _sigmoid(z_g))
        y = y * lax.rsqrt(jnp.mean(y * y, axis=-1, keepdims=True) + NORM_EPS)
        yssm_ref[:, cs] = (y * ng_ref[:, cs]).astype(yssm_ref.dtype)


def _head_expand_matrix():
    e = np.zeros((2 * LANES, SSM_D_INNER), np.float32)
    for h in range(SSM_HEADS):
        e[h, h * SSM_HEAD_DIM:(h + 1) * SSM_HEAD_DIM] = 1.0
        e[LANES + h, h * SSM_HEAD_DIM:(h + 1) * SSM_HEAD_DIM] = 1.0
    return jnp.asarray(e, BF16)


def ssd_and_short_conv(proj, dt_raw, conv_w, conv_b, dt_bias, a_log, d_skip, norm_g, sc_w, layer, batch, seq):
    t = proj.shape[0]
    L = SSM_CHUNK
    nc = seq // L
    row_map = lambda b, c: (b * nc + c, 0)

    def col_block(off, width):
        return pl.BlockSpec((L, width), lambda b, c: (b * nc + c, off // width))

    def param(rows, cols):
        return pl.BlockSpec((None, rows, cols), lambda b, c: (layer, 0, 0))

    return pl.pallas_call(
        _mixer_body,
        grid=(batch, nc),
        in_specs=[
            col_block(XBC_OFF, SSM_CONV_DIM),
            col_block(SC_OFF, 3 * SC_WIDTH),
            col_block(Z_OFF, SSM_D_INNER),
            pl.BlockSpec((L, LANES), row_map),
            param(SSM_CONV, SSM_CONV_DIM),
            param(1, SSM_CONV_DIM),
            param(1, LANES),
            param(1, LANES),
            param(1, SSM_D_INNER),
            param(1, SSM_D_INNER),
            param(SC_CONV, SC_WIDTH),
            pl.BlockSpec((2 * LANES, SSM_D_INNER), lambda b, c: (0, 0)),
        ],
        out_specs=[
            pl.BlockSpec((L, SSM_D_INNER), row_map),
            pl.BlockSpec((L, SC_WIDTH), row_map),
        ],
        out_shape=[
            jax.ShapeDtypeStruct((t, SSM_D_INNER), BF16),
            jax.ShapeDtypeStruct((t, SC_WIDTH), BF16),
        ],
        scratch_shapes=[
            pltpu.VMEM((L + 2 * SUBLANES, SSM_CONV_DIM), F32),
            pltpu.VMEM((L + 2 * SUBLANES, SC_WIDTH), F32),
            pltpu.VMEM((L, SSM_CONV_DIM), F32),
            pltpu.VMEM((SSM_GROUPS, SSM_STATE, SSM_D_INNER // SSM_GROUPS), F32),
        ],
        compiler_params=_params("parallel", "arbitrary"),
        name="ssd_short_conv",
    )(proj, proj, proj, dt_raw, conv_w, conv_b, dt_bias, a_log, d_skip, norm_g, sc_w, _head_expand_matrix())


def _merge_body(ya_ref, ym_ref, yc_ref, ga_ref, gm_ref, gc_ref, wa_ref, wm_ref, wc_ref, o_ref):
    a = jnp.dot(ya_ref[...], wa_ref[...], preferred_element_type=F32)
    m = jnp.dot(ym_ref[...], wm_ref[...], preferred_element_type=F32)
    c = jnp.dot(yc_ref[...], wc_ref[...], preferred_element_type=F32)
    o = _sigmoid(ga_ref[...]) * a + _sigmoid(gm_ref[...]) * m + _sigmoid(gc_ref[...]) * c
    o_ref[...] = o.astype(o_ref.dtype)


def branch_merge(y_attn, y_ssm, y_sc, proj, w_a, w_m, w_c, layer, tm=1024, tn=512):
    t = y_attn.shape[0]
    d = w_a.shape[-1]
    tm = min(tm, t)
    gc = GATE_OFF // tn
    gs = d // tn

    def act(width):
        return pl.BlockSpec((tm, width), lambda i, j: (i, 0))

    def gate(k):
        return pl.BlockSpec((tm, tn), lambda i, j: (i, gc + k * gs + j))

    def weight(rows):
        return pl.BlockSpec((None, rows, tn), lambda i, j: (layer, 0, j))

    return pl.pallas_call(
        _merge_body,
        grid=(t // tm, d // tn),
        in_specs=[act(y_attn.shape[1]), act(y_ssm.shape[1]), act(y_sc.shape[1]),
                  gate(0), gate(1), gate(2),
                  weight(w_a.shape[1]), weight(w_m.shape[1]), weight(w_c.shape[1])],
        out_specs=pl.BlockSpec((tm, tn), lambda i, j: (i, j)),
        out_shape=jax.ShapeDtypeStruct((t, d), BF16),
        compiler_params=_params("parallel", "arbitrary"),
        name="branch_merge",
    )(y_attn, y_ssm, y_sc, proj, proj, proj, w_a, w_m, w_c)


def _outproj_body(m_ref, x_ref, w_ref, gpost_ref, gnext_ref, o_ref, h_ref, wb_ref):
    @pl.when(pl.program_id(0) == 0)
    def _():
        wb_ref[...] = w_ref[...].astype(BF16)

    for rows in _row_chunks(m_ref.shape[0]):
        o = jnp.dot(m_ref[rows, :], wb_ref[...], preferred_element_type=F32)
        x = x_ref[rows, :] + _rms(o, gpost_ref[...])
        o_ref[rows, :] = x
        h_ref[rows, :] = _rms(x, gnext_ref[...]).astype(h_ref.dtype)


def out_projection(merged, x, w, g_post, g_next, layer, tm=512):
    t, d = x.shape
    tm = min(tm, t)
    rows = pl.BlockSpec((tm, d), lambda i: (i, 0))
    gain = pl.BlockSpec((None, 1, d), lambda i: (layer, 0, 0))
    return pl.pallas_call(
        _outproj_body,
        grid=(t // tm,),
        in_specs=[rows, rows, pl.BlockSpec((None, d, d), lambda i: (layer, 0, 0), pipeline_mode=pl.Buffered(1)),
                  gain, gain],
        out_specs=[rows, rows],
        out_shape=[jax.ShapeDtypeStruct((t, d), F32), jax.ShapeDtypeStruct((t, d), BF16)],
        scratch_shapes=[pltpu.VMEM((d, d), BF16)],
        compiler_params=_params("arbitrary"),
        name="out_projection",
    )(merged, x, w, g_post, g_next)


def _mlp_body(h_ref, x_ref, w1_ref, w2_ref, gpost_ref, o_ref):
    f = pl.program_id(1)

    @pl.when(f == 0)
    def _():
        o_ref[...] = jnp.zeros(o_ref.shape, F32)

    u = jnp.maximum(jnp.dot(h_ref[...], w1_ref[...].astype(BF16), preferred_element_type=F32), 0.0)
    o_ref[...] += jnp.dot((u * u).astype(BF16), w2_ref[...].astype(BF16), preferred_element_type=F32)

    @pl.when(f == pl.num_programs(1) - 1)
    def _():
        o_ref[...] = x_ref[...] + _rms(o_ref[...], gpost_ref[...])


def mlp(h, x, w1, w2, g_post, layer, tm=1024, tf=512):
    t, d = x.shape
    ff = w1.shape[-1]
    tm = min(tm, t)
    return pl.pallas_call(
        _mlp_body,
        grid=(t // tm, ff // tf),
        in_specs=[
            pl.BlockSpec((tm, d), lambda i, f: (i, 0)),
            pl.BlockSpec((tm, d), lambda i, f: (i, 0), pipeline_mode=pl.Buffered(1)),
            pl.BlockSpec((None, d, tf), lambda i, f: (layer, 0, f)),
            pl.BlockSpec((None, tf, d), lambda i, f: (layer, f, 0)),
            pl.BlockSpec((None, 1, d), lambda i, f: (layer, 0, 0)),
        ],
        out_specs=pl.BlockSpec((tm, d), lambda i, f: (i, 0)),
        out_shape=jax.ShapeDtypeStruct((t, d), F32),
        compiler_params=_params("parallel", "arbitrary"),
        name="mlp",
    )(h, x, w1, w2, g_post)


def _ple_body(x_ref, p_ref, gpre_ref, wg_ref, wp_ref, gpost_ref, *rest):
    wgb_ref, wpb_ref = rest[-2:]

    @pl.when(pl.program_id(0) == 0)
    def _():
        wgb_ref[...] = wg_ref[...].astype(BF16)
        wpb_ref[...] = wp_ref[...].astype(BF16)

    for rows in _row_chunks(x_ref.shape[0]):
        x = x_ref[rows, :]
        h = _rms(x, gpre_ref[...]).astype(BF16)
        gate = _sigmoid(jnp.dot(h, wgb_ref[...], preferred_element_type=F32))
        emb = jnp.dot(p_ref[rows, :].astype(BF16), wpb_ref[...], preferred_element_type=F32)
        x = x + _rms(gate * emb, gpost_ref[...])
        if len(rest) == 3:
            o_ref = rest[0]
        else:
            gnext_ref, o_ref, h_ref = rest[:3]
            h_ref[rows, :] = _rms(x, gnext_ref[...]).astype(h_ref.dtype)
        o_ref[rows, :] = x


def per_layer_embedding(x, p, g_pre, w_gate, w_proj, g_post, layer, g_next=None, tm=512):
    t, d = x.shape
    pd = p.shape[-1]
    tm = min(tm, t)
    rows = pl.BlockSpec((tm, d), lambda i: (i, 0))
    gain = pl.BlockSpec((None, 1, d), lambda i: (layer, 0, 0))
    in_specs = [
        rows,
        pl.BlockSpec((None, tm, pd), lambda i: (layer, i, 0)),
        gain,
        pl.BlockSpec((None, d, d), lambda i: (layer, 0, 0), pipeline_mode=pl.Buffered(1)),
        pl.BlockSpec((None, pd, d), lambda i: (layer, 0, 0), pipeline_mode=pl.Buffered(1)),
        gain,
    ]
    args = [x, p, g_pre, w_gate, w_proj, g_post]
    out_specs = [rows]
    out_shape = [jax.ShapeDtypeStruct((t, d), F32)]
    if g_next is not None:
        in_specs.append(pl.BlockSpec((None, 1, d), lambda i: (layer + 1, 0, 0)))
        args.append(g_next)
        out_specs.append(rows)
        out_shape.append(jax.ShapeDtypeStruct((t, d), BF16))
    return pl.pallas_call(
        _ple_body,
        grid=(t // tm,),
        in_specs=in_specs,
        out_specs=out_specs,
        out_shape=out_shape,
        scratch_shapes=[pltpu.VMEM((d, d), BF16), pltpu.VMEM((pd, d), BF16)],
        compiler_params=_params("arbitrary"),
        name="per_layer_embedding",
    )(*args)


def _row(v):
    return v[:, None, :]


def _lane_pad(v):
    return jnp.pad(v, ((0, 0), (0, LANES - v.shape[-1])))[:, None, :]


def kernel(x, p, rel_bias, g_mix_pre, w_in, ssm_conv_w, ssm_conv_b, ssm_dt_bias, ssm_a_log, ssm_d, ssm_norm_g, sc_conv_w, w_br_attn, w_br_ssm, w_br_conv, w_out, g_mix_post, g_mlp_pre, w_mlp_up, w_mlp_down, g_mlp_post, g_ple_pre, w_ple_gate, w_ple_proj, g_ple_post):
    batch, seq, d = x.shape
    depth = w_in.shape[0]
    t = batch * seq

    w_in_t = jnp.swapaxes(w_in, 1, 2)
    w_a, w_m, w_c = w_br_attn.astype(BF16), w_br_ssm.astype(BF16), w_br_conv.astype(BF16)
    d_skip = jnp.repeat(ssm_d, SSM_HEAD_DIM, axis=-1)[:, None, :]
    dt_bias, a_log = _lane_pad(ssm_dt_bias), _lane_pad(ssm_a_log)
    conv_b, norm_g = _row(ssm_conv_b), _row(ssm_norm_g)
    g_mix_pre, g_mix_post = _row(g_mix_pre), _row(g_mix_post)
    g_mlp_pre, g_mlp_post = _row(g_mlp_pre), _row(g_mlp_post)
    g_ple_pre, g_ple_post = _row(g_ple_pre), _row(g_ple_post)

    bias = bias_tiles(rel_bias)
    xf = x.reshape(t, d)
    pf = p.reshape(depth, t, p.shape[-1])
    h = pre_norm(xf, g_mix_pre, 0)
    for i in range(depth):
        proj = in_projection(h, w_in_t, i)
        y_attn = moba_attention(proj, bias, batch, seq, QKV_OFF)
        dt_raw = dt_projection(h, w_in_t, i)
        y_ssm, y_sc = ssd_and_short_conv(proj, dt_raw, ssm_conv_w, conv_b, dt_bias, a_log, d_skip, norm_g,
                                         sc_conv_w, i, batch, seq)
        merged = branch_merge(y_attn, y_ssm, y_sc, proj, w_a, w_m, w_c, i)
        xf, h_mlp = out_projection(merged, xf, w_out, g_mix_post, g_mlp_pre, i)
        xf = mlp(h_mlp, xf, w_mlp_up, w_mlp_down, g_mlp_post, i)
        if i + 1 < depth:
            xf, h = per_layer_embedding(xf, pf, g_ple_pre, w_ple_gate, w_ple_proj, g_ple_post, i, g_next=g_mix_pre)
        else:
            (xf,) = per_layer_embedding(xf, pf, g_ple_pre, w_ple_gate, w_ple_proj, g_ple_post, i)
    return xf.reshape(batch, seq, d)
```

```python
import functools
import math

import numpy as np
import jax
import jax.numpy as jnp
from jax import lax
from jax.experimental import pallas as pl
from jax.experimental.pallas import tpu as pltpu

F32 = jnp.float32
BF16 = jnp.bfloat16

NORM_EPS = 1e-6
NEG_INF = -1e30
LOG2E = math.log2(math.e)

D_MODEL = 2048
PLE_DIM = 256
ATTN_HEADS = 8
ATTN_HEAD_DIM = 128
ATTN_WIDTH = ATTN_HEADS * ATTN_HEAD_DIM
MOBA_BLOCK = 256
MOBA_TOPK = 3
REL_BUCKETS = 32
REL_MAX_DISTANCE = 128
SSM_D_INNER = D_MODEL
SSM_HEAD_DIM = 64
SSM_HEADS = SSM_D_INNER // SSM_HEAD_DIM
SSM_GROUPS = 4
SSM_STATE = 128
SSM_CONV = 4
SSM_CHUNK = 256
SSD_SUB = 128
SSM_CONV_DIM = SSM_D_INNER + 2 * SSM_GROUPS * SSM_STATE
SC_WIDTH = D_MODEL // 2
SC_CONV = 3
D_FF = 4 * D_MODEL

LANES = 128
SUBLANES = 8
VMEM_LIMIT_BYTES = 60 * 1024 * 1024

XBC_OFF = 0
SC_OFF = XBC_OFF + SSM_CONV_DIM
GATE_OFF = SC_OFF + 3 * SC_WIDTH
Z_OFF = GATE_OFF + 3 * D_MODEL
QKV_OFF = Z_OFF + SSM_D_INNER
PROJ_WIDTH = QKV_OFF + 3 * ATTN_WIDTH
PROJ_TN = 1024

_O_Q = 0
_O_Z = 3 * ATTN_WIDTH
_O_XBC = _O_Z + SSM_D_INNER
_O_DT = _O_XBC + SSM_CONV_DIM
_O_SC = _O_DT + SSM_HEADS
_O_GATE = _O_SC + 3 * SC_WIDTH
_O_END = _O_GATE + 3 * D_MODEL

_NT = (((1,), (1,)), ((), ()))


def _params(*sem):
    return pltpu.CompilerParams(dimension_semantics=sem, vmem_limit_bytes=VMEM_LIMIT_BYTES)


def _rms(x, g):
    return x * lax.rsqrt(jnp.mean(x * x, axis=-1, keepdims=True) + NORM_EPS) * g


def _sigmoid(x):
    return 1.0 / (1.0 + jnp.exp(-x))


def _row_chunks(n, parts=4):
    step = n // parts
    return [slice(k * step, (k + 1) * step) for k in range(parts)]


_T = PROJ_TN
_N_XBC, _N_TAIL, _N_Z, _N_QKV = SSM_CONV_DIM // _T, (_O_END - _O_SC) // _T, SSM_D_INNER // _T, 3 * ATTN_WIDTH // _T
_J_TAIL, _J_Z, _J_QKV = _N_XBC, _N_XBC + _N_TAIL, _N_XBC + _N_TAIL + _N_Z
assert (_J_QKV + _N_QKV) * _T == PROJ_WIDTH
assert all(o % SUBLANES == 0 for o in (_O_XBC, _O_SC, _O_Z, _O_Q, _O_DT)) and _T % SUBLANES == 0


def _src_row(j):
    s, ts = SUBLANES, _T // SUBLANES
    groups = jnp.where(j < _J_TAIL, _O_XBC // s + ts * j,
                       jnp.where(j < _J_Z, _O_SC // s + ts * (j - _J_TAIL),
                                 jnp.where(j < _J_QKV, _O_Z // s + ts * (j - _J_Z), _O_Q // s + ts * (j - _J_QKV))))
    return groups * s


def _norm_body(x_ref, g_ref, o_ref):
    o_ref[...] = _rms(x_ref[...], g_ref[...]).astype(o_ref.dtype)


def pre_norm(x, g, layer, tm=512):
    t, d = x.shape
    tm = min(tm, t)
    return pl.pallas_call(
        _norm_body,
        grid=(t // tm,),
        in_specs=[pl.BlockSpec((tm, d), lambda i: (i, 0)),
                  pl.BlockSpec((None, 1, d), lambda i: (layer, 0, 0))],
        out_specs=pl.BlockSpec((tm, d), lambda i: (i, 0)),
        out_shape=jax.ShapeDtypeStruct((t, d), BF16),
        compiler_params=_params("parallel"),
        name="pre_norm",
    )(x, g)


def _inproj_body(h_ref, wt_ref, o_ref):
    o_ref[...] = lax.dot_general(h_ref[...], wt_ref[0].astype(BF16), _NT, preferred_element_type=F32)


def _inproj_conv_body(h_ref, wt_ref, cw_ref, cb_ref, o_ref, *, chunk):
    j = pl.program_id(1)
    w = wt_ref[0].astype(BF16)
    tm, tn = o_ref.shape

    @pl.when(j >= _J_TAIL)
    def _():
        o_ref[...] = lax.dot_general(h_ref[...], w, _NT, preferred_element_type=F32)

    @pl.when(j < _J_TAIL)
    def _():
        width = cw_ref.shape[0]
        shape = (chunk // SUBLANES, SUBLANES, tn)
        sub = lax.broadcasted_iota(jnp.int32, shape, 1)
        taps = [cw_ref[k:k + 1, :].reshape(1, 1, tn) for k in range(width)]
        bias = cb_ref[...].reshape(1, 1, tn)

        def project(r):
            rows = slice(r * chunk, (r + 1) * chunk)
            return lax.dot_general(h_ref[rows, :], w, _NT, preferred_element_type=F32).reshape(shape)

        carry = jnp.zeros((1, SUBLANES, tn), F32)
        nxt = project(0)
        for r in range(tm // chunk):
            cur = nxt
            if (r + 1) * chunk < tm:
                nxt = project(r + 1)
            prev = jnp.concatenate([carry, cur[:-1]], axis=0)
            acc = taps[width - 1] * cur
            for s in range(1, width):
                acc = acc + taps[width - 1 - s] * pltpu.roll(jnp.where(sub >= SUBLANES - s, prev, cur), s, 1)
            v = acc + bias
            o_ref[r * chunk:(r + 1) * chunk, :] = (v * _sigmoid(v)).reshape(chunk, tn)
            carry = cur[-1:]


def in_projection(h, w_t, conv_w, conv_b, layer, seq, chunk=256):
    t, d = h.shape
    tn = _T
    tm = seq
    last_xbc = _N_XBC - 1
    return pl.pallas_call(
        functools.partial(_inproj_conv_body, chunk=chunk),
        grid=(t // tm, PROJ_WIDTH // tn),
        in_specs=[
            pl.BlockSpec((tm, d), lambda i, j: (i, 0), pipeline_mode=pl.Buffered(1)),
            pl.BlockSpec((pl.Element(1), pl.Element(tn), pl.Element(d)), lambda i, j: (layer, _src_row(j), 0)),
            pl.BlockSpec((None, SSM_CONV, tn), lambda i, j: (layer, 0, jnp.minimum(j, last_xbc))),
            pl.BlockSpec((None, 1, tn), lambda i, j: (layer, 0, jnp.minimum(j, last_xbc))),
        ],
        out_specs=pl.BlockSpec((tm, tn), lambda i, j: (i, j)),
        out_shape=jax.ShapeDtypeStruct((t, PROJ_WIDTH), F32),
        compiler_params=_params("parallel", "arbitrary"),
        name="in_projection",
    )(h, w_t, conv_w, conv_b)


def dt_projection(h, w_t, layer, tm=2048):
    t, d = h.shape
    tm = min(tm, t)
    return pl.pallas_call(
        _inproj_body,
        grid=(t // tm,),
        in_specs=[
            pl.BlockSpec((tm, d), lambda i: (i, 0)),
            pl.BlockSpec((pl.Element(1), pl.Element(LANES), pl.Element(d)), lambda i: (layer, _O_DT, 0)),
        ],
        out_specs=pl.BlockSpec((tm, LANES), lambda i: (i, 0)),
        out_shape=jax.ShapeDtypeStruct((t, LANES), F32),
        compiler_params=_params("parallel"),
        name="dt_projection",
    )(h, w_t)


def _bias_body(tbl_ref, o_ref):
    h = pl.program_id(0)
    blk = MOBA_BLOCK
    key = lax.broadcasted_iota(jnp.int32, (blk, blk), 0)
    qry = lax.broadcasted_iota(jnp.int32, (blk, blk), 1)
    max_exact = REL_BUCKETS // 2
    for t in range(3):
        n = jnp.maximum(t * blk + qry - key, 0)
        nf = jnp.maximum(n, 1).astype(F32)
        large = max_exact + (jnp.log(nf / max_exact) / math.log(REL_MAX_DISTANCE / max_exact)
                             * (REL_BUCKETS - max_exact)).astype(jnp.int32)
        large = jnp.minimum(large, REL_BUCKETS - 1)
        bucket = jnp.where(n < max_exact, n, large)
        acc = jnp.zeros((blk, blk), F32)
        for b in range(REL_BUCKETS):
            acc = jnp.where(bucket == b, tbl_ref[b, h], acc)
        o_ref[t] = acc


def bias_tiles(rel_bias):
    nh = rel_bias.shape[1]
    return pl.pallas_call(
        _bias_body,
        grid=(nh,),
        in_specs=[pl.BlockSpec(memory_space=pltpu.SMEM)],
        out_specs=pl.BlockSpec((None, 3, MOBA_BLOCK, MOBA_BLOCK), lambda h: (h, 0, 0, 0)),
        out_shape=jax.ShapeDtypeStruct((nh, 3, MOBA_BLOCK, MOBA_BLOCK), F32),
        compiler_params=_params("arbitrary"),
        name="bias_tiles",
    )(rel_bias)


def _attn_body(q_ref, k_ref, v_ref, bias_ref, o_ref, kmean_ref, *, nb):
    blk = MOBA_BLOCK
    scale = ATTN_HEAD_DIM ** -0.5

    k = k_ref[...]
    kb = k.astype(BF16)
    v_t = v_ref[...].T.astype(BF16)
    for j in range(nb):
        kmean_ref[j:j + 1, :] = jnp.mean(k[j * blk:(j + 1) * blk, :], axis=0, keepdims=True)
    q_all = q_ref[...]
    gate_all = lax.dot_general(kmean_ref[...], q_all, _NT, precision=lax.Precision.HIGHEST,
                               preferred_element_type=F32)
    qb_all = (q_all * (scale * LOG2E)).astype(BF16)

    key = lax.broadcasted_iota(jnp.int32, (blk, blk), 0)
    qry = lax.broadcasted_iota(jnp.int32, (blk, blk), 1)
    causal = key <= qry
    bias_own = bias_ref[0] * LOG2E
    bias_prev = bias_ref[1] * LOG2E
    bias_far = bias_ref[2, 0:1, 0:1] * LOG2E
    bidx = lax.broadcasted_iota(jnp.int32, (gate_all.shape[0], blk), 0)

    def scores(qi):
        return lax.dot_general(kb[0:(qi + 1) * blk, :], qb_all[qi * blk:(qi + 1) * blk, :], _NT,
                               preferred_element_type=F32)

    def finish(cols, nk, p_t, denom):
        o_t = jnp.dot(v_t[:, 0:nk], p_t, preferred_element_type=F32)
        o_ref[cols, :] = (o_t / denom).T.astype(o_ref.dtype)

    pending = None
    s_next = scores(0)
    for qi in range(nb):
        cols = slice(qi * blk, (qi + 1) * blk)
        nk = (qi + 1) * blk
        s = s_next
        if qi + 1 < nb:
            s_next = scores(qi + 1)

        keep = None
        if qi > MOBA_TOPK:
            gate = gate_all[:, cols]
            past = bidx < qi
            keep = []
            for j in range(qi):
                g_j = gate[j:j + 1, :]
                beats = ((gate > g_j) | ((gate == g_j) & (bidx < j))) & past
                keep.append(jnp.sum(beats.astype(F32), axis=0, keepdims=True) < MOBA_TOPK)

        pieces = []
        for j in range(qi + 1):
            s_j = s[j * blk:(j + 1) * blk, :]
            if j == qi:
                s_j = jnp.where(causal, s_j + bias_own, NEG_INF)
            elif j == qi - 1:
                s_j = s_j + bias_prev
                if keep is not None:
                    s_j = jnp.where(keep[j], s_j, NEG_INF)
            elif keep is not None:
                s_j = s_j + jnp.where(keep[j], bias_far, NEG_INF)
            else:
                s_j = s_j + bias_far
            pieces.append(s_j)

        m = pieces[0]
        for s_j in pieces[1:]:
            m = jnp.maximum(m, s_j)
        m = jnp.max(m, axis=0, keepdims=True)
        probs = [jnp.exp2(s_j - m) for s_j in pieces]
        total = probs[0]
        for p_j in probs[1:]:
            total = total + p_j
        denom = jnp.sum(total, axis=0, keepdims=True)
        p_t = jnp.concatenate([p_j.astype(BF16) for p_j in probs], axis=0)
        if pending is not None:
            finish(*pending)
        pending = (cols, nk, p_t, denom)
    finish(*pending)


def moba_attention(proj, bias, batch, seq, qkv_off):
    t = proj.shape[0]
    blk, dh, nh = MOBA_BLOCK, ATTN_HEAD_DIM, ATTN_HEADS
    nb = seq // blk
    qc = qkv_off // dh
    return pl.pallas_call(
        functools.partial(_attn_body, nb=nb),
        grid=(batch, nh),
        in_specs=[
            pl.BlockSpec((seq, dh), lambda b, h: (b, qc + h)),
            pl.BlockSpec((seq, dh), lambda b, h: (b, qc + nh + h)),
            pl.BlockSpec((seq, dh), lambda b, h: (b, qc + 2 * nh + h)),
            pl.BlockSpec((None, 3, blk, blk), lambda b, h: (h, 0, 0, 0)),
        ],
        out_specs=pl.BlockSpec((seq, dh), lambda b, h: (b, h)),
        out_shape=jax.ShapeDtypeStruct((t, nh * dh), BF16),
        scratch_shapes=[pltpu.VMEM((max(nb, SUBLANES), dh), F32)],
        compiler_params=_params("parallel", "parallel"),
        name="moba_attention",
    )(proj, proj, proj, bias)


def _hi_lo(v):
    hi = v.astype(BF16)
    lo = (v - hi.astype(F32)).astype(BF16)
    return jnp.concatenate([hi, lo], axis=1)


def _causal_taps(ext_ref, cs, w_ref, rows):
    width = w_ref.shape[0]
    cols = w_ref[:, cs].shape[1]
    shape = (rows // SUBLANES, SUBLANES, cols)
    prev = ext_ref[0:rows, cs].reshape(shape)
    cur = ext_ref[SUBLANES:SUBLANES + rows, cs].reshape(shape)
    sub = lax.broadcasted_iota(jnp.int32, shape, 1)
    acc = w_ref[width - 1:width, cs].reshape(1, 1, cols) * cur
    for j in range(1, width):
        shifted = pltpu.roll(jnp.where(sub >= SUBLANES - j, prev, cur), j, 1)
        acc = acc + w_ref[width - 1 - j:width - j, cs].reshape(1, 1, cols) * shifted
    return acc.reshape(rows, cols)


def _mixer_body(xc_ref, sc_ref, z_ref, dt_ref, dtb_ref, alog_ref, dskip_ref, ng_ref,
                scw_ref, e_ref, yssm_ref, ysc_ref, ext2_ref, state_ref):
    c = pl.program_id(1)
    L = SSM_CHUNK
    sub = SSD_SUB
    halo = SUBLANES
    gw = SSM_D_INNER // SSM_GROUPS
    hpg = SSM_HEADS // SSM_GROUPS
    n_state = SSM_STATE

    @pl.when(c == 0)
    def _():
        ext2_ref[0:halo, :] = jnp.zeros((halo, ext2_ref.shape[1]), F32)
        state_ref[...] = jnp.zeros(state_ref.shape, F32)

    @pl.when(c > 0)
    def _():
        ext2_ref[0:halo, :] = ext2_ref[L:L + halo, :]

    cchunk = 512
    ext2_ref[halo:halo + L, :] = sc_ref[:, SC_WIDTH:2 * SC_WIDTH] * sc_ref[:, 2 * SC_WIDTH:3 * SC_WIDTH]
    for ci in range(SC_WIDTH // cchunk):
        cs = slice(ci * cchunk, (ci + 1) * cchunk)
        ysc_ref[:, cs] = (sc_ref[:, cs] * _causal_taps(ext2_ref, cs, scw_ref, L)).astype(ysc_ref.dtype)

    rows = lax.broadcasted_iota(jnp.int32, (L, LANES), 0)
    head_lane = lax.broadcasted_iota(jnp.int32, (L, LANES), 1) < SSM_HEADS
    x_dt = jnp.where(head_lane, dt_ref[...], 0.0) + dtb_ref[...]
    dt = jnp.maximum(x_dt, 0.0) + jnp.log1p(jnp.exp(-jnp.abs(x_dt)))
    a = dt * (-jnp.exp(alog_ref[...]))
    pos = rows % sub
    a_cum = a
    sh = 1
    while sh < sub:
        a_cum = a_cum + jnp.where(pos >= sh, pltpu.roll(a_cum, sh, axis=0), 0.0)
        sh *= 2
    a_cum_t = a_cum.T
    ea = jnp.exp(a_cum)
    a_end = a_cum[L - 1:L, :]
    for s in range(L // sub - 2, -1, -1):
        a_end = jnp.where(rows < (s + 1) * sub, a_cum[(s + 1) * sub - 1:(s + 1) * sub, :], a_end)
    ds = jnp.exp(a_end - a_cum)
    dt_hl, ea_hl, ds_hl = _hi_lo(dt), _hi_lo(ea), _hi_lo(ds)

    row = lax.broadcasted_iota(jnp.int32, (sub, sub), 0)
    col = lax.broadcasted_iota(jnp.int32, (sub, sub), 1)
    causal = row >= col
    low_half = lax.broadcasted_iota(jnp.int32, (sub, LANES), 1) < SSM_HEAD_DIM

    for g in range(SSM_GROUPS):
        cs = slice(g * gw, (g + 1) * gw)
        b_cols = slice(SSM_D_INNER + g * n_state, SSM_D_INNER + (g + 1) * n_state)
        c_cols = slice(SSM_D_INNER + (SSM_GROUPS + g) * n_state, SSM_D_INNER + (SSM_GROUPS + g + 1) * n_state)
        e_g = e_ref[:, cs]
        xs_g = xc_ref[:, cs]
        ea_e = jnp.dot(ea_hl, e_g, preferred_element_type=F32)
        xdt = xs_g * jnp.dot(dt_hl, e_g, preferred_element_type=F32)
        xdt_b = xdt.astype(BF16)
        xds_b = (xdt * jnp.dot(ds_hl, e_g, preferred_element_type=F32)).astype(BF16)

        st = state_ref[g]
        y_rows = []
        for s in range(L // sub):
            rs = slice(s * sub, (s + 1) * sub)
            b_s = xc_ref[rs, b_cols]
            c_sb = xc_ref[rs, c_cols].astype(BF16)
            cb = lax.dot_general(c_sb, b_s.astype(BF16), _NT, preferred_element_type=F32)
            y = jnp.dot(c_sb, st.astype(BF16), preferred_element_type=F32) * ea_e[rs, :]

            pieces = []
            for pe in range(hpg // 2):
                x_pair = xdt_b[rs, pe * LANES:(pe + 1) * LANES]
                acc = None
                for half in range(2):
                    h = g * hpg + 2 * pe + half
                    seg = a_cum[rs, h:h + 1] - a_cum_t[h:h + 1, rs]
                    decay = jnp.exp(jnp.where(causal, seg, -jnp.inf))
                    m_h = (cb * decay).astype(BF16)
                    keep = low_half if half == 0 else jnp.logical_not(low_half)
                    x_h = jnp.where(keep, x_pair, jnp.zeros_like(x_pair))
                    term = jnp.dot(m_h, x_h, preferred_element_type=F32)
                    acc = term if acc is None else acc + term
                pieces.append(acc)
            y_rows.append(y + jnp.concatenate(pieces, axis=1))

            upd = jnp.dot(b_s.T.astype(BF16), xds_b[rs, :], preferred_element_type=F32)
            st = st * ea_e[(s + 1) * sub - 1:(s + 1) * sub, :] + upd
        state_ref[g] = st

        y = jnp.concatenate(y_rows, axis=0) + dskip_ref[:, cs] * xs_g
        z_g = z_ref[:, cs]
        y = y * (z_g * _sigmoid(z_g))
        y = y * lax.rsqrt(jnp.mean(y * y, axis=-1, keepdims=True) + NORM_EPS)
        yssm_ref[:, cs] = (y * ng_ref[:, cs]).astype(yssm_ref.dtype)


def _head_expand_matrix():
    e = np.zeros((2 * LANES, SSM_D_INNER), np.float32)
    for h in range(SSM_HEADS):
        e[h, h * SSM_HEAD_DIM:(h + 1) * SSM_HEAD_DIM] = 1.0
        e[LANES + h, h * SSM_HEAD_DIM:(h + 1) * SSM_HEAD_DIM] = 1.0
    return jnp.asarray(e, BF16)


def ssd_and_short_conv(proj, dt_raw, dt_bias, a_log, d_skip, norm_g, sc_w, layer, batch, seq):
    t = proj.shape[0]
    L = SSM_CHUNK
    nc = seq // L
    row_map = lambda b, c: (b * nc + c, 0)

    def col_block(off, width):
        return pl.BlockSpec((L, width), lambda b, c: (b * nc + c, off // width))

    def param(rows, cols):
        return pl.BlockSpec((None, rows, cols), lambda b, c: (layer, 0, 0))

    return pl.pallas_call(
        _mixer_body,
        grid=(batch, nc),
        in_specs=[
            col_block(XBC_OFF, SSM_CONV_DIM),
            col_block(SC_OFF, 3 * SC_WIDTH),
            col_block(Z_OFF, SSM_D_INNER),
            pl.BlockSpec((L, LANES), row_map),
            param(1, LANES),
            param(1, LANES),
            param(1, SSM_D_INNER),
            param(1, SSM_D_INNER),
            param(SC_CONV, SC_WIDTH),
            pl.BlockSpec((2 * LANES, SSM_D_INNER), lambda b, c: (0, 0)),
        ],
        out_specs=[
            pl.BlockSpec((L, SSM_D_INNER), row_map),
            pl.BlockSpec((L, SC_WIDTH), row_map),
        ],
        out_shape=[
            jax.ShapeDtypeStruct((t, SSM_D_INNER), BF16),
            jax.ShapeDtypeStruct((t, SC_WIDTH), BF16),
        ],
        scratch_shapes=[
            pltpu.VMEM((L + 2 * SUBLANES, SC_WIDTH), F32),
            pltpu.VMEM((SSM_GROUPS, SSM_STATE, SSM_D_INNER // SSM_GROUPS), F32),
        ],
        compiler_params=_params("parallel", "arbitrary"),
        name="ssd_short_conv",
    )(proj, proj, proj, dt_raw, dt_bias, a_log, d_skip, norm_g, sc_w, _head_expand_matrix())


def _merge_body(ya_ref, ym_ref, yc_ref, ga_ref, gm_ref, gc_ref, wa_ref, wm_ref, wc_ref, o_ref):
    a = jnp.dot(ya_ref[...], wa_ref[...], preferred_element_type=F32)
    m = jnp.dot(ym_ref[...], wm_ref[...], preferred_element_type=F32)
    c = jnp.dot(yc_ref[...], wc_ref[...], preferred_element_type=F32)
    o = _sigmoid(ga_ref[...]) * a + _sigmoid(gm_ref[...]) * m + _sigmoid(gc_ref[...]) * c
    o_ref[...] = o.astype(o_ref.dtype)


def branch_merge(y_attn, y_ssm, y_sc, proj, w_a, w_m, w_c, layer, tm=1024, tn=512):
    t = y_attn.shape[0]
    d = w_a.shape[-1]
    tm = min(tm, t)
    gc = GATE_OFF // tn
    gs = d // tn

    def act(width):
        return pl.BlockSpec((tm, width), lambda i, j: (i, 0))

    def gate(k):
        return pl.BlockSpec((tm, tn), lambda i, j: (i, gc + k * gs + j))

    def weight(rows):
        return pl.BlockSpec((None, rows, tn), lambda i, j: (layer, 0, j))

    return pl.pallas_call(
        _merge_body,
        grid=(t // tm, d // tn),
        in_specs=[act(y_attn.shape[1]), act(y_ssm.shape[1]), act(y_sc.shape[1]),
                  gate(0), gate(1), gate(2),
                  weight(w_a.shape[1]), weight(w_m.shape[1]), weight(w_c.shape[1])],
        out_specs=pl.BlockSpec((tm, tn), lambda i, j: (i, j)),
        out_shape=jax.ShapeDtypeStruct((t, d), BF16),
        compiler_params=_params("parallel", "arbitrary"),
        name="branch_merge",
    )(y_attn, y_ssm, y_sc, proj, proj, proj, w_a, w_m, w_c)


def _outproj_body(m_ref, x_ref, w_ref, gpost_ref, gnext_ref, o_ref, h_ref, wb_ref):
    @pl.when(pl.program_id(0) == 0)
    def _():
        wb_ref[...] = w_ref[...].astype(BF16)

    chunks = _row_chunks(m_ref.shape[0])
    project = lambda rows: jnp.dot(m_ref[rows, :], wb_ref[...], preferred_element_type=F32)
    nxt = project(chunks[0])
    for k, rows in enumerate(chunks):
        o = nxt
        if k + 1 < len(chunks):
            nxt = project(chunks[k + 1])
        x = x_ref[rows, :] + _rms(o, gpost_ref[...])
        o_ref[rows, :] = x
        h_ref[rows, :] = _rms(x, gnext_ref[...]).astype(h_ref.dtype)


def out_projection(merged, x, w, g_post, g_next, layer, tm=512):
    t, d = x.shape
    tm = min(tm, t)
    rows = pl.BlockSpec((tm, d), lambda i: (i, 0))
    gain = pl.BlockSpec((None, 1, d), lambda i: (layer, 0, 0))
    return pl.pallas_call(
        _outproj_body,
        grid=(t // tm,),
        in_specs=[rows, rows, pl.BlockSpec((None, d, d), lambda i: (layer, 0, 0), pipeline_mode=pl.Buffered(1)),
                  gain, gain],
        out_specs=[rows, rows],
        out_shape=[jax.ShapeDtypeStruct((t, d), F32), jax.ShapeDtypeStruct((t, d), BF16)],
        scratch_shapes=[pltpu.VMEM((d, d), BF16)],
        compiler_params=_params("arbitrary"),
        name="out_projection",
    )(merged, x, w, g_post, g_next)


def _mlp_body(h_ref, x_ref, w1_ref, w2_ref, gpost_ref, o_ref):
    f = pl.program_id(1)

    @pl.when(f == 0)
    def _():
        o_ref[...] = jnp.zeros(o_ref.shape, F32)

    u = jnp.maximum(jnp.dot(h_ref[...], w1_ref[...].astype(BF16), preferred_element_type=F32), 0.0)
    o_ref[...] += jnp.dot((u * u).astype(BF16), w2_ref[...].astype(BF16), preferred_element_type=F32)

    @pl.when(f == pl.num_programs(1) - 1)
    def _():
        o_ref[...] = x_ref[...] + _rms(o_ref[...], gpost_ref[...])


def mlp(h, x, w1, w2, g_post, layer, tm=1024, tf=512):
    t, d = x.shape
    ff = w1.shape[-1]
    tm = min(tm, t)
    return pl.pallas_call(
        _mlp_body,
        grid=(t // tm, ff // tf),
        in_specs=[
            pl.BlockSpec((tm, d), lambda i, f: (i, 0)),
            pl.BlockSpec((tm, d), lambda i, f: (i, 0), pipeline_mode=pl.Buffered(1)),
            pl.BlockSpec((None, d, tf), lambda i, f: (layer, 0, f)),
            pl.BlockSpec((None, tf, d), lambda i, f: (layer, f, 0)),
            pl.BlockSpec((None, 1, d), lambda i, f: (layer, 0, 0)),
        ],
        out_specs=pl.BlockSpec((tm, d), lambda i, f: (i, 0)),
        out_shape=jax.ShapeDtypeStruct((t, d), F32),
        compiler_params=_params("parallel", "arbitrary"),
        name="mlp",
    )(h, x, w1, w2, g_post)


def _ple_body(x_ref, p_ref, gpre_ref, wg_ref, wp_ref, gpost_ref, *rest):
    wgb_ref, wpb_ref = rest[-2:]

    @pl.when(pl.program_id(0) == 0)
    def _():
        wgb_ref[...] = wg_ref[...].astype(BF16)
        wpb_ref[...] = wp_ref[...].astype(BF16)

    chunks = _row_chunks(x_ref.shape[0])

    def project(rows):
        h = _rms(x_ref[rows, :], gpre_ref[...]).astype(BF16)
        return (jnp.dot(h, wgb_ref[...], preferred_element_type=F32),
                jnp.dot(p_ref[rows, :].astype(BF16), wpb_ref[...], preferred_element_type=F32))

    nxt = project(chunks[0])
    for k, rows in enumerate(chunks):
        pre_gate, emb = nxt
        if k + 1 < len(chunks):
            nxt = project(chunks[k + 1])
        x = x_ref[rows, :] + _rms(_sigmoid(pre_gate) * emb, gpost_ref[...])
        if len(rest) == 3:
            o_ref = rest[0]
        else:
            gnext_ref, o_ref, h_ref = rest[:3]
            h_ref[rows, :] = _rms(x, gnext_ref[...]).astype(h_ref.dtype)
        o_ref[rows, :] = x


def per_layer_embedding(x, p, g_pre, w_gate, w_proj, g_post, layer, g_next=None, tm=512):
    t, d = x.shape
    pd = p.shape[-1]
    tm = min(tm, t)
    rows = pl.BlockSpec((tm, d), lambda i: (i, 0))
    gain = pl.BlockSpec((None, 1, d), lambda i: (layer, 0, 0))
    in_specs = [
        rows,
        pl.BlockSpec((None, tm, pd), lambda i: (layer, i, 0)),
        gain,
        pl.BlockSpec((None, d, d), lambda i: (layer, 0, 0), pipeline_mode=pl.Buffered(1)),
        pl.BlockSpec((None, pd, d), lambda i: (layer, 0, 0), pipeline_mode=pl.Buffered(1)),
        gain,
    ]
    args = [x, p, g_pre, w_gate, w_proj, g_post]
    out_specs = [rows]
    out_shape = [jax.ShapeDtypeStruct((t, d), F32)]
    if g_next is not None:
        in_specs.append(pl.BlockSpec((None, 1, d), lambda i: (layer + 1, 0, 0)))
        args.append(g_next)
        out_specs.append(rows)
        out_shape.append(jax.ShapeDtypeStruct((t, d), BF16))
    return pl.pallas_call(
        _ple_body,
        grid=(t // tm,),
        in_specs=in_specs,
        out_specs=out_specs,
        out_shape=out_shape,
        scratch_shapes=[pltpu.VMEM((d, d), BF16), pltpu.VMEM((pd, d), BF16)],
        compiler_params=_params("arbitrary"),
        name="per_layer_embedding",
    )(*args)


def _row(v):
    return v[:, None, :]


def _lane_pad(v):
    return jnp.pad(v, ((0, 0), (0, LANES - v.shape[-1])))[:, None, :]


def kernel(x, p, rel_bias, g_mix_pre, w_in, ssm_conv_w, ssm_conv_b, ssm_dt_bias, ssm_a_log, ssm_d, ssm_norm_g, sc_conv_w, w_br_attn, w_br_ssm, w_br_conv, w_out, g_mix_post, g_mlp_pre, w_mlp_up, w_mlp_down, g_mlp_post, g_ple_pre, w_ple_gate, w_ple_proj, g_ple_post):
    batch, seq, d = x.shape
    depth = w_in.shape[0]
    t = batch * seq

    w_in_t = jnp.swapaxes(w_in, 1, 2)
    w_a, w_m, w_c = w_br_attn.astype(BF16), w_br_ssm.astype(BF16), w_br_conv.astype(BF16)
    d_skip = jnp.repeat(ssm_d, SSM_HEAD_DIM, axis=-1)[:, None, :]
    dt_bias, a_log = _lane_pad(ssm_dt_bias), _lane_pad(ssm_a_log)
    conv_b, norm_g = _row(ssm_conv_b), _row(ssm_norm_g)
    g_mix_pre, g_mix_post = _row(g_mix_pre), _row(g_mix_post)
    g_mlp_pre, g_mlp_post = _row(g_mlp_pre), _row(g_mlp_post)
    g_ple_pre, g_ple_post = _row(g_ple_pre), _row(g_ple_post)

    bias = bias_tiles(rel_bias)
    xf = x.reshape(t, d)
    pf = p.reshape(depth, t, p.shape[-1])
    h = pre_norm(xf, g_mix_pre, 0)
    for i in range(depth):
        proj = in_projection(h, w_in_t, ssm_conv_w, conv_b, i, seq)
        y_attn = moba_attention(proj, bias, batch, seq, QKV_OFF)
        dt_raw = dt_projection(h, w_in_t, i)
        y_ssm, y_sc = ssd_and_short_conv(proj, dt_raw, dt_bias, a_log, d_skip, norm_g, sc_conv_w, i, batch, seq)
        merged = branch_merge(y_attn, y_ssm, y_sc, proj, w_a, w_m, w_c, i)
        xf, h_mlp = out_projection(merged, xf, w_out, g_mix_post, g_mlp_pre, i)
        xf = mlp(h_mlp, xf, w_mlp_up, w_mlp_down, g_mlp_post, i)
        if i + 1 < depth:
            xf, h = per_layer_embedding(xf, pf, g_ple_pre, w_ple_gate, w_ple_proj, g_ple_post, i, g_next=g_mix_pre)
        else:
            (xf,) = per_layer_embedding(xf, pf, g_ple_pre, w_ple_gate, w_ple_proj, g_ple_post, i)
    return xf.reshape(batch, seq, d)
```

```python
import functools
import math

import numpy as np
import jax
import jax.numpy as jnp
from jax import lax
from jax.experimental import pallas as pl
from jax.experimental.pallas import tpu as pltpu

F32 = jnp.float32
BF16 = jnp.bfloat16

NORM_EPS = 1e-6
NEG_INF = -1e30
LOG2E = math.log2(math.e)

D_MODEL = 2048
PLE_DIM = 256
ATTN_HEADS = 8
ATTN_HEAD_DIM = 128
ATTN_WIDTH = ATTN_HEADS * ATTN_HEAD_DIM
MOBA_BLOCK = 256
MOBA_TOPK = 3
REL_BUCKETS = 32
REL_MAX_DISTANCE = 128
SSM_D_INNER = D_MODEL
SSM_HEAD_DIM = 64
SSM_HEADS = SSM_D_INNER // SSM_HEAD_DIM
SSM_GROUPS = 4
SSM_STATE = 128
SSM_CONV = 4
SSM_CHUNK = 256
SSD_SUB = 128
SSM_CONV_DIM = SSM_D_INNER + 2 * SSM_GROUPS * SSM_STATE
SC_WIDTH = D_MODEL // 2
SC_CONV = 3
D_FF = 4 * D_MODEL

LANES = 128
SUBLANES = 8
VMEM_LIMIT_BYTES = 60 * 1024 * 1024

XBC_OFF = 0
SC_OFF = XBC_OFF + SSM_CONV_DIM
GATE_OFF = SC_OFF + 3 * SC_WIDTH
Z_OFF = GATE_OFF + 3 * D_MODEL
QKV_OFF = Z_OFF + SSM_D_INNER
PROJ_WIDTH = QKV_OFF + 3 * ATTN_WIDTH
PROJ_TN = 1024

_O_Q = 0
_O_Z = 3 * ATTN_WIDTH
_O_XBC = _O_Z + SSM_D_INNER
_O_DT = _O_XBC + SSM_CONV_DIM
_O_SC = _O_DT + SSM_HEADS
_O_GATE = _O_SC + 3 * SC_WIDTH
_O_END = _O_GATE + 3 * D_MODEL

_NT = (((1,), (1,)), ((), ()))


def _params(*sem):
    return pltpu.CompilerParams(dimension_semantics=sem, vmem_limit_bytes=VMEM_LIMIT_BYTES)


def _rms(x, g):
    return x * lax.rsqrt(jnp.mean(x * x, axis=-1, keepdims=True) + NORM_EPS) * g


def _sigmoid(x):
    return 1.0 / (1.0 + jnp.exp(-x))


def _row_chunks(n, parts=4):
    step = n // parts
    return [slice(k * step, (k + 1) * step) for k in range(parts)]


_T = PROJ_TN
_N_XBC, _N_TAIL, _N_Z, _N_QKV = SSM_CONV_DIM // _T, (_O_END - _O_SC) // _T, SSM_D_INNER // _T, 3 * ATTN_WIDTH // _T
_J_TAIL, _J_Z, _J_QKV = _N_XBC, _N_XBC + _N_TAIL, _N_XBC + _N_TAIL + _N_Z
assert (_J_QKV + _N_QKV) * _T == PROJ_WIDTH
assert all(o % SUBLANES == 0 for o in (_O_XBC, _O_SC, _O_Z, _O_Q, _O_DT)) and _T % SUBLANES == 0


def _src_row(j):
    s, ts = SUBLANES, _T // SUBLANES
    groups = jnp.where(j < _J_TAIL, _O_XBC // s + ts * j,
                       jnp.where(j < _J_Z, _O_SC // s + ts * (j - _J_TAIL),
                                 jnp.where(j < _J_QKV, _O_Z // s + ts * (j - _J_Z), _O_Q // s + ts * (j - _J_QKV))))
    return groups * s


def _norm_body(x_ref, g_ref, o_ref):
    o_ref[...] = _rms(x_ref[...], g_ref[...]).astype(o_ref.dtype)


def pre_norm(x, g, layer, tm=512):
    t, d = x.shape
    tm = min(tm, t)
    return pl.pallas_call(
        _norm_body,
        grid=(t // tm,),
        in_specs=[pl.BlockSpec((tm, d), lambda i: (i, 0)),
                  pl.BlockSpec((None, 1, d), lambda i: (layer, 0, 0))],
        out_specs=pl.BlockSpec((tm, d), lambda i: (i, 0)),
        out_shape=jax.ShapeDtypeStruct((t, d), BF16),
        compiler_params=_params("parallel"),
        name="pre_norm",
    )(x, g)


def _inproj_body(h_ref, wt_ref, o_ref):
    o_ref[...] = lax.dot_general(h_ref[...], wt_ref[0].astype(BF16), _NT, preferred_element_type=F32)


def _inproj_conv_body(h_ref, wt_ref, cw_ref, cb_ref, o_ref, *, chunk):
    j = pl.program_id(1)
    w = wt_ref[0].astype(BF16)
    tm, tn = o_ref.shape

    @pl.when(j >= _J_TAIL)
    def _():
        o_ref[...] = lax.dot_general(h_ref[...], w, _NT, preferred_element_type=F32)

    @pl.when(j < _J_TAIL)
    def _():
        width = cw_ref.shape[0]
        shape = (chunk // SUBLANES, SUBLANES, tn)
        sub = lax.broadcasted_iota(jnp.int32, shape, 1)
        taps = [cw_ref[k:k + 1, :].reshape(1, 1, tn) for k in range(width)]
        bias = cb_ref[...].reshape(1, 1, tn)

        def project(r):
            rows = slice(r * chunk, (r + 1) * chunk)
            return lax.dot_general(h_ref[rows, :], w, _NT, preferred_element_type=F32).reshape(shape)

        carry = jnp.zeros((1, SUBLANES, tn), F32)
        nxt = project(0)
        for r in range(tm // chunk):
            cur = nxt
            if (r + 1) * chunk < tm:
                nxt = project(r + 1)
            prev = jnp.concatenate([carry, cur[:-1]], axis=0)
            acc = taps[width - 1] * cur
            for s in range(1, width):
                acc = acc + taps[width - 1 - s] * pltpu.roll(jnp.where(sub >= SUBLANES - s, prev, cur), s, 1)
            v = acc + bias
            o_ref[r * chunk:(r + 1) * chunk, :] = (v * _sigmoid(v)).reshape(chunk, tn)
            carry = cur[-1:]


def in_projection(h, w_t, conv_w, conv_b, layer, seq, chunk=1024):
    t, d = h.shape
    tn = _T
    tm = seq
    last_xbc = _N_XBC - 1
    return pl.pallas_call(
        functools.partial(_inproj_conv_body, chunk=chunk),
        grid=(t // tm, PROJ_WIDTH // tn),
        in_specs=[
            pl.BlockSpec((tm, d), lambda i, j: (i, 0), pipeline_mode=pl.Buffered(1)),
            pl.BlockSpec((pl.Element(1), pl.Element(tn), pl.Element(d)), lambda i, j: (layer, _src_row(j), 0)),
            pl.BlockSpec((None, SSM_CONV, tn), lambda i, j: (layer, 0, jnp.minimum(j, last_xbc))),
            pl.BlockSpec((None, 1, tn), lambda i, j: (layer, 0, jnp.minimum(j, last_xbc))),
        ],
        out_specs=pl.BlockSpec((tm, tn), lambda i, j: (i, j)),
        out_shape=jax.ShapeDtypeStruct((t, PROJ_WIDTH), F32),
        compiler_params=_params("parallel", "arbitrary"),
        name="in_projection",
    )(h, w_t, conv_w, conv_b)


def dt_projection(h, w_t, layer, tm=2048):
    t, d = h.shape
    tm = min(tm, t)
    return pl.pallas_call(
        _inproj_body,
        grid=(t // tm,),
        in_specs=[
            pl.BlockSpec((tm, d), lambda i: (i, 0)),
            pl.BlockSpec((pl.Element(1), pl.Element(LANES), pl.Element(d)), lambda i: (layer, _O_DT, 0)),
        ],
        out_specs=pl.BlockSpec((tm, LANES), lambda i: (i, 0)),
        out_shape=jax.ShapeDtypeStruct((t, LANES), F32),
        compiler_params=_params("parallel"),
        name="dt_projection",
    )(h, w_t)


def _bias_body(tbl_ref, o_ref):
    h = pl.program_id(0)
    blk = MOBA_BLOCK
    key = lax.broadcasted_iota(jnp.int32, (blk, blk), 0)
    qry = lax.broadcasted_iota(jnp.int32, (blk, blk), 1)
    max_exact = REL_BUCKETS // 2
    for t in range(3):
        n = jnp.maximum(t * blk + qry - key, 0)
        nf = jnp.maximum(n, 1).astype(F32)
        large = max_exact + (jnp.log(nf / max_exact) / math.log(REL_MAX_DISTANCE / max_exact)
                             * (REL_BUCKETS - max_exact)).astype(jnp.int32)
        large = jnp.minimum(large, REL_BUCKETS - 1)
        bucket = jnp.where(n < max_exact, n, large)
        acc = jnp.zeros((blk, blk), F32)
        for b in range(REL_BUCKETS):
            acc = jnp.where(bucket == b, tbl_ref[b, h], acc)
        o_ref[t] = acc


def bias_tiles(rel_bias):
    nh = rel_bias.shape[1]
    return pl.pallas_call(
        _bias_body,
        grid=(nh,),
        in_specs=[pl.BlockSpec(memory_space=pltpu.SMEM)],
        out_specs=pl.BlockSpec((None, 3, MOBA_BLOCK, MOBA_BLOCK), lambda h: (h, 0, 0, 0)),
        out_shape=jax.ShapeDtypeStruct((nh, 3, MOBA_BLOCK, MOBA_BLOCK), F32),
        compiler_params=_params("arbitrary"),
        name="bias_tiles",
    )(rel_bias)


def _attn_body(q_ref, k_ref, v_ref, bias_ref, o_ref, kmean_ref, *, nb):
    blk = MOBA_BLOCK
    scale = ATTN_HEAD_DIM ** -0.5

    k = k_ref[...]
    kb = k.astype(BF16)
    v_t = v_ref[...].T.astype(BF16)
    for j in range(nb):
        kmean_ref[j:j + 1, :] = jnp.mean(k[j * blk:(j + 1) * blk, :], axis=0, keepdims=True)
    q_all = q_ref[...]
    gate_all = lax.dot_general(kmean_ref[...], q_all, _NT, precision=lax.Precision.HIGHEST,
                               preferred_element_type=F32)
    qb_all = (q_all * (scale * LOG2E)).astype(BF16)

    key = lax.broadcasted_iota(jnp.int32, (blk, blk), 0)
    qry = lax.broadcasted_iota(jnp.int32, (blk, blk), 1)
    causal = key <= qry
    bias_own = bias_ref[0] * LOG2E
    bias_prev = bias_ref[1] * LOG2E
    bias_far = bias_ref[2, 0:1, 0:1] * LOG2E
    bidx = lax.broadcasted_iota(jnp.int32, (gate_all.shape[0], blk), 0)

    def scores(qi):
        return lax.dot_general(kb[0:(qi + 1) * blk, :], qb_all[qi * blk:(qi + 1) * blk, :], _NT,
                               preferred_element_type=F32)

    def finish(cols, nk, p_t, denom):
        o_t = jnp.dot(v_t[:, 0:nk], p_t, preferred_element_type=F32)
        o_ref[cols, :] = (o_t / denom).T.astype(o_ref.dtype)

    pending = None
    s_next = scores(0)
    for qi in range(nb):
        cols = slice(qi * blk, (qi + 1) * blk)
        nk = (qi + 1) * blk
        s = s_next
        if qi + 1 < nb:
            s_next = scores(qi + 1)

        keep = None
        if qi > MOBA_TOPK:
            gate = gate_all[:, cols]
            past = bidx < qi
            keep = []
            for j in range(qi):
                g_j = gate[j:j + 1, :]
                beats = ((gate > g_j) | ((gate == g_j) & (bidx < j))) & past
                keep.append(jnp.sum(beats.astype(F32), axis=0, keepdims=True) < MOBA_TOPK)

        pieces = []
        for j in range(qi + 1):
            s_j = s[j * blk:(j + 1) * blk, :]
            if j == qi:
                s_j = jnp.where(causal, s_j + bias_own, NEG_INF)
            elif j == qi - 1:
                s_j = s_j + bias_prev
                if keep is not None:
                    s_j = jnp.where(keep[j], s_j, NEG_INF)
            elif keep is not None:
                s_j = s_j + jnp.where(keep[j], bias_far, NEG_INF)
            else:
                s_j = s_j + bias_far
            pieces.append(s_j)

        m = pieces[0]
        for s_j in pieces[1:]:
            m = jnp.maximum(m, s_j)
        m = jnp.max(m, axis=0, keepdims=True)
        probs = [jnp.exp2(s_j - m) for s_j in pieces]
        total = probs[0]
        for p_j in probs[1:]:
            total = total + p_j
        denom = jnp.sum(total, axis=0, keepdims=True)
        p_t = jnp.concatenate([p_j.astype(BF16) for p_j in probs], axis=0)
        if pending is not None:
            finish(*pending)
        pending = (cols, nk, p_t, denom)
    finish(*pending)


def moba_attention(proj, bias, batch, seq, qkv_off):
    t = proj.shape[0]
    blk, dh, nh = MOBA_BLOCK, ATTN_HEAD_DIM, ATTN_HEADS
    nb = seq // blk
    qc = qkv_off // dh
    return pl.pallas_call(
        functools.partial(_attn_body, nb=nb),
        grid=(batch, nh),
        in_specs=[
            pl.BlockSpec((seq, dh), lambda b, h: (b, qc + h)),
            pl.BlockSpec((seq, dh), lambda b, h: (b, qc + nh + h)),
            pl.BlockSpec((seq, dh), lambda b, h: (b, qc + 2 * nh + h)),
            pl.BlockSpec((None, 3, blk, blk), lambda b, h: (h, 0, 0, 0)),
        ],
        out_specs=pl.BlockSpec((seq, dh), lambda b, h: (b, h)),
        out_shape=jax.ShapeDtypeStruct((t, nh * dh), BF16),
        scratch_shapes=[pltpu.VMEM((max(nb, SUBLANES), dh), F32)],
        compiler_params=_params("parallel", "parallel"),
        name="moba_attention",
    )(proj, proj, proj, bias)


def _hi_lo(v):
    hi = v.astype(BF16)
    lo = (v - hi.astype(F32)).astype(BF16)
    return jnp.concatenate([hi, lo], axis=1)


def _causal_taps(ext_ref, cs, w_ref, rows):
    width = w_ref.shape[0]
    cols = w_ref[:, cs].shape[1]
    shape = (rows // SUBLANES, SUBLANES, cols)
    prev = ext_ref[0:rows, cs].reshape(shape)
    cur = ext_ref[SUBLANES:SUBLANES + rows, cs].reshape(shape)
    sub = lax.broadcasted_iota(jnp.int32, shape, 1)
    acc = w_ref[width - 1:width, cs].reshape(1, 1, cols) * cur
    for j in range(1, width):
        shifted = pltpu.roll(jnp.where(sub >= SUBLANES - j, prev, cur), j, 1)
        acc = acc + w_ref[width - 1 - j:width - j, cs].reshape(1, 1, cols) * shifted
    return acc.reshape(rows, cols)


def _mixer_body(xc_ref, sc_ref, z_ref, dt_ref, dtb_ref, alog_ref, dskip_ref, ng_ref,
                scw_ref, e_ref, yssm_ref, ysc_ref, ext2_ref, state_ref):
    c = pl.program_id(1)
    L = SSM_CHUNK
    sub = SSD_SUB
    halo = SUBLANES
    gw = SSM_D_INNER // SSM_GROUPS
    hpg = SSM_HEADS // SSM_GROUPS
    n_state = SSM_STATE

    @pl.when(c == 0)
    def _():
        ext2_ref[0:halo, :] = jnp.zeros((halo, ext2_ref.shape[1]), F32)
        state_ref[...] = jnp.zeros(state_ref.shape, F32)

    @pl.when(c > 0)
    def _():
        ext2_ref[0:halo, :] = ext2_ref[L:L + halo, :]

    cchunk = 512
    ext2_ref[halo:halo + L, :] = sc_ref[:, SC_WIDTH:2 * SC_WIDTH] * sc_ref[:, 2 * SC_WIDTH:3 * SC_WIDTH]
    for ci in range(SC_WIDTH // cchunk):
        cs = slice(ci * cchunk, (ci + 1) * cchunk)
        ysc_ref[:, cs] = (sc_ref[:, cs] * _causal_taps(ext2_ref, cs, scw_ref, L)).astype(ysc_ref.dtype)

    rows = lax.broadcasted_iota(jnp.int32, (L, LANES), 0)
    head_lane = lax.broadcasted_iota(jnp.int32, (L, LANES), 1) < SSM_HEADS
    x_dt = jnp.where(head_lane, dt_ref[...], 0.0) + dtb_ref[...]
    dt = jnp.maximum(x_dt, 0.0) + jnp.log1p(jnp.exp(-jnp.abs(x_dt)))
    a = dt * (-jnp.exp(alog_ref[...]))
    pos = rows % sub
    a_cum = a
    sh = 1
    while sh < sub:
        a_cum = a_cum + jnp.where(pos >= sh, pltpu.roll(a_cum, sh, axis=0), 0.0)
        sh *= 2
    a_cum_t = a_cum.T
    ea = jnp.exp(a_cum)
    a_end = a_cum[L - 1:L, :]
    for s in range(L // sub - 2, -1, -1):
        a_end = jnp.where(rows < (s + 1) * sub, a_cum[(s + 1) * sub - 1:(s + 1) * sub, :], a_end)
    ds = jnp.exp(a_end - a_cum)
    dt_hl, ea_hl, ds_hl = _hi_lo(dt), _hi_lo(ea), _hi_lo(ds)

    row = lax.broadcasted_iota(jnp.int32, (sub, sub), 0)
    col = lax.broadcasted_iota(jnp.int32, (sub, sub), 1)
    causal = row >= col
    low_half = lax.broadcasted_iota(jnp.int32, (sub, LANES), 1) < SSM_HEAD_DIM

    for g in range(SSM_GROUPS):
        cs = slice(g * gw, (g + 1) * gw)
        b_cols = slice(SSM_D_INNER + g * n_state, SSM_D_INNER + (g + 1) * n_state)
        c_cols = slice(SSM_D_INNER + (SSM_GROUPS + g) * n_state, SSM_D_INNER + (SSM_GROUPS + g + 1) * n_state)
        e_g = e_ref[:, cs]
        xs_g = xc_ref[:, cs]
        ea_e = jnp.dot(ea_hl, e_g, preferred_element_type=F32)
        xdt = xs_g * jnp.dot(dt_hl, e_g, preferred_element_type=F32)
        xdt_b = xdt.astype(BF16)
        xds_b = (xdt * jnp.dot(ds_hl, e_g, preferred_element_type=F32)).astype(BF16)

        st = state_ref[g]
        y_rows = []
        for s in range(L // sub):
            rs = slice(s * sub, (s + 1) * sub)
            b_s = xc_ref[rs, b_cols]
            c_sb = xc_ref[rs, c_cols].astype(BF16)
            cb = lax.dot_general(c_sb, b_s.astype(BF16), _NT, preferred_element_type=F32)
            y = jnp.dot(c_sb, st.astype(BF16), preferred_element_type=F32) * ea_e[rs, :]

            pieces = []
            for pe in range(hpg // 2):
                x_pair = xdt_b[rs, pe * LANES:(pe + 1) * LANES]
                acc = None
                for half in range(2):
                    h = g * hpg + 2 * pe + half
                    seg = a_cum[rs, h:h + 1] - a_cum_t[h:h + 1, rs]
                    decay = jnp.exp(jnp.where(causal, seg, -jnp.inf))
                    m_h = (cb * decay).astype(BF16)
                    keep = low_half if half == 0 else jnp.logical_not(low_half)
                    x_h = jnp.where(keep, x_pair, jnp.zeros_like(x_pair))
                    term = jnp.dot(m_h, x_h, preferred_element_type=F32)
                    acc = term if acc is None else acc + term
                pieces.append(acc)
            y_rows.append(y + jnp.concatenate(pieces, axis=1))

            upd = jnp.dot(b_s.T.astype(BF16), xds_b[rs, :], preferred_element_type=F32)
            st = st * ea_e[(s + 1) * sub - 1:(s + 1) * sub, :] + upd
        state_ref[g] = st

        y = jnp.concatenate(y_rows, axis=0) + dskip_ref[:, cs] * xs_g
        z_g = z_ref[:, cs]
        y = y * (z_g * _sigmoid(z_g))
        y = y * lax.rsqrt(jnp.mean(y * y, axis=-1, keepdims=True) + NORM_EPS)
        yssm_ref[:, cs] = (y * ng_ref[:, cs]).astype(yssm_ref.dtype)


def _head_expand_matrix():
    e = np.zeros((2 * LANES, SSM_D_INNER), np.float32)
    for h in range(SSM_HEADS):
        e[h, h * SSM_HEAD_DIM:(h + 1) * SSM_HEAD_DIM] = 1.0
        e[LANES + h, h * SSM_HEAD_DIM:(h + 1) * SSM_HEAD_DIM] = 1.0
    return jnp.asarray(e, BF16)


def ssd_and_short_conv(proj, dt_raw, dt_bias, a_log, d_skip, norm_g, sc_w, layer, batch, seq):
    t = proj.shape[0]
    L = SSM_CHUNK
    nc = seq // L
    row_map = lambda b, c: (b * nc + c, 0)

    def col_block(off, width):
        return pl.BlockSpec((L, width), lambda b, c: (b * nc + c, off // width))

    def param(rows, cols):
        return pl.BlockSpec((None, rows, cols), lambda b, c: (layer, 0, 0))

    return pl.pallas_call(
        _mixer_body,
        grid=(batch, nc),
        in_specs=[
            col_block(XBC_OFF, SSM_CONV_DIM),
            col_block(SC_OFF, 3 * SC_WIDTH),
            col_block(Z_OFF, SSM_D_INNER),
            pl.BlockSpec((L, LANES), row_map),
            param(1, LANES),
            param(1, LANES),
            param(1, SSM_D_INNER),
            param(1, SSM_D_INNER),
            param(SC_CONV, SC_WIDTH),
            pl.BlockSpec((2 * LANES, SSM_D_INNER), lambda b, c: (0, 0)),
        ],
        out_specs=[
            pl.BlockSpec((L, SSM_D_INNER), row_map),
            pl.BlockSpec((L, SC_WIDTH), row_map),
        ],
        out_shape=[
            jax.ShapeDtypeStruct((t, SSM_D_INNER), BF16),
            jax.ShapeDtypeStruct((t, SC_WIDTH), BF16),
        ],
        scratch_shapes=[
            pltpu.VMEM((L + 2 * SUBLANES, SC_WIDTH), F32),
            pltpu.VMEM((SSM_GROUPS, SSM_STATE, SSM_D_INNER // SSM_GROUPS), F32),
        ],
        compiler_params=_params("parallel", "arbitrary"),
        name="ssd_short_conv",
    )(proj, proj, proj, dt_raw, dt_bias, a_log, d_skip, norm_g, sc_w, _head_expand_matrix())


def _merge_body(ya_ref, ym_ref, yc_ref, ga_ref, gm_ref, gc_ref, wa_ref, wm_ref, wc_ref, o_ref):
    a = jnp.dot(ya_ref[...], wa_ref[...], preferred_element_type=F32)
    m = jnp.dot(ym_ref[...], wm_ref[...], preferred_element_type=F32)
    c = jnp.dot(yc_ref[...], wc_ref[...], preferred_element_type=F32)
    o = _sigmoid(ga_ref[...]) * a + _sigmoid(gm_ref[...]) * m + _sigmoid(gc_ref[...]) * c
    o_ref[...] = o.astype(o_ref.dtype)


def branch_merge(y_attn, y_ssm, y_sc, proj, w_a, w_m, w_c, layer, tm=1024, tn=512):
    t = y_attn.shape[0]
    d = w_a.shape[-1]
    tm = min(tm, t)
    gc = GATE_OFF // tn
    gs = d // tn

    def act(width):
        return pl.BlockSpec((tm, width), lambda i, j: (i, 0))

    def gate(k):
        return pl.BlockSpec((tm, tn), lambda i, j: (i, gc + k * gs + j))

    def weight(rows):
        return pl.BlockSpec((None, rows, tn), lambda i, j: (layer, 0, j))

    return pl.pallas_call(
        _merge_body,
        grid=(t // tm, d // tn),
        in_specs=[act(y_attn.shape[1]), act(y_ssm.shape[1]), act(y_sc.shape[1]),
                  gate(0), gate(1), gate(2),
                  weight(w_a.shape[1]), weight(w_m.shape[1]), weight(w_c.shape[1])],
        out_specs=pl.BlockSpec((tm, tn), lambda i, j: (i, j)),
        out_shape=jax.ShapeDtypeStruct((t, d), BF16),
        compiler_params=_params("parallel", "arbitrary"),
        name="branch_merge",
    )(y_attn, y_ssm, y_sc, proj, proj, proj, w_a, w_m, w_c)


def _outproj_body(m_ref, x_ref, w_ref, gpost_ref, gnext_ref, o_ref, h_ref, wb_ref):
    @pl.when(pl.program_id(0) == 0)
    def _():
        wb_ref[...] = w_ref[...].astype(BF16)

    chunks = _row_chunks(m_ref.shape[0])
    project = lambda rows: jnp.dot(m_ref[rows, :], wb_ref[...], preferred_element_type=F32)
    nxt = project(chunks[0])
    for k, rows in enumerate(chunks):
        o = nxt
        if k + 1 < len(chunks):
            nxt = project(chunks[k + 1])
        x = x_ref[rows, :] + _rms(o, gpost_ref[...])
        o_ref[rows, :] = x
        h_ref[rows, :] = _rms(x, gnext_ref[...]).astype(h_ref.dtype)


def out_projection(merged, x, w, g_post, g_next, layer, tm=512):
    t, d = x.shape
    tm = min(tm, t)
    rows = pl.BlockSpec((tm, d), lambda i: (i, 0))
    gain = pl.BlockSpec((None, 1, d), lambda i: (layer, 0, 0))
    return pl.pallas_call(
        _outproj_body,
        grid=(t // tm,),
        in_specs=[rows, rows, pl.BlockSpec((None, d, d), lambda i: (layer, 0, 0), pipeline_mode=pl.Buffered(1)),
                  gain, gain],
        out_specs=[rows, rows],
        out_shape=[jax.ShapeDtypeStruct((t, d), F32), jax.ShapeDtypeStruct((t, d), BF16)],
        scratch_shapes=[pltpu.VMEM((d, d), BF16)],
        compiler_params=_params("arbitrary"),
        name="out_projection",
    )(merged, x, w, g_post, g_next)


def _mlp_body(h_ref, x_ref, w1_ref, w2_ref, gpost_ref, o_ref):
    f = pl.program_id(1)

    @pl.when(f == 0)
    def _():
        o_ref[...] = jnp.zeros(o_ref.shape, F32)

    u = jnp.maximum(jnp.dot(h_ref[...], w1_ref[...].astype(BF16), preferred_element_type=F32), 0.0)
    o_ref[...] += jnp.dot((u * u).astype(BF16), w2_ref[...].astype(BF16), preferred_element_type=F32)

    @pl.when(f == pl.num_programs(1) - 1)
    def _():
        o_ref[...] = x_ref[...] + _rms(o_ref[...], gpost_ref[...])


def mlp(h, x, w1, w2, g_post, layer, tm=1024, tf=512):
    t, d = x.shape
    ff = w1.shape[-1]
    tm = min(tm, t)
    return pl.pallas_call(
        _mlp_body,
        grid=(t // tm, ff // tf),
        in_specs=[
            pl.BlockSpec((tm, d), lambda i, f: (i, 0)),
            pl.BlockSpec((tm, d), lambda i, f: (i, 0), pipeline_mode=pl.Buffered(1)),
            pl.BlockSpec((None, d, tf), lambda i, f: (layer, 0, f)),
            pl.BlockSpec((None, tf, d), lambda i, f: (layer, f, 0)),
            pl.BlockSpec((None, 1, d), lambda i, f: (layer, 0, 0)),
        ],
        out_specs=pl.BlockSpec((tm, d), lambda i, f: (i, 0)),
        out_shape=jax.ShapeDtypeStruct((t, d), F32),
        compiler_params=_params("parallel", "arbitrary"),
        name="mlp",
    )(h, x, w1, w2, g_post)


def _ple_body(x_ref, p_ref, gpre_ref, wg_ref, wp_ref, gpost_ref, *rest):
    wgb_ref, wpb_ref = rest[-2:]

    @pl.when(pl.program_id(0) == 0)
    def _():
        wgb_ref[...] = wg_ref[...].astype(BF16)
        wpb_ref[...] = wp_ref[...].astype(BF16)

    chunks = _row_chunks(x_ref.shape[0])

    def project(rows):
        h = _rms(x_ref[rows, :], gpre_ref[...]).astype(BF16)
        return (jnp.dot(h, wgb_ref[...], preferred_element_type=F32),
                jnp.dot(p_ref[rows, :].astype(BF16), wpb_ref[...], preferred_element_type=F32))

    nxt = project(chunks[0])
    for k, rows in enumerate(chunks):
        pre_gate, emb = nxt
        if k + 1 < len(chunks):
            nxt = project(chunks[k + 1])
        x = x_ref[rows, :] + _rms(_sigmoid(pre_gate) * emb, gpost_ref[...])
        if len(rest) == 3:
            o_ref = rest[0]
        else:
            gnext_ref, o_ref, h_ref = rest[:3]
            h_ref[rows, :] = _rms(x, gnext_ref[...]).astype(h_ref.dtype)
        o_ref[rows, :] = x


def per_layer_embedding(x, p, g_pre, w_gate, w_proj, g_post, layer, g_next=None, tm=512):
    t, d = x.shape
    pd = p.shape[-1]
    tm = min(tm, t)
    rows = pl.BlockSpec((tm, d), lambda i: (i, 0))
    gain = pl.BlockSpec((None, 1, d), lambda i: (layer, 0, 0))
    in_specs = [
        rows,
        pl.BlockSpec((None, tm, pd), lambda i: (layer, i, 0)),
        gain,
        pl.BlockSpec((None, d, d), lambda i: (layer, 0, 0), pipeline_mode=pl.Buffered(1)),
        pl.BlockSpec((None, pd, d), lambda i: (layer, 0, 0), pipeline_mode=pl.Buffered(1)),
        gain,
    ]
    args = [x, p, g_pre, w_gate, w_proj, g_post]
    out_specs = [rows]
    out_shape = [jax.ShapeDtypeStruct((t, d), F32)]
    if g_next is not None:
        in_specs.append(pl.BlockSpec((None, 1, d), lambda i: (layer + 1, 0, 0)))
        args.append(g_next)
        out_specs.append(rows)
        out_shape.append(jax.ShapeDtypeStruct((t, d), BF16))
    return pl.pallas_call(
        _ple_body,
        grid=(t // tm,),
        in_specs=in_specs,
        out_specs=out_specs,
        out_shape=out_shape,
        scratch_shapes=[pltpu.VMEM((d, d), BF16), pltpu.VMEM((pd, d), BF16)],
        compiler_params=_params("arbitrary"),
        name="per_layer_embedding",
    )(*args)


def _row(v):
    return v[:, None, :]


def _lane_pad(v):
    return jnp.pad(v, ((0, 0), (0, LANES - v.shape[-1])))[:, None, :]


def kernel(x, p, rel_bias, g_mix_pre, w_in, ssm_conv_w, ssm_conv_b, ssm_dt_bias, ssm_a_log, ssm_d, ssm_norm_g, sc_conv_w, w_br_attn, w_br_ssm, w_br_conv, w_out, g_mix_post, g_mlp_pre, w_mlp_up, w_mlp_down, g_mlp_post, g_ple_pre, w_ple_gate, w_ple_proj, g_ple_post):
    batch, seq, d = x.shape
    depth = w_in.shape[0]
    t = batch * seq

    w_in_t = jnp.swapaxes(w_in, 1, 2)
    w_a, w_m, w_c = w_br_attn.astype(BF16), w_br_ssm.astype(BF16), w_br_conv.astype(BF16)
    d_skip = jnp.repeat(ssm_d, SSM_HEAD_DIM, axis=-1)[:, None, :]
    dt_bias, a_log = _lane_pad(ssm_dt_bias), _lane_pad(ssm_a_log)
    conv_b, norm_g = _row(ssm_conv_b), _row(ssm_norm_g)
    g_mix_pre, g_mix_post = _row(g_mix_pre), _row(g_mix_post)
    g_mlp_pre, g_mlp_post = _row(g_mlp_pre), _row(g_mlp_post)
    g_ple_pre, g_ple_post = _row(g_ple_pre), _row(g_ple_post)

    bias = bias_tiles(rel_bias)
    xf = x.reshape(t, d)
    pf = p.reshape(depth, t, p.shape[-1])
    h = pre_norm(xf, g_mix_pre, 0)
    for i in range(depth):
        proj = in_projection(h, w_in_t, ssm_conv_w, conv_b, i, seq)
        y_attn = moba_attention(proj, bias, batch, seq, QKV_OFF)
        dt_raw = dt_projection(h, w_in_t, i)
        y_ssm, y_sc = ssd_and_short_conv(proj, dt_raw, dt_bias, a_log, d_skip, norm_g, sc_conv_w, i, batch, seq)
        merged = branch_merge(y_attn, y_ssm, y_sc, proj, w_a, w_m, w_c, i)
        xf, h_mlp = out_projection(merged, xf, w_out, g_mix_post, g_mlp_pre, i)
        xf = mlp(h_mlp, xf, w_mlp_up, w_mlp_down, g_mlp_post, i)
        if i + 1 < depth:
            xf, h = per_layer_embedding(xf, pf, g_ple_pre, w_ple_gate, w_ple_proj, g_ple_post, i, g_next=g_mix_pre)
        else:
            (xf,) = per_layer_embedding(xf, pf, g_ple_pre, w_ple_gate, w_ple_proj, g_ple_post, i)
    return xf.reshape(batch, seq, d)
```

```python
import functools
import math

import numpy as np
import jax
import jax.numpy as jnp
from jax import lax
from jax.experimental import pallas as pl
from jax.experimental.pallas import tpu as pltpu

F32 = jnp.float32
BF16 = jnp.bfloat16

NORM_EPS = 1e-6
NEG_INF = -1e30
LOG2E = math.log2(math.e)

D_MODEL = 2048
PLE_DIM = 256
ATTN_HEADS = 8
ATTN_HEAD_DIM = 128
ATTN_WIDTH = ATTN_HEADS * ATTN_HEAD_DIM
MOBA_BLOCK = 256
MOBA_TOPK = 3
REL_BUCKETS = 32
REL_MAX_DISTANCE = 128
SSM_D_INNER = D_MODEL
SSM_HEAD_DIM = 64
SSM_HEADS = SSM_D_INNER // SSM_HEAD_DIM
SSM_GROUPS = 4
SSM_STATE = 128
SSM_CONV = 4
SSM_CHUNK = 256
SSD_SUB = 128
SSM_CONV_DIM = SSM_D_INNER + 2 * SSM_GROUPS * SSM_STATE
SC_WIDTH = D_MODEL // 2
SC_CONV = 3
D_FF = 4 * D_MODEL

LANES = 128
SUBLANES = 8
VMEM_LIMIT_BYTES = 60 * 1024 * 1024

XBC_OFF = 0
SC_OFF = XBC_OFF + SSM_CONV_DIM
GATE_OFF = SC_OFF + 3 * SC_WIDTH
Z_OFF = GATE_OFF + 3 * D_MODEL
QKV_OFF = Z_OFF + SSM_D_INNER
PROJ_WIDTH = QKV_OFF + 3 * ATTN_WIDTH
PROJ_TN = 1024

_O_Q = 0
_O_Z = 3 * ATTN_WIDTH
_O_XBC = _O_Z + SSM_D_INNER
_O_DT = _O_XBC + SSM_CONV_DIM
_O_SC = _O_DT + SSM_HEADS
_O_GATE = _O_SC + 3 * SC_WIDTH
_O_END = _O_GATE + 3 * D_MODEL

_NT = (((1,), (1,)), ((), ()))


def _params(*sem):
    return pltpu.CompilerParams(dimension_semantics=sem, vmem_limit_bytes=VMEM_LIMIT_BYTES)


def _rms(x, g):
    return x * lax.rsqrt(jnp.mean(x * x, axis=-1, keepdims=True) + NORM_EPS) * g


def _sigmoid(x):
    return 1.0 / (1.0 + jnp.exp(-x))


def _row_chunks(n, parts=4):
    step = n // parts
    return [slice(k * step, (k + 1) * step) for k in range(parts)]


_T = PROJ_TN
_N_XBC, _N_TAIL, _N_Z, _N_QKV = SSM_CONV_DIM // _T, (_O_END - _O_SC) // _T, SSM_D_INNER // _T, 3 * ATTN_WIDTH // _T
_J_TAIL, _J_Z, _J_QKV = _N_XBC, _N_XBC + _N_TAIL, _N_XBC + _N_TAIL + _N_Z
assert (_J_QKV + _N_QKV) * _T == PROJ_WIDTH
assert all(o % SUBLANES == 0 for o in (_O_XBC, _O_SC, _O_Z, _O_Q, _O_DT)) and _T % SUBLANES == 0


def _src_row(j):
    s, ts = SUBLANES, _T // SUBLANES
    groups = jnp.where(j < _J_TAIL, _O_XBC // s + ts * j,
                       jnp.where(j < _J_Z, _O_SC // s + ts * (j - _J_TAIL),
                                 jnp.where(j < _J_QKV, _O_Z // s + ts * (j - _J_Z), _O_Q // s + ts * (j - _J_QKV))))
    return groups * s


def _norm_body(x_ref, g_ref, o_ref):
    o_ref[...] = _rms(x_ref[...], g_ref[...]).astype(o_ref.dtype)


def pre_norm(x, g, layer, tm=512):
    t, d = x.shape
    tm = min(tm, t)
    return pl.pallas_call(
        _norm_body,
        grid=(t // tm,),
        in_specs=[pl.BlockSpec((tm, d), lambda i: (i, 0)),
                  pl.BlockSpec((None, 1, d), lambda i: (layer, 0, 0))],
        out_specs=pl.BlockSpec((tm, d), lambda i: (i, 0)),
        out_shape=jax.ShapeDtypeStruct((t, d), BF16),
        compiler_params=_params("parallel"),
        name="pre_norm",
    )(x, g)


def _inproj_body(h_ref, wt_ref, o_ref):
    o_ref[...] = lax.dot_general(h_ref[...], wt_ref[0].astype(BF16), _NT, preferred_element_type=F32)


def in_projection(h, w_t, layer, tm=2048):
    t, d = h.shape
    tn = _T
    tm = min(tm, t)
    return pl.pallas_call(
        _inproj_body,
        grid=(t // tm, PROJ_WIDTH // tn),
        in_specs=[
            pl.BlockSpec((tm, d), lambda i, j: (i, 0)),
            pl.BlockSpec((pl.Element(1), pl.Element(tn), pl.Element(d)), lambda i, j: (layer, _src_row(j), 0)),
        ],
        out_specs=pl.BlockSpec((tm, tn), lambda i, j: (i, j)),
        out_shape=jax.ShapeDtypeStruct((t, PROJ_WIDTH), F32),
        compiler_params=_params("parallel", "arbitrary"),
        name="in_projection",
    )(h, w_t)


def dt_projection(h, w_t, layer, tm=2048):
    t, d = h.shape
    tm = min(tm, t)
    return pl.pallas_call(
        _inproj_body,
        grid=(t // tm,),
        in_specs=[
            pl.BlockSpec((tm, d), lambda i: (i, 0)),
            pl.BlockSpec((pl.Element(1), pl.Element(LANES), pl.Element(d)), lambda i: (layer, _O_DT, 0)),
        ],
        out_specs=pl.BlockSpec((tm, LANES), lambda i: (i, 0)),
        out_shape=jax.ShapeDtypeStruct((t, LANES), F32),
        compiler_params=_params("parallel"),
        name="dt_projection",
    )(h, w_t)


def _bias_body(tbl_ref, o_ref):
    h = pl.program_id(0)
    blk = MOBA_BLOCK
    key = lax.broadcasted_iota(jnp.int32, (blk, blk), 0)
    qry = lax.broadcasted_iota(jnp.int32, (blk, blk), 1)
    max_exact = REL_BUCKETS // 2
    for t in range(3):
        n = jnp.maximum(t * blk + qry - key, 0)
        nf = jnp.maximum(n, 1).astype(F32)
        large = max_exact + (jnp.log(nf / max_exact) / math.log(REL_MAX_DISTANCE / max_exact)
                             * (REL_BUCKETS - max_exact)).astype(jnp.int32)
        large = jnp.minimum(large, REL_BUCKETS - 1)
        bucket = jnp.where(n < max_exact, n, large)
        acc = jnp.zeros((blk, blk), F32)
        for b in range(REL_BUCKETS):
            acc = jnp.where(bucket == b, tbl_ref[b, h], acc)
        o_ref[t] = acc


def bias_tiles(rel_bias):
    nh = rel_bias.shape[1]
    return pl.pallas_call(
        _bias_body,
        grid=(nh,),
        in_specs=[pl.BlockSpec(memory_space=pltpu.SMEM)],
        out_specs=pl.BlockSpec((None, 3, MOBA_BLOCK, MOBA_BLOCK), lambda h: (h, 0, 0, 0)),
        out_shape=jax.ShapeDtypeStruct((nh, 3, MOBA_BLOCK, MOBA_BLOCK), F32),
        compiler_params=_params("arbitrary"),
        name="bias_tiles",
    )(rel_bias)


def _attn_body(q_ref, k_ref, v_ref, bias_ref, o_ref, kmean_ref, *, nb):
    blk = MOBA_BLOCK
    scale = ATTN_HEAD_DIM ** -0.5

    k = k_ref[...]
    kb = k.astype(BF16)
    v_t = v_ref[...].T.astype(BF16)
    for j in range(nb):
        kmean_ref[j:j + 1, :] = jnp.mean(k[j * blk:(j + 1) * blk, :], axis=0, keepdims=True)
    q_all = q_ref[...]
    gate_all = lax.dot_general(kmean_ref[...], q_all, _NT, precision=lax.Precision.HIGHEST,
                               preferred_element_type=F32)
    qb_all = (q_all * (scale * LOG2E)).astype(BF16)

    key = lax.broadcasted_iota(jnp.int32, (blk, blk), 0)
    qry = lax.broadcasted_iota(jnp.int32, (blk, blk), 1)
    causal = key <= qry
    bias_own = bias_ref[0] * LOG2E
    bias_prev = bias_ref[1] * LOG2E
    bias_far = bias_ref[2, 0:1, 0:1] * LOG2E
    bidx = lax.broadcasted_iota(jnp.int32, (gate_all.shape[0], blk), 0)

    def scores(qi):
        return lax.dot_general(kb[0:(qi + 1) * blk, :], qb_all[qi * blk:(qi + 1) * blk, :], _NT,
                               preferred_element_type=F32)

    def finish(cols, nk, p_t, denom):
        o_t = jnp.dot(v_t[:, 0:nk], p_t, preferred_element_type=F32)
        o_ref[cols, :] = (o_t / denom).T.astype(o_ref.dtype)

    pending = None
    s_next = scores(0)
    for qi in range(nb):
        cols = slice(qi * blk, (qi + 1) * blk)
        nk = (qi + 1) * blk
        s = s_next
        if qi + 1 < nb:
            s_next = scores(qi + 1)

        keep = None
        if qi > MOBA_TOPK:
            gate = gate_all[:, cols]
            past = bidx < qi
            keep = []
            for j in range(qi):
                g_j = gate[j:j + 1, :]
                beats = ((gate > g_j) | ((gate == g_j) & (bidx < j))) & past
                keep.append(jnp.sum(beats.astype(F32), axis=0, keepdims=True) < MOBA_TOPK)

        pieces = []
        for j in range(qi + 1):
            s_j = s[j * blk:(j + 1) * blk, :]
            if j == qi:
                s_j = jnp.where(causal, s_j + bias_own, NEG_INF)
            elif j == qi - 1:
                s_j = s_j + bias_prev
                if keep is not None:
                    s_j = jnp.where(keep[j], s_j, NEG_INF)
            elif keep is not None:
                s_j = s_j + jnp.where(keep[j], bias_far, NEG_INF)
            else:
                s_j = s_j + bias_far
            pieces.append(s_j)

        m = pieces[0]
        for s_j in pieces[1:]:
            m = jnp.maximum(m, s_j)
        m = jnp.max(m, axis=0, keepdims=True)
        probs = [jnp.exp2(s_j - m) for s_j in pieces]
        total = probs[0]
        for p_j in probs[1:]:
            total = total + p_j
        denom = jnp.sum(total, axis=0, keepdims=True)
        p_t = jnp.concatenate([p_j.astype(BF16) for p_j in probs], axis=0)
        if pending is not None:
            finish(*pending)
        pending = (cols, nk, p_t, denom)
    finish(*pending)


def moba_attention(proj, bias, batch, seq, qkv_off):
    t = proj.shape[0]
    blk, dh, nh = MOBA_BLOCK, ATTN_HEAD_DIM, ATTN_HEADS
    nb = seq // blk
    qc = qkv_off // dh
    return pl.pallas_call(
        functools.partial(_attn_body, nb=nb),
        grid=(batch, nh),
        in_specs=[
            pl.BlockSpec((seq, dh), lambda b, h: (b, qc + h)),
            pl.BlockSpec((seq, dh), lambda b, h: (b, qc + nh + h)),
            pl.BlockSpec((seq, dh), lambda b, h: (b, qc + 2 * nh + h)),
            pl.BlockSpec((None, 3, blk, blk), lambda b, h: (h, 0, 0, 0)),
        ],
        out_specs=pl.BlockSpec((seq, dh), lambda b, h: (b, h)),
        out_shape=jax.ShapeDtypeStruct((t, nh * dh), BF16),
        scratch_shapes=[pltpu.VMEM((max(nb, SUBLANES), dh), F32)],
        compiler_params=_params("parallel", "parallel"),
        name="moba_attention",
    )(proj, proj, proj, bias)


def _hi_lo(v):
    hi = v.astype(BF16)
    lo = (v - hi.astype(F32)).astype(BF16)
    return jnp.concatenate([hi, lo], axis=1)


def _causal_taps(ext_ref, cs, w_ref, rows):
    width = w_ref.shape[0]
    cols = w_ref[:, cs].shape[1]
    shape = (rows // SUBLANES, SUBLANES, cols)
    prev = ext_ref[0:rows, cs].reshape(shape)
    cur = ext_ref[SUBLANES:SUBLANES + rows, cs].reshape(shape)
    sub = lax.broadcasted_iota(jnp.int32, shape, 1)
    acc = w_ref[width - 1:width, cs].reshape(1, 1, cols) * cur
    for j in range(1, width):
        shifted = pltpu.roll(jnp.where(sub >= SUBLANES - j, prev, cur), j, 1)
        acc = acc + w_ref[width - 1 - j:width - j, cs].reshape(1, 1, cols) * shifted
    return acc.reshape(rows, cols)


def _mixer_body(xbc_ref, sc_ref, z_ref, dt_ref, cw_ref, cb_ref, dtb_ref, alog_ref, dskip_ref, ng_ref,
                scw_ref, e_ref, yssm_ref, ysc_ref, ext_ref, ext2_ref, xc_ref, state_ref):
    c = pl.program_id(1)
    L = SSM_CHUNK
    sub = SSD_SUB
    halo = SUBLANES
    gw = SSM_D_INNER // SSM_GROUPS
    hpg = SSM_HEADS // SSM_GROUPS
    n_state = SSM_STATE

    @pl.when(c == 0)
    def _():
        ext_ref[0:halo, :] = jnp.zeros((halo, ext_ref.shape[1]), F32)
        ext2_ref[0:halo, :] = jnp.zeros((halo, ext2_ref.shape[1]), F32)
        state_ref[...] = jnp.zeros(state_ref.shape, F32)

    @pl.when(c > 0)
    def _():
        ext_ref[0:halo, :] = ext_ref[L:L + halo, :]
        ext2_ref[0:halo, :] = ext2_ref[L:L + halo, :]

    ext_ref[halo:halo + L, :] = xbc_ref[...]
    cchunk = 512
    for ci in range(SSM_CONV_DIM // cchunk):
        cs = slice(ci * cchunk, (ci + 1) * cchunk)
        v = _causal_taps(ext_ref, cs, cw_ref, L) + cb_ref[:, cs]
        xc_ref[:, cs] = v * _sigmoid(v)

    ext2_ref[halo:halo + L, :] = sc_ref[:, SC_WIDTH:2 * SC_WIDTH] * sc_ref[:, 2 * SC_WIDTH:3 * SC_WIDTH]
    for ci in range(SC_WIDTH // cchunk):
        cs = slice(ci * cchunk, (ci + 1) * cchunk)
        ysc_ref[:, cs] = (sc_ref[:, cs] * _causal_taps(ext2_ref, cs, scw_ref, L)).astype(ysc_ref.dtype)

    rows = lax.broadcasted_iota(jnp.int32, (L, LANES), 0)
    head_lane = lax.broadcasted_iota(jnp.int32, (L, LANES), 1) < SSM_HEADS
    x_dt = jnp.where(head_lane, dt_ref[...], 0.0) + dtb_ref[...]
    dt = jnp.maximum(x_dt, 0.0) + jnp.log1p(jnp.exp(-jnp.abs(x_dt)))
    a = dt * (-LOG2E * jnp.exp(alog_ref[...]))
    pos = rows % sub
    a_cum = a
    sh = 1
    while sh < sub:
        a_cum = a_cum + jnp.where(pos >= sh, pltpu.roll(a_cum, sh, axis=0), 0.0)
        sh *= 2
    a_cum_t = a_cum.T
    ea = jnp.exp2(a_cum)
    a_end = a_cum[L - 1:L, :]
    for s in range(L // sub - 2, -1, -1):
        a_end = jnp.where(rows < (s + 1) * sub, a_cum[(s + 1) * sub - 1:(s + 1) * sub, :], a_end)
    ds = jnp.exp2(a_end - a_cum)
    dt_hl, ea_hl, ds_hl = _hi_lo(dt), _hi_lo(ea), _hi_lo(ds)

    row = lax.broadcasted_iota(jnp.int32, (sub, sub), 0)
    col = lax.broadcasted_iota(jnp.int32, (sub, sub), 1)
    causal = row >= col
    low_half = lax.broadcasted_iota(jnp.int32, (sub, LANES), 1) < SSM_HEAD_DIM

    for g in range(SSM_GROUPS):
        cs = slice(g * gw, (g + 1) * gw)
        b_cols = slice(SSM_D_INNER + g * n_state, SSM_D_INNER + (g + 1) * n_state)
        c_cols = slice(SSM_D_INNER + (SSM_GROUPS + g) * n_state, SSM_D_INNER + (SSM_GROUPS + g + 1) * n_state)
        e_g = e_ref[:, cs]
        xs_g = xc_ref[:, cs]
        ea_e = jnp.dot(ea_hl, e_g, preferred_element_type=F32)
        xdt = xs_g * jnp.dot(dt_hl, e_g, preferred_element_type=F32)
        xdt_b = xdt.astype(BF16)
        xds_b = (xdt * jnp.dot(ds_hl, e_g, preferred_element_type=F32)).astype(BF16)

        st = state_ref[g]
        y_rows = []
        for s in range(L // sub):
            rs = slice(s * sub, (s + 1) * sub)
            b_s = xc_ref[rs, b_cols]
            c_sb = xc_ref[rs, c_cols].astype(BF16)
            cb = lax.dot_general(c_sb, b_s.astype(BF16), _NT, preferred_element_type=F32)
            y = jnp.dot(c_sb, st.astype(BF16), preferred_element_type=F32) * ea_e[rs, :]

            pieces = []
            for pe in range(hpg // 2):
                x_pair = xdt_b[rs, pe * LANES:(pe + 1) * LANES]
                acc = None
                for half in range(2):
                    h = g * hpg + 2 * pe + half
                    seg = a_cum[rs, h:h + 1] - a_cum_t[h:h + 1, rs]
                    decay = jnp.exp2(jnp.where(causal, seg, -jnp.inf))
                    m_h = (cb * decay).astype(BF16)
                    keep = low_half if half == 0 else jnp.logical_not(low_half)
                    x_h = jnp.where(keep, x_pair, jnp.zeros_like(x_pair))
                    term = jnp.dot(m_h, x_h, preferred_element_type=F32)
                    acc = term if acc is None else acc + term
                pieces.append(acc)
            y_rows.append(y + jnp.concatenate(pieces, axis=1))

            upd = jnp.dot(b_s.T.astype(BF16), xds_b[rs, :], preferred_element_type=F32)
            st = st * ea_e[(s + 1) * sub - 1:(s + 1) * sub, :] + upd
        state_ref[g] = st

        y = jnp.concatenate(y_rows, axis=0) + dskip_ref[:, cs] * xs_g
        z_g = z_ref[:, cs]
        y = y * (z_g * _sigmoid(z_g))
        y = y * lax.rsqrt(jnp.mean(y * y, axis=-1, keepdims=True) + NORM_EPS)
        yssm_ref[:, cs] = (y * ng_ref[:, cs]).astype(yssm_ref.dtype)


def _head_expand_matrix():
    e = np.zeros((2 * LANES, SSM_D_INNER), np.float32)
    for h in range(SSM_HEADS):
        e[h, h * SSM_HEAD_DIM:(h + 1) * SSM_HEAD_DIM] = 1.0
        e[LANES + h, h * SSM_HEAD_DIM:(h + 1) * SSM_HEAD_DIM] = 1.0
    return jnp.asarray(e, BF16)


def ssd_and_short_conv(proj, dt_raw, conv_w, conv_b, dt_bias, a_log, d_skip, norm_g, sc_w, layer, batch, seq):
    t = proj.shape[0]
    L = SSM_CHUNK
    nc = seq // L
    row_map = lambda b, c: (b * nc + c, 0)

    def col_block(off, width):
        return pl.BlockSpec((L, width), lambda b, c: (b * nc + c, off // width))

    def param(rows, cols):
        return pl.BlockSpec((None, rows, cols), lambda b, c: (layer, 0, 0))

    return pl.pallas_call(
        _mixer_body,
        grid=(batch, nc),
        in_specs=[
            col_block(XBC_OFF, SSM_CONV_DIM),
            col_block(SC_OFF, 3 * SC_WIDTH),
            col_block(Z_OFF, SSM_D_INNER),
            pl.BlockSpec((L, LANES), row_map),
            param(SSM_CONV, SSM_CONV_DIM),
            param(1, SSM_CONV_DIM),
            param(1, LANES),
            param(1, LANES),
            param(1, SSM_D_INNER),
            param(1, SSM_D_INNER),
            param(SC_CONV, SC_WIDTH),
            pl.BlockSpec((2 * LANES, SSM_D_INNER), lambda b, c: (0, 0)),
        ],
        out_specs=[
            pl.BlockSpec((L, SSM_D_INNER), row_map),
            pl.BlockSpec((L, SC_WIDTH), row_map),
        ],
        out_shape=[
            jax.ShapeDtypeStruct((t, SSM_D_INNER), BF16),
            jax.ShapeDtypeStruct((t, SC_WIDTH), BF16),
        ],
        scratch_shapes=[
            pltpu.VMEM((L + 2 * SUBLANES, SSM_CONV_DIM), F32),
            pltpu.VMEM((L + 2 * SUBLANES, SC_WIDTH), F32),
            pltpu.VMEM((L, SSM_CONV_DIM), F32),
            pltpu.VMEM((SSM_GROUPS, SSM_STATE, SSM_D_INNER // SSM_GROUPS), F32),
        ],
        compiler_params=_params("parallel", "arbitrary"),
        name="ssd_short_conv",
    )(proj, proj, proj, dt_raw, conv_w, conv_b, dt_bias, a_log, d_skip, norm_g, sc_w, _head_expand_matrix())


def _merge_body(ya_ref, ym_ref, yc_ref, ga_ref, gm_ref, gc_ref, wa_ref, wm_ref, wc_ref, o_ref):
    a = jnp.dot(ya_ref[...], wa_ref[...], preferred_element_type=F32)
    m = jnp.dot(ym_ref[...], wm_ref[...], preferred_element_type=F32)
    c = jnp.dot(yc_ref[...], wc_ref[...], preferred_element_type=F32)
    o = _sigmoid(ga_ref[...]) * a + _sigmoid(gm_ref[...]) * m + _sigmoid(gc_ref[...]) * c
    o_ref[...] = o.astype(o_ref.dtype)


def branch_merge(y_attn, y_ssm, y_sc, proj, w_a, w_m, w_c, layer, tm=1024, tn=512):
    t = y_attn.shape[0]
    d = w_a.shape[-1]
    tm = min(tm, t)
    gc = GATE_OFF // tn
    gs = d // tn

    def act(width):
        return pl.BlockSpec((tm, width), lambda i, j: (i, 0))

    def gate(k):
        return pl.BlockSpec((tm, tn), lambda i, j: (i, gc + k * gs + j))

    def weight(rows):
        return pl.BlockSpec((None, rows, tn), lambda i, j: (layer, 0, j))

    return pl.pallas_call(
        _merge_body,
        grid=(t // tm, d // tn),
        in_specs=[act(y_attn.shape[1]), act(y_ssm.shape[1]), act(y_sc.shape[1]),
                  gate(0), gate(1), gate(2),
                  weight(w_a.shape[1]), weight(w_m.shape[1]), weight(w_c.shape[1])],
        out_specs=pl.BlockSpec((tm, tn), lambda i, j: (i, j)),
        out_shape=jax.ShapeDtypeStruct((t, d), BF16),
        compiler_params=_params("parallel", "arbitrary"),
        name="branch_merge",
    )(y_attn, y_ssm, y_sc, proj, proj, proj, w_a, w_m, w_c)


def _outproj_body(m_ref, x_ref, w_ref, gpost_ref, gnext_ref, o_ref, h_ref, wb_ref):
    @pl.when(pl.program_id(0) == 0)
    def _():
        wb_ref[...] = w_ref[...].astype(BF16)

    for rows in _row_chunks(m_ref.shape[0]):
        o = jnp.dot(m_ref[rows, :], wb_ref[...], preferred_element_type=F32)
        x = x_ref[rows, :] + _rms(o, gpost_ref[...])
        o_ref[rows, :] = x
        h_ref[rows, :] = _rms(x, gnext_ref[...]).astype(h_ref.dtype)


def out_projection(merged, x, w, g_post, g_next, layer, tm=512):
    t, d = x.shape
    tm = min(tm, t)
    rows = pl.BlockSpec((tm, d), lambda i: (i, 0))
    gain = pl.BlockSpec((None, 1, d), lambda i: (layer, 0, 0))
    return pl.pallas_call(
        _outproj_body,
        grid=(t // tm,),
        in_specs=[rows, rows, pl.BlockSpec((None, d, d), lambda i: (layer, 0, 0), pipeline_mode=pl.Buffered(1)),
                  gain, gain],
        out_specs=[rows, rows],
        out_shape=[jax.ShapeDtypeStruct((t, d), F32), jax.ShapeDtypeStruct((t, d), BF16)],
        scratch_shapes=[pltpu.VMEM((d, d), BF16)],
        compiler_params=_params("arbitrary"),
        name="out_projection",
    )(merged, x, w, g_post, g_next)


def _mlp_body(h_ref, w1_ref, w2_ref, gpost_ref, o_ref):
    f = pl.program_id(1)

    @pl.when(f == 0)
    def _():
        o_ref[...] = jnp.zeros(o_ref.shape, F32)

    u = jnp.maximum(jnp.dot(h_ref[...], w1_ref[...].astype(BF16), preferred_element_type=F32), 0.0)
    o_ref[...] += jnp.dot((u * u).astype(BF16), w2_ref[...].astype(BF16), preferred_element_type=F32)

    @pl.when(f == pl.num_programs(1) - 1)
    def _():
        o_ref[...] = _rms(o_ref[...], gpost_ref[...])


def mlp(h, w1, w2, g_post, layer, tm=1024, tf=512):
    t, d = h.shape
    ff = w1.shape[-1]
    tm = min(tm, t)
    return pl.pallas_call(
        _mlp_body,
        grid=(t // tm, ff // tf),
        in_specs=[
            pl.BlockSpec((tm, d), lambda i, f: (i, 0)),
            pl.BlockSpec((None, d, tf), lambda i, f: (layer, 0, f)),
            pl.BlockSpec((None, tf, d), lambda i, f: (layer, f, 0)),
            pl.BlockSpec((None, 1, d), lambda i, f: (layer, 0, 0)),
        ],
        out_specs=pl.BlockSpec((tm, d), lambda i, f: (i, 0)),
        out_shape=jax.ShapeDtypeStruct((t, d), F32),
        compiler_params=_params("parallel", "arbitrary"),
        name="mlp",
    )(h, w1, w2, g_post)


def _ple_body(x_ref, dx_ref, p_ref, gpre_ref, wg_ref, wp_ref, gpost_ref, *rest):
    wpb_ref = rest[-1]

    @pl.when(pl.program_id(0) == 0)
    def _():
        wpb_ref[...] = wp_ref[...].astype(BF16)

    for rows in _row_chunks(x_ref.shape[0]):
        x = x_ref[rows, :] + dx_ref[rows, :]
        h = _rms(x, gpre_ref[...]).astype(BF16)
        gate = _sigmoid(jnp.dot(h, wg_ref[...], preferred_element_type=F32))
        emb = jnp.dot(p_ref[rows, :].astype(BF16), wpb_ref[...], preferred_element_type=F32)
        x = x + _rms(gate * emb, gpost_ref[...])
        if len(rest) == 2:
            o_ref = rest[0]
        else:
            gnext_ref, o_ref, h_ref = rest[:3]
            h_ref[rows, :] = _rms(x, gnext_ref[...]).astype(h_ref.dtype)
        o_ref[rows, :] = x


def per_layer_embedding(x, dx, p, g_pre, w_gate, w_proj, g_post, layer, g_next=None, tm=512):
    t, d = x.shape
    pd = p.shape[-1]
    tm = min(tm, t)
    rows = pl.BlockSpec((tm, d), lambda i: (i, 0))
    gain = pl.BlockSpec((None, 1, d), lambda i: (layer, 0, 0))
    in_specs = [
        rows,
        rows,
        pl.BlockSpec((None, tm, pd), lambda i: (layer, i, 0)),
        gain,
        pl.BlockSpec((None, d, d), lambda i: (layer, 0, 0), pipeline_mode=pl.Buffered(1)),
        pl.BlockSpec((None, pd, d), lambda i: (layer, 0, 0), pipeline_mode=pl.Buffered(1)),
        gain,
    ]
    args = [x, dx, p, g_pre, w_gate, w_proj, g_post]
    out_specs = [rows]
    out_shape = [jax.ShapeDtypeStruct((t, d), F32)]
    if g_next is not None:
        in_specs.append(pl.BlockSpec((None, 1, d), lambda i: (layer + 1, 0, 0)))
        args.append(g_next)
        out_specs.append(rows)
        out_shape.append(jax.ShapeDtypeStruct((t, d), BF16))
    return pl.pallas_call(
        _ple_body,
        grid=(t // tm,),
        in_specs=in_specs,
        out_specs=out_specs,
        out_shape=out_shape,
        scratch_shapes=[pltpu.VMEM((pd, d), BF16)],
        compiler_params=_params("arbitrary"),
        name="per_layer_embedding",
    )(*args)


def _row(v):
    return v[:, None, :]


def _lane_pad(v):
    return jnp.pad(v, ((0, 0), (0, LANES - v.shape[-1])))[:, None, :]


def kernel(x, p, rel_bias, g_mix_pre, w_in, ssm_conv_w, ssm_conv_b, ssm_dt_bias, ssm_a_log, ssm_d, ssm_norm_g, sc_conv_w, w_br_attn, w_br_ssm, w_br_conv, w_out, g_mix_post, g_mlp_pre, w_mlp_up, w_mlp_down, g_mlp_post, g_ple_pre, w_ple_gate, w_ple_proj, g_ple_post):
    batch, seq, d = x.shape
    depth = w_in.shape[0]
    t = batch * seq

    w_in_t = jnp.swapaxes(w_in, 1, 2)
    w_a, w_m, w_c = w_br_attn.astype(BF16), w_br_ssm.astype(BF16), w_br_conv.astype(BF16)
    w_pg = w_ple_gate.astype(BF16)
    d_skip = jnp.repeat(ssm_d, SSM_HEAD_DIM, axis=-1)[:, None, :]
    dt_bias, a_log = _lane_pad(ssm_dt_bias), _lane_pad(ssm_a_log)
    conv_b, norm_g = _row(ssm_conv_b), _row(ssm_norm_g)
    g_mix_pre, g_mix_post = _row(g_mix_pre), _row(g_mix_post)
    g_mlp_pre, g_mlp_post = _row(g_mlp_pre), _row(g_mlp_post)
    g_ple_pre, g_ple_post = _row(g_ple_pre), _row(g_ple_post)

    bias = bias_tiles(rel_bias)
    xf = x.reshape(t, d)
    pf = p.reshape(depth, t, p.shape[-1])
    h = pre_norm(xf, g_mix_pre, 0)
    for i in range(depth):
        proj = in_projection(h, w_in_t, i)
        y_attn = moba_attention(proj, bias, batch, seq, QKV_OFF)
        dt_raw = dt_projection(h, w_in_t, i)
        y_ssm, y_sc = ssd_and_short_conv(proj, dt_raw, ssm_conv_w, conv_b, dt_bias, a_log, d_skip, norm_g,
                                         sc_conv_w, i, batch, seq)
        merged = branch_merge(y_attn, y_ssm, y_sc, proj, w_a, w_m, w_c, i)
        xf, h_mlp = out_projection(merged, xf, w_out, g_mix_post, g_mlp_pre, i)
        dx = mlp(h_mlp, w_mlp_up, w_mlp_down, g_mlp_post, i)
        if i + 1 < depth:
            xf, h = per_layer_embedding(xf, dx, pf, g_ple_pre, w_pg, w_ple_proj, g_ple_post, i, g_next=g_mix_pre)
        else:
            (xf,) = per_layer_embedding(xf, dx, pf, g_ple_pre, w_pg, w_ple_proj, g_ple_post, i)
    return xf.reshape(batch, seq, d)
```

```python
import functools
import math

import numpy as np
import jax
import jax.numpy as jnp
from jax import lax
from jax.experimental import pallas as pl
from jax.experimental.pallas import tpu as pltpu

F32 = jnp.float32
BF16 = jnp.bfloat16

NORM_EPS = 1e-6
NEG_INF = -1e30
LOG2E = math.log2(math.e)

D_MODEL = 2048
PLE_DIM = 256
ATTN_HEADS = 8
ATTN_HEAD_DIM = 128
ATTN_WIDTH = ATTN_HEADS * ATTN_HEAD_DIM
MOBA_BLOCK = 256
MOBA_TOPK = 3
REL_BUCKETS = 32
REL_MAX_DISTANCE = 128
SSM_D_INNER = D_MODEL
SSM_HEAD_DIM = 64
SSM_HEADS = SSM_D_INNER // SSM_HEAD_DIM
SSM_GROUPS = 4
SSM_STATE = 128
SSM_CONV = 4
SSM_CHUNK = 256
SSD_SUB = 128
SSM_CONV_DIM = SSM_D_INNER + 2 * SSM_GROUPS * SSM_STATE
SC_WIDTH = D_MODEL // 2
SC_CONV = 3
D_FF = 4 * D_MODEL

LANES = 128
SUBLANES = 8
VMEM_LIMIT_BYTES = 60 * 1024 * 1024

XBC_OFF = 0
SC_OFF = XBC_OFF + SSM_CONV_DIM
GATE_OFF = SC_OFF + 3 * SC_WIDTH
Z_OFF = GATE_OFF + 3 * D_MODEL
QKV_OFF = Z_OFF + SSM_D_INNER
PROJ_WIDTH = QKV_OFF + 3 * ATTN_WIDTH
PROJ_TN = 1024

_O_Q = 0
_O_Z = 3 * ATTN_WIDTH
_O_XBC = _O_Z + SSM_D_INNER
_O_DT = _O_XBC + SSM_CONV_DIM
_O_SC = _O_DT + SSM_HEADS
_O_GATE = _O_SC + 3 * SC_WIDTH
_O_END = _O_GATE + 3 * D_MODEL

_NT = (((1,), (1,)), ((), ()))


def _params(*sem):
    return pltpu.CompilerParams(dimension_semantics=sem, vmem_limit_bytes=VMEM_LIMIT_BYTES)


def _rms(x, g):
    return x * lax.rsqrt(jnp.mean(x * x, axis=-1, keepdims=True) + NORM_EPS) * g


def _sigmoid(x):
    return 1.0 / (1.0 + jnp.exp(-x))


def _row_chunks(n, parts=4):
    step = n // parts
    return [slice(k * step, (k + 1) * step) for k in range(parts)]


_T = PROJ_TN
_N_XBC, _N_TAIL, _N_Z, _N_QKV = SSM_CONV_DIM // _T, (_O_END - _O_SC) // _T, SSM_D_INNER // _T, 3 * ATTN_WIDTH // _T
_J_TAIL, _J_Z, _J_QKV = _N_XBC, _N_XBC + _N_TAIL, _N_XBC + _N_TAIL + _N_Z
assert (_J_QKV + _N_QKV) * _T == PROJ_WIDTH
assert all(o % SUBLANES == 0 for o in (_O_XBC, _O_SC, _O_Z, _O_Q, _O_DT)) and _T % SUBLANES == 0


def _src_row(j):
    s, ts = SUBLANES, _T // SUBLANES
    groups = jnp.where(j < _J_TAIL, _O_XBC // s + ts * j,
                       jnp.where(j < _J_Z, _O_SC // s + ts * (j - _J_TAIL),
                                 jnp.where(j < _J_QKV, _O_Z // s + ts * (j - _J_Z), _O_Q // s + ts * (j - _J_QKV))))
    return groups * s


def _norm_body(x_ref, g_ref, o_ref):
    o_ref[...] = _rms(x_ref[...], g_ref[...]).astype(o_ref.dtype)


def pre_norm(x, g, layer, tm=512):
    t, d = x.shape
    tm = min(tm, t)
    return pl.pallas_call(
        _norm_body,
        grid=(t // tm,),
        in_specs=[pl.BlockSpec((tm, d), lambda i: (i, 0)),
                  pl.BlockSpec((None, 1, d), lambda i: (layer, 0, 0))],
        out_specs=pl.BlockSpec((tm, d), lambda i: (i, 0)),
        out_shape=jax.ShapeDtypeStruct((t, d), BF16),
        compiler_params=_params("parallel"),
        name="pre_norm",
    )(x, g)


def _inproj_body(h_ref, wt_ref, o_ref):
    o_ref[...] = lax.dot_general(h_ref[...], wt_ref[0].astype(BF16), _NT, preferred_element_type=F32)


def in_projection(h, w_t, layer, tm=2048):
    t, d = h.shape
    tn = _T
    tm = min(tm, t)
    return pl.pallas_call(
        _inproj_body,
        grid=(t // tm, PROJ_WIDTH // tn),
        in_specs=[
            pl.BlockSpec((tm, d), lambda i, j: (i, 0)),
            pl.BlockSpec((pl.Element(1), pl.Element(tn), pl.Element(d)), lambda i, j: (layer, _src_row(j), 0)),
        ],
        out_specs=pl.BlockSpec((tm, tn), lambda i, j: (i, j)),
        out_shape=jax.ShapeDtypeStruct((t, PROJ_WIDTH), F32),
        compiler_params=_params("parallel", "arbitrary"),
        name="in_projection",
    )(h, w_t)


def dt_projection(h, w_t, layer, tm=2048):
    t, d = h.shape
    tm = min(tm, t)
    return pl.pallas_call(
        _inproj_body,
        grid=(t // tm,),
        in_specs=[
            pl.BlockSpec((tm, d), lambda i: (i, 0)),
            pl.BlockSpec((pl.Element(1), pl.Element(LANES), pl.Element(d)), lambda i: (layer, _O_DT, 0)),
        ],
        out_specs=pl.BlockSpec((tm, LANES), lambda i: (i, 0)),
        out_shape=jax.ShapeDtypeStruct((t, LANES), F32),
        compiler_params=_params("parallel"),
        name="dt_projection",
    )(h, w_t)


def _bias_body(tbl_ref, o_ref):
    h = pl.program_id(0)
    blk = MOBA_BLOCK
    key = lax.broadcasted_iota(jnp.int32, (blk, blk), 0)
    qry = lax.broadcasted_iota(jnp.int32, (blk, blk), 1)
    max_exact = REL_BUCKETS // 2
    for t in range(3):
        n = jnp.maximum(t * blk + qry - key, 0)
        nf = jnp.maximum(n, 1).astype(F32)
        large = max_exact + (jnp.log(nf / max_exact) / math.log(REL_MAX_DISTANCE / max_exact)
                             * (REL_BUCKETS - max_exact)).astype(jnp.int32)
        large = jnp.minimum(large, REL_BUCKETS - 1)
        bucket = jnp.where(n < max_exact, n, large)
        acc = jnp.zeros((blk, blk), F32)
        for b in range(REL_BUCKETS):
            acc = jnp.where(bucket == b, tbl_ref[b, h], acc)
        o_ref[t] = acc


def bias_tiles(rel_bias):
    nh = rel_bias.shape[1]
    return pl.pallas_call(
        _bias_body,
        grid=(nh,),
        in_specs=[pl.BlockSpec(memory_space=pltpu.SMEM)],
        out_specs=pl.BlockSpec((None, 3, MOBA_BLOCK, MOBA_BLOCK), lambda h: (h, 0, 0, 0)),
        out_shape=jax.ShapeDtypeStruct((nh, 3, MOBA_BLOCK, MOBA_BLOCK), F32),
        compiler_params=_params("arbitrary"),
        name="bias_tiles",
    )(rel_bias)


def _attn_body(q_ref, k_ref, v_ref, bias_ref, o_ref, kmean_ref, *, nb):
    blk = MOBA_BLOCK
    scale = ATTN_HEAD_DIM ** -0.5

    k = k_ref[...]
    kb = k.astype(BF16)
    v_t = v_ref[...].T.astype(BF16)
    for j in range(nb):
        kmean_ref[j:j + 1, :] = jnp.mean(k[j * blk:(j + 1) * blk, :], axis=0, keepdims=True)
    q_all = q_ref[...]
    gate_all = lax.dot_general(kmean_ref[...], q_all, _NT, precision=lax.Precision.HIGHEST,
                               preferred_element_type=F32)
    qb_all = (q_all * (scale * LOG2E)).astype(BF16)

    key = lax.broadcasted_iota(jnp.int32, (blk, blk), 0)
    qry = lax.broadcasted_iota(jnp.int32, (blk, blk), 1)
    causal = key <= qry
    bias_own = bias_ref[0] * LOG2E
    bias_prev = bias_ref[1] * LOG2E
    bias_far = bias_ref[2, 0:1, 0:1] * LOG2E
    bidx = lax.broadcasted_iota(jnp.int32, (gate_all.shape[0], blk), 0)

    def scores(qi):
        return lax.dot_general(kb[0:(qi + 1) * blk, :], qb_all[qi * blk:(qi + 1) * blk, :], _NT,
                               preferred_element_type=F32)

    def finish(cols, nk, p_t, denom):
        o_t = jnp.dot(v_t[:, 0:nk], p_t, preferred_element_type=F32)
        o_ref[cols, :] = (o_t / denom).T.astype(o_ref.dtype)

    pending = None
    s_next = scores(0)
    for qi in range(nb):
        cols = slice(qi * blk, (qi + 1) * blk)
        nk = (qi + 1) * blk
        s = s_next
        if qi + 1 < nb:
            s_next = scores(qi + 1)

        keep = None
        if qi > MOBA_TOPK:
            gate = gate_all[:, cols]
            past = bidx < qi
            keep = []
            for j in range(qi):
                g_j = gate[j:j + 1, :]
                beats = ((gate > g_j) | ((gate == g_j) & (bidx < j))) & past
                keep.append(jnp.sum(beats.astype(F32), axis=0, keepdims=True) < MOBA_TOPK)

        pieces = []
        for j in range(qi + 1):
            s_j = s[j * blk:(j + 1) * blk, :]
            if j == qi:
                s_j = jnp.where(causal, s_j + bias_own, NEG_INF)
            elif j == qi - 1:
                s_j = s_j + bias_prev
                if keep is not None:
                    s_j = jnp.where(keep[j], s_j, NEG_INF)
            elif keep is not None:
                s_j = s_j + jnp.where(keep[j], bias_far, NEG_INF)
            else:
                s_j = s_j + bias_far
            pieces.append(s_j)

        m = pieces[0]
        for s_j in pieces[1:]:
            m = jnp.maximum(m, s_j)
        m = jnp.max(m, axis=0, keepdims=True)
        probs = [jnp.exp2(s_j - m) for s_j in pieces]
        total = probs[0]
        for p_j in probs[1:]:
            total = total + p_j
        denom = jnp.sum(total, axis=0, keepdims=True)
        p_t = jnp.concatenate([p_j.astype(BF16) for p_j in probs], axis=0)
        if pending is not None:
            finish(*pending)
        pending = (cols, nk, p_t, denom)
    finish(*pending)


def moba_attention(proj, bias, batch, seq, qkv_off):
    t = proj.shape[0]
    blk, dh, nh = MOBA_BLOCK, ATTN_HEAD_DIM, ATTN_HEADS
    nb = seq // blk
    qc = qkv_off // dh
    return pl.pallas_call(
        functools.partial(_attn_body, nb=nb),
        grid=(batch, nh),
        in_specs=[
            pl.BlockSpec((seq, dh), lambda b, h: (b, qc + h)),
            pl.BlockSpec((seq, dh), lambda b, h: (b, qc + nh + h)),
            pl.BlockSpec((seq, dh), lambda b, h: (b, qc + 2 * nh + h)),
            pl.BlockSpec((None, 3, blk, blk), lambda b, h: (h, 0, 0, 0)),
        ],
        out_specs=pl.BlockSpec((seq, dh), lambda b, h: (b, h)),
        out_shape=jax.ShapeDtypeStruct((t, nh * dh), BF16),
        scratch_shapes=[pltpu.VMEM((max(nb, SUBLANES), dh), F32)],
        compiler_params=_params("parallel", "parallel"),
        name="moba_attention",
    )(proj, proj, proj, bias)


def _hi_lo(v):
    hi = v.astype(BF16)
    lo = (v - hi.astype(F32)).astype(BF16)
    return jnp.concatenate([hi, lo], axis=1)


def _causal_taps(ext_ref, cs, w_ref, rows):
    width = w_ref.shape[0]
    cols = w_ref[:, cs].shape[1]
    shape = (rows // SUBLANES, SUBLANES, cols)
    prev = ext_ref[0:rows, cs].reshape(shape)
    cur = ext_ref[SUBLANES:SUBLANES + rows, cs].reshape(shape)
    sub = lax.broadcasted_iota(jnp.int32, shape, 1)
    acc = w_ref[width - 1:width, cs].reshape(1, 1, cols) * cur
    for j in range(1, width):
        shifted = pltpu.roll(jnp.where(sub >= SUBLANES - j, prev, cur), j, 1)
        acc = acc + w_ref[width - 1 - j:width - j, cs].reshape(1, 1, cols) * shifted
    return acc.reshape(rows, cols)


def _mixer_body(xbc_ref, sc_ref, z_ref, dt_ref, cw_ref, cb_ref, dtb_ref, alog_ref, dskip_ref, ng_ref,
                scw_ref, e_ref, yssm_ref, ysc_ref, ext_ref, ext2_ref, xc_ref, state_ref):
    c = pl.program_id(1)
    L = SSM_CHUNK
    sub = SSD_SUB
    halo = SUBLANES
    gw = SSM_D_INNER // SSM_GROUPS
    hpg = SSM_HEADS // SSM_GROUPS
    n_state = SSM_STATE

    @pl.when(c == 0)
    def _():
        ext_ref[0:halo, :] = jnp.zeros((halo, ext_ref.shape[1]), F32)
        ext2_ref[0:halo, :] = jnp.zeros((halo, ext2_ref.shape[1]), F32)
        state_ref[...] = jnp.zeros(state_ref.shape, F32)

    @pl.when(c > 0)
    def _():
        ext_ref[0:halo, :] = ext_ref[L:L + halo, :]
        ext2_ref[0:halo, :] = ext2_ref[L:L + halo, :]

    ext_ref[halo:halo + L, :] = xbc_ref[...]
    cchunk = 512
    for ci in range(SSM_CONV_DIM // cchunk):
        cs = slice(ci * cchunk, (ci + 1) * cchunk)
        v = _causal_taps(ext_ref, cs, cw_ref, L) + cb_ref[:, cs]
        xc_ref[:, cs] = v * _sigmoid(v)

    ext2_ref[halo:halo + L, :] = sc_ref[:, SC_WIDTH:2 * SC_WIDTH] * sc_ref[:, 2 * SC_WIDTH:3 * SC_WIDTH]
    for ci in range(SC_WIDTH // cchunk):
        cs = slice(ci * cchunk, (ci + 1) * cchunk)
        ysc_ref[:, cs] = (sc_ref[:, cs] * _causal_taps(ext2_ref, cs, scw_ref, L)).astype(ysc_ref.dtype)

    rows = lax.broadcasted_iota(jnp.int32, (L, LANES), 0)
    head_lane = lax.broadcasted_iota(jnp.int32, (L, LANES), 1) < SSM_HEADS
    x_dt = jnp.where(head_lane, dt_ref[...], 0.0) + dtb_ref[...]
    dt = jnp.maximum(x_dt, 0.0) + jnp.log1p(jnp.exp(-jnp.abs(x_dt)))
    a = dt * (-LOG2E * jnp.exp(alog_ref[...]))
    pos = rows % sub
    a_cum = a
    sh = 1
    while sh < sub:
        a_cum = a_cum + jnp.where(pos >= sh, pltpu.roll(a_cum, sh, axis=0), 0.0)
        sh *= 2
    a_cum_t = a_cum.T
    ea = jnp.exp2(a_cum)
    a_end = a_cum[L - 1:L, :]
    for s in range(L // sub - 2, -1, -1):
        a_end = jnp.where(rows < (s + 1) * sub, a_cum[(s + 1) * sub - 1:(s + 1) * sub, :], a_end)
    ds = jnp.exp2(a_end - a_cum)
    dt_hl, ea_hl, ds_hl = _hi_lo(dt), _hi_lo(ea), _hi_lo(ds)

    row = lax.broadcasted_iota(jnp.int32, (sub, sub), 0)
    col = lax.broadcasted_iota(jnp.int32, (sub, sub), 1)
    causal = row >= col
    low_half = lax.broadcasted_iota(jnp.int32, (sub, LANES), 1) < SSM_HEAD_DIM

    for g in range(SSM_GROUPS):
        cs = slice(g * gw, (g + 1) * gw)
        b_cols = slice(SSM_D_INNER + g * n_state, SSM_D_INNER + (g + 1) * n_state)
        c_cols = slice(SSM_D_INNER + (SSM_GROUPS + g) * n_state, SSM_D_INNER + (SSM_GROUPS + g + 1) * n_state)
        e_g = e_ref[:, cs]
        xs_g = xc_ref[:, cs]
        ea_e = jnp.dot(ea_hl, e_g, preferred_element_type=F32)
        xdt = xs_g * jnp.dot(dt_hl, e_g, preferred_element_type=F32)
        xdt_b = xdt.astype(BF16)
        xds_b = (xdt * jnp.dot(ds_hl, e_g, preferred_element_type=F32)).astype(BF16)

        st = state_ref[g]
        y_rows = []
        for s in range(L // sub):
            rs = slice(s * sub, (s + 1) * sub)
            b_s = xc_ref[rs, b_cols]
            c_sb = xc_ref[rs, c_cols].astype(BF16)
            cb = lax.dot_general(c_sb, b_s.astype(BF16), _NT, preferred_element_type=F32)
            y = jnp.dot(c_sb, st.astype(BF16), preferred_element_type=F32) * ea_e[rs, :]

            pieces = []
            for pe in range(hpg // 2):
                x_pair = xdt_b[rs, pe * LANES:(pe + 1) * LANES]
                acc = None
                for half in range(2):
                    h = g * hpg + 2 * pe + half
                    seg = a_cum[rs, h:h + 1] - a_cum_t[h:h + 1, rs]
                    decay = jnp.exp2(jnp.where(causal, seg, -jnp.inf))
                    m_h = (cb * decay).astype(BF16)
                    keep = low_half if half == 0 else jnp.logical_not(low_half)
                    x_h = jnp.where(keep, x_pair, jnp.zeros_like(x_pair))
                    term = jnp.dot(m_h, x_h, preferred_element_type=F32)
                    acc = term if acc is None else acc + term
                pieces.append(acc)
            y_rows.append(y + jnp.concatenate(pieces, axis=1))

            upd = jnp.dot(b_s.T.astype(BF16), xds_b[rs, :], preferred_element_type=F32)
            st = st * ea_e[(s + 1) * sub - 1:(s + 1) * sub, :] + upd
        state_ref[g] = st

        y = jnp.concatenate(y_rows, axis=0) + dskip_ref[:, cs] * xs_g
        z_g = z_ref[:, cs]
        y = y * (z_g * _sigmoid(z_g))
        y = y * lax.rsqrt(jnp.mean(y * y, axis=-1, keepdims=True) + NORM_EPS)
        yssm_ref[:, cs] = (y * ng_ref[:, cs]).astype(yssm_ref.dtype)


def _head_expand_matrix():
    e = np.zeros((2 * LANES, SSM_D_INNER), np.float32)
    for h in range(SSM_HEADS):
        e[h, h * SSM_HEAD_DIM:(h + 1) * SSM_HEAD_DIM] = 1.0
        e[LANES + h, h * SSM_HEAD_DIM:(h + 1) * SSM_HEAD_DIM] = 1.0
    return jnp.asarray(e, BF16)


def ssd_and_short_conv(proj, dt_raw, conv_w, conv_b, dt_bias, a_log, d_skip, norm_g, sc_w, layer, batch, seq):
    t = proj.shape[0]
    L = SSM_CHUNK
    nc = seq // L
    row_map = lambda b, c: (b * nc + c, 0)

    def col_block(off, width):
        return pl.BlockSpec((L, width), lambda b, c: (b * nc + c, off // width))

    def param(rows, cols):
        return pl.BlockSpec((None, rows, cols), lambda b, c: (layer, 0, 0))

    return pl.pallas_call(
        _mixer_body,
        grid=(batch, nc),
        in_specs=[
            col_block(XBC_OFF, SSM_CONV_DIM),
            col_block(SC_OFF, 3 * SC_WIDTH),
            col_block(Z_OFF, SSM_D_INNER),
            pl.BlockSpec((L, LANES), row_map),
            param(SSM_CONV, SSM_CONV_DIM),
            param(1, SSM_CONV_DIM),
            param(1, LANES),
            param(1, LANES),
            param(1, SSM_D_INNER),
            param(1, SSM_D_INNER),
            param(SC_CONV, SC_WIDTH),
            pl.BlockSpec((2 * LANES, SSM_D_INNER), lambda b, c: (0, 0)),
        ],
        out_specs=[
            pl.BlockSpec((L, SSM_D_INNER), row_map),
            pl.BlockSpec((L, SC_WIDTH), row_map),
        ],
        out_shape=[
            jax.ShapeDtypeStruct((t, SSM_D_INNER), BF16),
            jax.ShapeDtypeStruct((t, SC_WIDTH), BF16),
        ],
        scratch_shapes=[
            pltpu.VMEM((L + 2 * SUBLANES, SSM_CONV_DIM), F32),
            pltpu.VMEM((L + 2 * SUBLANES, SC_WIDTH), F32),
            pltpu.VMEM((L, SSM_CONV_DIM), F32),
            pltpu.VMEM((SSM_GROUPS, SSM_STATE, SSM_D_INNER // SSM_GROUPS), F32),
        ],
        compiler_params=_params("parallel", "arbitrary"),
        name="ssd_short_conv",
    )(proj, proj, proj, dt_raw, conv_w, conv_b, dt_bias, a_log, d_skip, norm_g, sc_w, _head_expand_matrix())


def _merge_body(ya_ref, ym_ref, yc_ref, ga_ref, gm_ref, gc_ref, wa_ref, wm_ref, wc_ref, o_ref):
    tn = o_ref.shape[1]
    cols = pl.ds(pl.multiple_of(pl.program_id(1) * tn, tn), tn)
    a = jnp.dot(ya_ref[...], wa_ref[:, cols], preferred_element_type=F32)
    m = jnp.dot(ym_ref[...], wm_ref[:, cols], preferred_element_type=F32)
    c = jnp.dot(yc_ref[...], wc_ref[:, cols], preferred_element_type=F32)
    o = _sigmoid(ga_ref[...]) * a + _sigmoid(gm_ref[...]) * m + _sigmoid(gc_ref[...]) * c
    o_ref[...] = o.astype(o_ref.dtype)


def branch_merge(y_attn, y_ssm, y_sc, proj, w_a, w_m, w_c, layer, tm=1024, tn=512):
    t = y_attn.shape[0]
    d = w_a.shape[-1]
    tm = min(tm, t)
    gc = GATE_OFF // tn
    gs = d // tn

    def act(width):
        return pl.BlockSpec((tm, width), lambda i, j: (i, 0))

    def gate(k):
        return pl.BlockSpec((tm, tn), lambda i, j: (i, gc + k * gs + j))

    def weight(rows):
        return pl.BlockSpec((None, rows, d), lambda i, j: (layer, 0, 0), pipeline_mode=pl.Buffered(1))

    return pl.pallas_call(
        _merge_body,
        grid=(t // tm, d // tn),
        in_specs=[act(y_attn.shape[1]), act(y_ssm.shape[1]), act(y_sc.shape[1]),
                  gate(0), gate(1), gate(2),
                  weight(w_a.shape[1]), weight(w_m.shape[1]), weight(w_c.shape[1])],
        out_specs=pl.BlockSpec((tm, tn), lambda i, j: (i, j)),
        out_shape=jax.ShapeDtypeStruct((t, d), BF16),
        compiler_params=_params("parallel", "arbitrary"),
        name="branch_merge",
    )(y_attn, y_ssm, y_sc, proj, proj, proj, w_a, w_m, w_c)


def _outproj_body(m_ref, x_ref, w_ref, gpost_ref, gnext_ref, o_ref, h_ref, wb_ref):
    @pl.when(pl.program_id(0) == 0)
    def _():
        wb_ref[...] = w_ref[...].astype(BF16)

    for rows in _row_chunks(m_ref.shape[0]):
        o = jnp.dot(m_ref[rows, :], wb_ref[...], preferred_element_type=F32)
        x = x_ref[rows, :] + _rms(o, gpost_ref[...])
        o_ref[rows, :] = x
        h_ref[rows, :] = _rms(x, gnext_ref[...]).astype(h_ref.dtype)


def out_projection(merged, x, w, g_post, g_next, layer, tm=512):
    t, d = x.shape
    tm = min(tm, t)
    rows = pl.BlockSpec((tm, d), lambda i: (i, 0))
    gain = pl.BlockSpec((None, 1, d), lambda i: (layer, 0, 0))
    return pl.pallas_call(
        _outproj_body,
        grid=(t // tm,),
        in_specs=[rows, rows, pl.BlockSpec((None, d, d), lambda i: (layer, 0, 0), pipeline_mode=pl.Buffered(1)),
                  gain, gain],
        out_specs=[rows, rows],
        out_shape=[jax.ShapeDtypeStruct((t, d), F32), jax.ShapeDtypeStruct((t, d), BF16)],
        scratch_shapes=[pltpu.VMEM((d, d), BF16)],
        compiler_params=_params("arbitrary"),
        name="out_projection",
    )(merged, x, w, g_post, g_next)


def _mlp_body(h_ref, w1_ref, w2_ref, gpost_ref, o_ref):
    f = pl.program_id(1)

    @pl.when(f == 0)
    def _():
        o_ref[...] = jnp.zeros(o_ref.shape, F32)

    u = jnp.maximum(jnp.dot(h_ref[...], w1_ref[...].astype(BF16), preferred_element_type=F32), 0.0)
    o_ref[...] += jnp.dot((u * u).astype(BF16), w2_ref[...].astype(BF16), preferred_element_type=F32)

    @pl.when(f == pl.num_programs(1) - 1)
    def _():
        o_ref[...] = _rms(o_ref[...], gpost_ref[...])


def mlp(h, w1, w2, g_post, layer, tm=1024, tf=512):
    t, d = h.shape
    ff = w1.shape[-1]
    tm = min(tm, t)
    return pl.pallas_call(
        _mlp_body,
        grid=(t // tm, ff // tf),
        in_specs=[
            pl.BlockSpec((tm, d), lambda i, f: (i, 0)),
            pl.BlockSpec((None, d, tf), lambda i, f: (layer, 0, f)),
            pl.BlockSpec((None, tf, d), lambda i, f: (layer, f, 0)),
            pl.BlockSpec((None, 1, d), lambda i, f: (layer, 0, 0)),
        ],
        out_specs=pl.BlockSpec((tm, d), lambda i, f: (i, 0)),
        out_shape=jax.ShapeDtypeStruct((t, d), F32),
        compiler_params=_params("parallel", "arbitrary"),
        name="mlp",
    )(h, w1, w2, g_post)


def _ple_body(x_ref, dx_ref, p_ref, gpre_ref, wg_ref, wp_ref, gpost_ref, *rest):
    wpb_ref = rest[-1]

    @pl.when(pl.program_id(0) == 0)
    def _():
        wpb_ref[...] = wp_ref[...].astype(BF16)

    for rows in _row_chunks(x_ref.shape[0]):
        x = x_ref[rows, :] + dx_ref[rows, :]
        h = _rms(x, gpre_ref[...]).astype(BF16)
        gate = _sigmoid(jnp.dot(h, wg_ref[...], preferred_element_type=F32))
        emb = jnp.dot(p_ref[rows, :].astype(BF16), wpb_ref[...], preferred_element_type=F32)
        x = x + _rms(gate * emb, gpost_ref[...])
        if len(rest) == 2:
            o_ref = rest[0]
        else:
            gnext_ref, o_ref, h_ref = rest[:3]
            h_ref[rows, :] = _rms(x, gnext_ref[...]).astype(h_ref.dtype)
        o_ref[rows, :] = x


def per_layer_embedding(x, dx, p, g_pre, w_gate, w_proj, g_post, layer, g_next=None, tm=512):
    t, d = x.shape
    pd = p.shape[-1]
    tm = min(tm, t)
    rows = pl.BlockSpec((tm, d), lambda i: (i, 0))
    gain = pl.BlockSpec((None, 1, d), lambda i: (layer, 0, 0))
    in_specs = [
        rows,
        rows,
        pl.BlockSpec((None, tm, pd), lambda i: (layer, i, 0)),
        gain,
        pl.BlockSpec((None, d, d), lambda i: (layer, 0, 0), pipeline_mode=pl.Buffered(1)),
        pl.BlockSpec((None, pd, d), lambda i: (layer, 0, 0), pipeline_mode=pl.Buffered(1)),
        gain,
    ]
    args = [x, dx, p, g_pre, w_gate, w_proj, g_post]
    out_specs = [rows]
    out_shape = [jax.ShapeDtypeStruct((t, d), F32)]
    if g_next is not None:
        in_specs.append(pl.BlockSpec((None, 1, d), lambda i: (layer + 1, 0, 0)))
        args.append(g_next)
        out_specs.append(rows)
        out_shape.append(jax.ShapeDtypeStruct((t, d), BF16))
    return pl.pallas_call(
        _ple_body,
        grid=(t // tm,),
        in_specs=in_specs,
        out_specs=out_specs,
        out_shape=out_shape,
        scratch_shapes=[pltpu.VMEM((pd, d), BF16)],
        compiler_params=_params("arbitrary"),
        name="per_layer_embedding",
    )(*args)


def _row(v):
    return v[:, None, :]


def _lane_pad(v):
    return jnp.pad(v, ((0, 0), (0, LANES - v.shape[-1])))[:, None, :]


def kernel(x, p, rel_bias, g_mix_pre, w_in, ssm_conv_w, ssm_conv_b, ssm_dt_bias, ssm_a_log, ssm_d, ssm_norm_g, sc_conv_w, w_br_attn, w_br_ssm, w_br_conv, w_out, g_mix_post, g_mlp_pre, w_mlp_up, w_mlp_down, g_mlp_post, g_ple_pre, w_ple_gate, w_ple_proj, g_ple_post):
    batch, seq, d = x.shape
    depth = w_in.shape[0]
    t = batch * seq

    w_in_t = jnp.swapaxes(w_in, 1, 2)
    w_a, w_m, w_c = w_br_attn.astype(BF16), w_br_ssm.astype(BF16), w_br_conv.astype(BF16)
    w_pg = w_ple_gate.astype(BF16)
    d_skip = jnp.repeat(ssm_d, SSM_HEAD_DIM, axis=-1)[:, None, :]
    dt_bias, a_log = _lane_pad(ssm_dt_bias), _lane_pad(ssm_a_log)
    conv_b, norm_g = _row(ssm_conv_b), _row(ssm_norm_g)
    g_mix_pre, g_mix_post = _row(g_mix_pre), _row(g_mix_post)
    g_mlp_pre, g_mlp_post = _row(g_mlp_pre), _row(g_mlp_post)
    g_ple_pre, g_ple_post = _row(g_ple_pre), _row(g_ple_post)

    bias = bias_tiles(rel_bias)
    xf = x.reshape(t, d)
    pf = p.reshape(depth, t, p.shape[-1])
    h = pre_norm(xf, g_mix_pre, 0)
    for i in range(depth):
        proj = in_projection(h, w_in_t, i)
        y_attn = moba_attention(proj, bias, batch, seq, QKV_OFF)
        dt_raw = dt_projection(h, w_in_t, i)
        y_ssm, y_sc = ssd_and_short_conv(proj, dt_raw, ssm_conv_w, conv_b, dt_bias, a_log, d_skip, norm_g,
                                         sc_conv_w, i, batch, seq)
        merged = branch_merge(y_attn, y_ssm, y_sc, proj, w_a, w_m, w_c, i)
        xf, h_mlp = out_projection(merged, xf, w_out, g_mix_post, g_mlp_pre, i)
        dx = mlp(h_mlp, w_mlp_up, w_mlp_down, g_mlp_post, i)
        if i + 1 < depth:
            xf, h = per_layer_embedding(xf, dx, pf, g_ple_pre, w_pg, w_ple_proj, g_ple_post, i, g_next=g_mix_pre)
        else:
            (xf,) = per_layer_embedding(xf, dx, pf, g_ple_pre, w_pg, w_ple_proj, g_ple_post, i)
    return xf.reshape(batch, seq, d)
```

```python
import functools
import math

import numpy as np
import jax
import jax.numpy as jnp
from jax import lax
from jax.experimental import pallas as pl
from jax.experimental.pallas import tpu as pltpu

F32 = jnp.float32
BF16 = jnp.bfloat16

NORM_EPS = 1e-6
NEG_INF = -1e30
LOG2E = math.log2(math.e)

D_MODEL = 2048
PLE_DIM = 256
ATTN_HEADS = 8
ATTN_HEAD_DIM = 128
ATTN_WIDTH = ATTN_HEADS * ATTN_HEAD_DIM
MOBA_BLOCK = 256
MOBA_TOPK = 3
REL_BUCKETS = 32
REL_MAX_DISTANCE = 128
SSM_D_INNER = D_MODEL
SSM_HEAD_DIM = 64
SSM_HEADS = SSM_D_INNER // SSM_HEAD_DIM
SSM_GROUPS = 4
SSM_STATE = 128
SSM_CONV = 4
SSM_CHUNK = 256
SSD_SUB = 128
SSM_CONV_DIM = SSM_D_INNER + 2 * SSM_GROUPS * SSM_STATE
SC_WIDTH = D_MODEL // 2
SC_CONV = 3
D_FF = 4 * D_MODEL

LANES = 128
SUBLANES = 8
VMEM_LIMIT_BYTES = 60 * 1024 * 1024

XBC_OFF = 0
SC_OFF = XBC_OFF + SSM_CONV_DIM
GATE_OFF = SC_OFF + 3 * SC_WIDTH
Z_OFF = GATE_OFF + 3 * D_MODEL
QKV_OFF = Z_OFF + SSM_D_INNER
PROJ_WIDTH = QKV_OFF + 3 * ATTN_WIDTH
PROJ_TN = 1024

_O_Q = 0
_O_Z = 3 * ATTN_WIDTH
_O_XBC = _O_Z + SSM_D_INNER
_O_DT = _O_XBC + SSM_CONV_DIM
_O_SC = _O_DT + SSM_HEADS
_O_GATE = _O_SC + 3 * SC_WIDTH
_O_END = _O_GATE + 3 * D_MODEL

_NT = (((1,), (1,)), ((), ()))


def _params(*sem):
    return pltpu.CompilerParams(dimension_semantics=sem, vmem_limit_bytes=VMEM_LIMIT_BYTES)


def _rms(x, g):
    return x * lax.rsqrt(jnp.mean(x * x, axis=-1, keepdims=True) + NORM_EPS) * g


def _sigmoid(x):
    return 1.0 / (1.0 + jnp.exp(-x))


def _row_chunks(n, parts=2):
    step = n // parts
    return [slice(k * step, (k + 1) * step) for k in range(parts)]


_T = PROJ_TN
_N_XBC, _N_TAIL, _N_Z, _N_QKV = SSM_CONV_DIM // _T, (_O_END - _O_SC) // _T, SSM_D_INNER // _T, 3 * ATTN_WIDTH // _T
_J_TAIL, _J_Z, _J_QKV = _N_XBC, _N_XBC + _N_TAIL, _N_XBC + _N_TAIL + _N_Z
assert (_J_QKV + _N_QKV) * _T == PROJ_WIDTH
assert all(o % SUBLANES == 0 for o in (_O_XBC, _O_SC, _O_Z, _O_Q, _O_DT)) and _T % SUBLANES == 0


def _src_row(j):
    s, ts = SUBLANES, _T // SUBLANES
    groups = jnp.where(j < _J_TAIL, _O_XBC // s + ts * j,
                       jnp.where(j < _J_Z, _O_SC // s + ts * (j - _J_TAIL),
                                 jnp.where(j < _J_QKV, _O_Z // s + ts * (j - _J_Z), _O_Q // s + ts * (j - _J_QKV))))
    return groups * s


def _norm_body(x_ref, g_ref, o_ref):
    o_ref[...] = _rms(x_ref[...], g_ref[...]).astype(o_ref.dtype)


def pre_norm(x, g, layer, tm=512):
    t, d = x.shape
    tm = min(tm, t)
    return pl.pallas_call(
        _norm_body,
        grid=(t // tm,),
        in_specs=[pl.BlockSpec((tm, d), lambda i: (i, 0)),
                  pl.BlockSpec((None, 1, d), lambda i: (layer, 0, 0))],
        out_specs=pl.BlockSpec((tm, d), lambda i: (i, 0)),
        out_shape=jax.ShapeDtypeStruct((t, d), BF16),
        compiler_params=_params("parallel"),
        name="pre_norm",
    )(x, g)


def _inproj_body(h_ref, wt_ref, o_ref):
    o_ref[...] = lax.dot_general(h_ref[...], wt_ref[0].astype(BF16), _NT, preferred_element_type=F32)


def in_projection(h, w_t, layer, tm=2048):
    t, d = h.shape
    tn = _T
    tm = min(tm, t)
    return pl.pallas_call(
        _inproj_body,
        grid=(t // tm, PROJ_WIDTH // tn),
        in_specs=[
            pl.BlockSpec((tm, d), lambda i, j: (i, 0)),
            pl.BlockSpec((pl.Element(1), pl.Element(tn), pl.Element(d)), lambda i, j: (layer, _src_row(j), 0)),
        ],
        out_specs=pl.BlockSpec((tm, tn), lambda i, j: (i, j)),
        out_shape=jax.ShapeDtypeStruct((t, PROJ_WIDTH), F32),
        compiler_params=_params("parallel", "arbitrary"),
        name="in_projection",
    )(h, w_t)


def dt_projection(h, w_t, layer, tm=2048):
    t, d = h.shape
    tm = min(tm, t)
    return pl.pallas_call(
        _inproj_body,
        grid=(t // tm,),
        in_specs=[
            pl.BlockSpec((tm, d), lambda i: (i, 0)),
            pl.BlockSpec((pl.Element(1), pl.Element(LANES), pl.Element(d)), lambda i: (layer, _O_DT, 0)),
        ],
        out_specs=pl.BlockSpec((tm, LANES), lambda i: (i, 0)),
        out_shape=jax.ShapeDtypeStruct((t, LANES), F32),
        compiler_params=_params("parallel"),
        name="dt_projection",
    )(h, w_t)


def _bias_body(tbl_ref, o_ref):
    h = pl.program_id(0)
    blk = MOBA_BLOCK
    key = lax.broadcasted_iota(jnp.int32, (blk, blk), 0)
    qry = lax.broadcasted_iota(jnp.int32, (blk, blk), 1)
    max_exact = REL_BUCKETS // 2
    for t in range(3):
        n = jnp.maximum(t * blk + qry - key, 0)
        nf = jnp.maximum(n, 1).astype(F32)
        large = max_exact + (jnp.log(nf / max_exact) / math.log(REL_MAX_DISTANCE / max_exact)
                             * (REL_BUCKETS - max_exact)).astype(jnp.int32)
        large = jnp.minimum(large, REL_BUCKETS - 1)
        bucket = jnp.where(n < max_exact, n, large)
        acc = jnp.zeros((blk, blk), F32)
        for b in range(REL_BUCKETS):
            acc = jnp.where(bucket == b, tbl_ref[b, h], acc)
        o_ref[t] = acc


def bias_tiles(rel_bias):
    nh = rel_bias.shape[1]
    return pl.pallas_call(
        _bias_body,
        grid=(nh,),
        in_specs=[pl.BlockSpec(memory_space=pltpu.SMEM)],
        out_specs=pl.BlockSpec((None, 3, MOBA_BLOCK, MOBA_BLOCK), lambda h: (h, 0, 0, 0)),
        out_shape=jax.ShapeDtypeStruct((nh, 3, MOBA_BLOCK, MOBA_BLOCK), F32),
        compiler_params=_params("arbitrary"),
        name="bias_tiles",
    )(rel_bias)


def _attn_body(q_ref, k_ref, v_ref, bias_ref, o_ref, kmean_ref, *, nb):
    blk = MOBA_BLOCK
    scale = ATTN_HEAD_DIM ** -0.5

    k = k_ref[...]
    kb = k.astype(BF16)
    v_t = v_ref[...].T.astype(BF16)
    for j in range(nb):
        kmean_ref[j:j + 1, :] = jnp.mean(k[j * blk:(j + 1) * blk, :], axis=0, keepdims=True)
    q_all = q_ref[...]
    gate_all = lax.dot_general(kmean_ref[...], q_all, _NT, precision=lax.Precision.HIGHEST,
                               preferred_element_type=F32)
    qb_all = (q_all * (scale * LOG2E)).astype(BF16)

    key = lax.broadcasted_iota(jnp.int32, (blk, blk), 0)
    qry = lax.broadcasted_iota(jnp.int32, (blk, blk), 1)
    causal = key <= qry
    bias_own = bias_ref[0] * LOG2E
    bias_prev = bias_ref[1] * LOG2E
    bias_far = bias_ref[2, 0:1, 0:1] * LOG2E
    bidx = lax.broadcasted_iota(jnp.int32, (gate_all.shape[0], blk), 0)

    def scores(qi):
        return lax.dot_general(kb[0:(qi + 1) * blk, :], qb_all[qi * blk:(qi + 1) * blk, :], _NT,
                               preferred_element_type=F32)

    def finish(cols, nk, p_t, denom):
        o_t = jnp.dot(v_t[:, 0:nk], p_t, preferred_element_type=F32)
        o_ref[cols, :] = (o_t / denom).T.astype(o_ref.dtype)

    pending = None
    s_next = scores(0)
    for qi in range(nb):
        cols = slice(qi * blk, (qi + 1) * blk)
        nk = (qi + 1) * blk
        s = s_next
        if qi + 1 < nb:
            s_next = scores(qi + 1)

        keep = None
        if qi > MOBA_TOPK:
            gate = gate_all[:, cols]
            past = bidx < qi
            keep = []
            for j in range(qi):
                g_j = gate[j:j + 1, :]
                beats = ((gate > g_j) | ((gate == g_j) & (bidx < j))) & past
                keep.append(jnp.sum(beats.astype(F32), axis=0, keepdims=True) < MOBA_TOPK)

        pieces = []
        for j in range(qi + 1):
            s_j = s[j * blk:(j + 1) * blk, :]
            if j == qi:
                s_j = jnp.where(causal, s_j + bias_own, NEG_INF)
            elif j == qi - 1:
                s_j = s_j + bias_prev
                if keep is not None:
                    s_j = jnp.where(keep[j], s_j, NEG_INF)
            elif keep is not None:
                s_j = s_j + jnp.where(keep[j], bias_far, NEG_INF)
            else:
                s_j = s_j + bias_far
            pieces.append(s_j)

        m = pieces[0]
        for s_j in pieces[1:]:
            m = jnp.maximum(m, s_j)
        m = jnp.max(m, axis=0, keepdims=True)
        probs = [jnp.exp2(s_j - m) for s_j in pieces]
        total = probs[0]
        for p_j in probs[1:]:
            total = total + p_j
        denom = jnp.sum(total, axis=0, keepdims=True)
        p_t = jnp.concatenate([p_j.astype(BF16) for p_j in probs], axis=0)
        if pending is not None:
            finish(*pending)
        pending = (cols, nk, p_t, denom)
    finish(*pending)


def moba_attention(proj, bias, batch, seq, qkv_off):
    t = proj.shape[0]
    blk, dh, nh = MOBA_BLOCK, ATTN_HEAD_DIM, ATTN_HEADS
    nb = seq // blk
    qc = qkv_off // dh
    return pl.pallas_call(
        functools.partial(_attn_body, nb=nb),
        grid=(batch, nh),
        in_specs=[
            pl.BlockSpec((seq, dh), lambda b, h: (b, qc + h)),
            pl.BlockSpec((seq, dh), lambda b, h: (b, qc + nh + h)),
            pl.BlockSpec((seq, dh), lambda b, h: (b, qc + 2 * nh + h)),
            pl.BlockSpec((None, 3, blk, blk), lambda b, h: (h, 0, 0, 0)),
        ],
        out_specs=pl.BlockSpec((seq, dh), lambda b, h: (b, h)),
        out_shape=jax.ShapeDtypeStruct((t, nh * dh), BF16),
        scratch_shapes=[pltpu.VMEM((max(nb, SUBLANES), dh), F32)],
        compiler_params=_params("parallel", "parallel"),
        name="moba_attention",
    )(proj, proj, proj, bias)


def _hi_lo(v):
    hi = v.astype(BF16)
    lo = (v - hi.astype(F32)).astype(BF16)
    return jnp.concatenate([hi, lo], axis=1)


def _causal_taps(ext_ref, cs, w_ref, rows):
    width = w_ref.shape[0]
    cols = w_ref[:, cs].shape[1]
    shape = (rows // SUBLANES, SUBLANES, cols)
    prev = ext_ref[0:rows, cs].reshape(shape)
    cur = ext_ref[SUBLANES:SUBLANES + rows, cs].reshape(shape)
    sub = lax.broadcasted_iota(jnp.int32, shape, 1)
    acc = w_ref[width - 1:width, cs].reshape(1, 1, cols) * cur
    for j in range(1, width):
        shifted = pltpu.roll(jnp.where(sub >= SUBLANES - j, prev, cur), j, 1)
        acc = acc + w_ref[width - 1 - j:width - j, cs].reshape(1, 1, cols) * shifted
    return acc.reshape(rows, cols)


def _mixer_body(xbc_ref, sc_ref, z_ref, dt_ref, cw_ref, cb_ref, dtb_ref, alog_ref, dskip_ref, ng_ref,
                scw_ref, e_ref, yssm_ref, ysc_ref, ext_ref, ext2_ref, xc_ref, state_ref):
    c = pl.program_id(1)
    L = SSM_CHUNK
    sub = SSD_SUB
    halo = SUBLANES
    gw = SSM_D_INNER // SSM_GROUPS
    hpg = SSM_HEADS // SSM_GROUPS
    n_state = SSM_STATE

    @pl.when(c == 0)
    def _():
        ext_ref[0:halo, :] = jnp.zeros((halo, ext_ref.shape[1]), F32)
        ext2_ref[0:halo, :] = jnp.zeros((halo, ext2_ref.shape[1]), F32)
        state_ref[...] = jnp.zeros(state_ref.shape, F32)

    @pl.when(c > 0)
    def _():
        ext_ref[0:halo, :] = ext_ref[L:L + halo, :]
        ext2_ref[0:halo, :] = ext2_ref[L:L + halo, :]

    ext_ref[halo:halo + L, :] = xbc_ref[...]
    cchunk = 512
    for ci in range(SSM_CONV_DIM // cchunk):
        cs = slice(ci * cchunk, (ci + 1) * cchunk)
        v = _causal_taps(ext_ref, cs, cw_ref, L) + cb_ref[:, cs]
        xc_ref[:, cs] = v * _sigmoid(v)

    ext2_ref[halo:halo + L, :] = sc_ref[:, SC_WIDTH:2 * SC_WIDTH] * sc_ref[:, 2 * SC_WIDTH:3 * SC_WIDTH]
    for ci in range(SC_WIDTH // cchunk):
        cs = slice(ci * cchunk, (ci + 1) * cchunk)
        ysc_ref[:, cs] = (sc_ref[:, cs] * _causal_taps(ext2_ref, cs, scw_ref, L)).astype(ysc_ref.dtype)

    rows = lax.broadcasted_iota(jnp.int32, (L, LANES), 0)
    head_lane = lax.broadcasted_iota(jnp.int32, (L, LANES), 1) < SSM_HEADS
    x_dt = jnp.where(head_lane, dt_ref[...], 0.0) + dtb_ref[...]
    dt = jnp.maximum(x_dt, 0.0) + jnp.log1p(jnp.exp(-jnp.abs(x_dt)))
    a = dt * (-LOG2E * jnp.exp(alog_ref[...]))
    pos = rows % sub
    a_cum = a
    sh = 1
    while sh < sub:
        a_cum = a_cum + jnp.where(pos >= sh, pltpu.roll(a_cum, sh, axis=0), 0.0)
        sh *= 2
    a_cum_t = a_cum.T
    ea = jnp.exp2(a_cum)
    a_end = a_cum[L - 1:L, :]
    for s in range(L // sub - 2, -1, -1):
        a_end = jnp.where(rows < (s + 1) * sub, a_cum[(s + 1) * sub - 1:(s + 1) * sub, :], a_end)
    ds = jnp.exp2(a_end - a_cum)
    dt_hl, ea_hl, ds_hl = _hi_lo(dt), _hi_lo(ea), _hi_lo(ds)

    row = lax.broadcasted_iota(jnp.int32, (sub, sub), 0)
    col = lax.broadcasted_iota(jnp.int32, (sub, sub), 1)
    causal = row >= col
    low_half = lax.broadcasted_iota(jnp.int32, (sub, LANES), 1) < SSM_HEAD_DIM

    for g in range(SSM_GROUPS):
        cs = slice(g * gw, (g + 1) * gw)
        b_cols = slice(SSM_D_INNER + g * n_state, SSM_D_INNER + (g + 1) * n_state)
        c_cols = slice(SSM_D_INNER + (SSM_GROUPS + g) * n_state, SSM_D_INNER + (SSM_GROUPS + g + 1) * n_state)
        e_g = e_ref[:, cs]
        xs_g = xc_ref[:, cs]
        ea_e = jnp.dot(ea_hl, e_g, preferred_element_type=F32)
        xdt = xs_g * jnp.dot(dt_hl, e_g, preferred_element_type=F32)
        xdt_b = xdt.astype(BF16)
        xds_b = (xdt * jnp.dot(ds_hl, e_g, preferred_element_type=F32)).astype(BF16)

        st = state_ref[g]
        y_rows = []
        for s in range(L // sub):
            rs = slice(s * sub, (s + 1) * sub)
            b_s = xc_ref[rs, b_cols]
            c_sb = xc_ref[rs, c_cols].astype(BF16)
            cb = lax.dot_general(c_sb, b_s.astype(BF16), _NT, preferred_element_type=F32)
            y = jnp.dot(c_sb, st.astype(BF16), preferred_element_type=F32) * ea_e[rs, :]

            pieces = []
            for pe in range(hpg // 2):
                x_pair = xdt_b[rs, pe * LANES:(pe + 1) * LANES]
                acc = None
                for half in range(2):
                    h = g * hpg + 2 * pe + half
                    seg = a_cum[rs, h:h + 1] - a_cum_t[h:h + 1, rs]
                    decay = jnp.exp2(jnp.where(causal, seg, -jnp.inf))
                    m_h = (cb * decay).astype(BF16)
                    keep = low_half if half == 0 else jnp.logical_not(low_half)
                    x_h = jnp.where(keep, x_pair, jnp.zeros_like(x_pair))
                    term = jnp.dot(m_h, x_h, preferred_element_type=F32)
                    acc = term if acc is None else acc + term
                pieces.append(acc)
            y_rows.append(y + jnp.concatenate(pieces, axis=1))

            upd = jnp.dot(b_s.T.astype(BF16), xds_b[rs, :], preferred_element_type=F32)
            st = st * ea_e[(s + 1) * sub - 1:(s + 1) * sub, :] + upd
        state_ref[g] = st

        y = jnp.concatenate(y_rows, axis=0) + dskip_ref[:, cs] * xs_g
        z_g = z_ref[:, cs]
        y = y * (z_g * _sigmoid(z_g))
        y = y * lax.rsqrt(jnp.mean(y * y, axis=-1, keepdims=True) + NORM_EPS)
        yssm_ref[:, cs] = (y * ng_ref[:, cs]).astype(yssm_ref.dtype)


def _head_expand_matrix():
    e = np.zeros((2 * LANES, SSM_D_INNER), np.float32)
    for h in range(SSM_HEADS):
        e[h, h * SSM_HEAD_DIM:(h + 1) * SSM_HEAD_DIM] = 1.0
        e[LANES + h, h * SSM_HEAD_DIM:(h + 1) * SSM_HEAD_DIM] = 1.0
    return jnp.asarray(e, BF16)


def ssd_and_short_conv(proj, dt_raw, conv_w, conv_b, dt_bias, a_log, d_skip, norm_g, sc_w, layer, batch, seq):
    t = proj.shape[0]
    L = SSM_CHUNK
    nc = seq // L
    row_map = lambda b, c: (b * nc + c, 0)

    def col_block(off, width):
        return pl.BlockSpec((L, width), lambda b, c: (b * nc + c, off // width))

    def param(rows, cols):
        return pl.BlockSpec((None, rows, cols), lambda b, c: (layer, 0, 0))

    return pl.pallas_call(
        _mixer_body,
        grid=(batch, nc),
        in_specs=[
            col_block(XBC_OFF, SSM_CONV_DIM),
            col_block(SC_OFF, 3 * SC_WIDTH),
            col_block(Z_OFF, SSM_D_INNER),
            pl.BlockSpec((L, LANES), row_map),
            param(SSM_CONV, SSM_CONV_DIM),
            param(1, SSM_CONV_DIM),
            param(1, LANES),
            param(1, LANES),
            param(1, SSM_D_INNER),
            param(1, SSM_D_INNER),
            param(SC_CONV, SC_WIDTH),
            pl.BlockSpec((2 * LANES, SSM_D_INNER), lambda b, c: (0, 0)),
        ],
        out_specs=[
            pl.BlockSpec((L, SSM_D_INNER), row_map),
            pl.BlockSpec((L, SC_WIDTH), row_map),
        ],
        out_shape=[
            jax.ShapeDtypeStruct((t, SSM_D_INNER), BF16),
            jax.ShapeDtypeStruct((t, SC_WIDTH), BF16),
        ],
        scratch_shapes=[
            pltpu.VMEM((L + 2 * SUBLANES, SSM_CONV_DIM), F32),
            pltpu.VMEM((L + 2 * SUBLANES, SC_WIDTH), F32),
            pltpu.VMEM((L, SSM_CONV_DIM), F32),
            pltpu.VMEM((SSM_GROUPS, SSM_STATE, SSM_D_INNER // SSM_GROUPS), F32),
        ],
        compiler_params=_params("parallel", "arbitrary"),
        name="ssd_short_conv",
    )(proj, proj, proj, dt_raw, conv_w, conv_b, dt_bias, a_log, d_skip, norm_g, sc_w, _head_expand_matrix())


def _merge_body(ya_ref, ym_ref, yc_ref, ga_ref, gm_ref, gc_ref, wa_ref, wm_ref, wc_ref, o_ref):
    tn = o_ref.shape[1]
    cols = pl.ds(pl.multiple_of(pl.program_id(1) * tn, tn), tn)
    a = jnp.dot(ya_ref[...], wa_ref[:, cols], preferred_element_type=F32)
    m = jnp.dot(ym_ref[...], wm_ref[:, cols], preferred_element_type=F32)
    c = jnp.dot(yc_ref[...], wc_ref[:, cols], preferred_element_type=F32)
    o = _sigmoid(ga_ref[...]) * a + _sigmoid(gm_ref[...]) * m + _sigmoid(gc_ref[...]) * c
    o_ref[...] = o.astype(o_ref.dtype)


def branch_merge(y_attn, y_ssm, y_sc, proj, w_a, w_m, w_c, layer, tm=1024, tn=512):
    t = y_attn.shape[0]
    d = w_a.shape[-1]
    tm = min(tm, t)
    gc = GATE_OFF // tn
    gs = d // tn

    def act(width):
        return pl.BlockSpec((tm, width), lambda i, j: (i, 0))

    def gate(k):
        return pl.BlockSpec((tm, tn), lambda i, j: (i, gc + k * gs + j))

    def weight(rows):
        return pl.BlockSpec((None, rows, d), lambda i, j: (layer, 0, 0), pipeline_mode=pl.Buffered(1))

    return pl.pallas_call(
        _merge_body,
        grid=(t // tm, d // tn),
        in_specs=[act(y_attn.shape[1]), act(y_ssm.shape[1]), act(y_sc.shape[1]),
                  gate(0), gate(1), gate(2),
                  weight(w_a.shape[1]), weight(w_m.shape[1]), weight(w_c.shape[1])],
        out_specs=pl.BlockSpec((tm, tn), lambda i, j: (i, j)),
        out_shape=jax.ShapeDtypeStruct((t, d), BF16),
        compiler_params=_params("parallel", "arbitrary"),
        name="branch_merge",
    )(y_attn, y_ssm, y_sc, proj, proj, proj, w_a, w_m, w_c)


def _outproj_body(m_ref, x_ref, w_ref, gpost_ref, gnext_ref, o_ref, h_ref, wb_ref):
    @pl.when(pl.program_id(0) == 0)
    def _():
        wb_ref[...] = w_ref[...].astype(BF16)

    for rows in _row_chunks(m_ref.shape[0]):
        o = jnp.dot(m_ref[rows, :], wb_ref[...], preferred_element_type=F32)
        x = x_ref[rows, :] + _rms(o, gpost_ref[...])
        o_ref[rows, :] = x
        h_ref[rows, :] = _rms(x, gnext_ref[...]).astype(h_ref.dtype)


def out_projection(merged, x, w, g_post, g_next, layer, tm=512):
    t, d = x.shape
    tm = min(tm, t)
    rows = pl.BlockSpec((tm, d), lambda i: (i, 0))
    gain = pl.BlockSpec((None, 1, d), lambda i: (layer, 0, 0))
    return pl.pallas_call(
        _outproj_body,
        grid=(t // tm,),
        in_specs=[rows, rows, pl.BlockSpec((None, d, d), lambda i: (layer, 0, 0), pipeline_mode=pl.Buffered(1)),
                  gain, gain],
        out_specs=[rows, rows],
        out_shape=[jax.ShapeDtypeStruct((t, d), F32), jax.ShapeDtypeStruct((t, d), BF16)],
        scratch_shapes=[pltpu.VMEM((d, d), BF16)],
        compiler_params=_params("arbitrary"),
        name="out_projection",
    )(merged, x, w, g_post, g_next)


def _mlp_body(h_ref, w1_ref, w2_ref, gpost_ref, o_ref):
    f = pl.program_id(1)

    @pl.when(f == 0)
    def _():
        o_ref[...] = jnp.zeros(o_ref.shape, F32)

    u = jnp.maximum(jnp.dot(h_ref[...], w1_ref[...].astype(BF16), preferred_element_type=F32), 0.0)
    o_ref[...] += jnp.dot((u * u).astype(BF16), w2_ref[...].astype(BF16), preferred_element_type=F32)

    @pl.when(f == pl.num_programs(1) - 1)
    def _():
        o_ref[...] = _rms(o_ref[...], gpost_ref[...])


def mlp(h, w1, w2, g_post, layer, tm=1024, tf=512):
    t, d = h.shape
    ff = w1.shape[-1]
    tm = min(tm, t)
    return pl.pallas_call(
        _mlp_body,
        grid=(t // tm, ff // tf),
        in_specs=[
            pl.BlockSpec((tm, d), lambda i, f: (i, 0)),
            pl.BlockSpec((None, d, tf), lambda i, f: (layer, 0, f)),
            pl.BlockSpec((None, tf, d), lambda i, f: (layer, f, 0)),
            pl.BlockSpec((None, 1, d), lambda i, f: (layer, 0, 0)),
        ],
        out_specs=pl.BlockSpec((tm, d), lambda i, f: (i, 0)),
        out_shape=jax.ShapeDtypeStruct((t, d), F32),
        compiler_params=_params("parallel", "arbitrary"),
        name="mlp",
    )(h, w1, w2, g_post)


def _ple_body(x_ref, dx_ref, p_ref, gpre_ref, wg_ref, wp_ref, gpost_ref, *rest):
    wpb_ref = rest[-1]

    @pl.when(pl.program_id(0) == 0)
    def _():
        wpb_ref[...] = wp_ref[...].astype(BF16)

    for rows in _row_chunks(x_ref.shape[0]):
        x = x_ref[rows, :] + dx_ref[rows, :]
        h = _rms(x, gpre_ref[...]).astype(BF16)
        gate = _sigmoid(jnp.dot(h, wg_ref[...], preferred_element_type=F32))
        emb = jnp.dot(p_ref[rows, :].astype(BF16), wpb_ref[...], preferred_element_type=F32)
        x = x + _rms(gate * emb, gpost_ref[...])
        if len(rest) == 2:
            o_ref = rest[0]
        else:
            gnext_ref, o_ref, h_ref = rest[:3]
            h_ref[rows, :] = _rms(x, gnext_ref[...]).astype(h_ref.dtype)
        o_ref[rows, :] = x


def per_layer_embedding(x, dx, p, g_pre, w_gate, w_proj, g_post, layer, g_next=None, tm=512):
    t, d = x.shape
    pd = p.shape[-1]
    tm = min(tm, t)
    rows = pl.BlockSpec((tm, d), lambda i: (i, 0))
    gain = pl.BlockSpec((None, 1, d), lambda i: (layer, 0, 0))
    in_specs = [
        rows,
        rows,
        pl.BlockSpec((None, tm, pd), lambda i: (layer, i, 0)),
        gain,
        pl.BlockSpec((None, d, d), lambda i: (layer, 0, 0), pipeline_mode=pl.Buffered(1)),
        pl.BlockSpec((None, pd, d), lambda i: (layer, 0, 0), pipeline_mode=pl.Buffered(1)),
        gain,
    ]
    args = [x, dx, p, g_pre, w_gate, w_proj, g_post]
    out_specs = [rows]
    out_shape = [jax.ShapeDtypeStruct((t, d), F32)]
    if g_next is not None:
        in_specs.append(pl.BlockSpec((None, 1, d), lambda i: (layer + 1, 0, 0)))
        args.append(g_next)
        out_specs.append(rows)
        out_shape.append(jax.ShapeDtypeStruct((t, d), BF16))
    return pl.pallas_call(
        _ple_body,
        grid=(t // tm,),
        in_specs=in_specs,
        out_specs=out_specs,
        out_shape=out_shape,
        scratch_shapes=[pltpu.VMEM((pd, d), BF16)],
        compiler_params=_params("arbitrary"),
        name="per_layer_embedding",
    )(*args)


def _row(v):
    return v[:, None, :]


def _lane_pad(v):
    return jnp.pad(v, ((0, 0), (0, LANES - v.shape[-1])))[:, None, :]


def kernel(x, p, rel_bias, g_mix_pre, w_in, ssm_conv_w, ssm_conv_b, ssm_dt_bias, ssm_a_log, ssm_d, ssm_norm_g, sc_conv_w, w_br_attn, w_br_ssm, w_br_conv, w_out, g_mix_post, g_mlp_pre, w_mlp_up, w_mlp_down, g_mlp_post, g_ple_pre, w_ple_gate, w_ple_proj, g_ple_post):
    batch, seq, d = x.shape
    depth = w_in.shape[0]
    t = batch * seq

    w_in_t = jnp.swapaxes(w_in, 1, 2)
    w_a, w_m, w_c = w_br_attn.astype(BF16), w_br_ssm.astype(BF16), w_br_conv.astype(BF16)
    w_pg = w_ple_gate.astype(BF16)
    d_skip = jnp.repeat(ssm_d, SSM_HEAD_DIM, axis=-1)[:, None, :]
    dt_bias, a_log = _lane_pad(ssm_dt_bias), _lane_pad(ssm_a_log)
    conv_b, norm_g = _row(ssm_conv_b), _row(ssm_norm_g)
    g_mix_pre, g_mix_post = _row(g_mix_pre), _row(g_mix_post)
    g_mlp_pre, g_mlp_post = _row(g_mlp_pre), _row(g_mlp_post)
    g_ple_pre, g_ple_post = _row(g_ple_pre), _row(g_ple_post)

    bias = bias_tiles(rel_bias)
    xf = x.reshape(t, d)
    pf = p.reshape(depth, t, p.shape[-1])
    h = pre_norm(xf, g_mix_pre, 0)
    for i in range(depth):
        proj = in_projection(h, w_in_t, i)
        y_attn = moba_attention(proj, bias, batch, seq, QKV_OFF)
        dt_raw = dt_projection(h, w_in_t, i)
        y_ssm, y_sc = ssd_and_short_conv(proj, dt_raw, ssm_conv_w, conv_b, dt_bias, a_log, d_skip, norm_g,
                                         sc_conv_w, i, batch, seq)
        merged = branch_merge(y_attn, y_ssm, y_sc, proj, w_a, w_m, w_c, i)
        xf, h_mlp = out_projection(merged, xf, w_out, g_mix_post, g_mlp_pre, i)
        dx = mlp(h_mlp, w_mlp_up, w_mlp_down, g_mlp_post, i)
        if i + 1 < depth:
            xf, h = per_layer_embedding(xf, dx, pf, g_ple_pre, w_pg, w_ple_proj, g_ple_post, i, g_next=g_mix_pre)
        else:
            (xf,) = per_layer_embedding(xf, dx, pf, g_ple_pre, w_pg, w_ple_proj, g_ple_post, i)
    return xf.reshape(batch, seq, d)
```

```python
import functools
import math

import numpy as np
import jax
import jax.numpy as jnp
from jax import lax
from jax.experimental import pallas as pl
from jax.experimental.pallas import tpu as pltpu

F32 = jnp.float32
BF16 = jnp.bfloat16

NORM_EPS = 1e-6
NEG_INF = -1e30
LOG2E = math.log2(math.e)

D_MODEL = 2048
PLE_DIM = 256
ATTN_HEADS = 8
ATTN_HEAD_DIM = 128
ATTN_WIDTH = ATTN_HEADS * ATTN_HEAD_DIM
MOBA_BLOCK = 256
MOBA_TOPK = 3
REL_BUCKETS = 32
REL_MAX_DISTANCE = 128
SSM_D_INNER = D_MODEL
SSM_HEAD_DIM = 64
SSM_HEADS = SSM_D_INNER // SSM_HEAD_DIM
SSM_GROUPS = 4
SSM_STATE = 128
SSM_CONV = 4
SSM_CHUNK = 256
SSD_SUB = 128
SSM_CONV_DIM = SSM_D_INNER + 2 * SSM_GROUPS * SSM_STATE
SC_WIDTH = D_MODEL // 2
SC_CONV = 3
D_FF = 4 * D_MODEL

LANES = 128
SUBLANES = 8
VMEM_LIMIT_BYTES = 60 * 1024 * 1024

XBC_OFF = 0
SC_OFF = XBC_OFF + SSM_CONV_DIM
GATE_OFF = SC_OFF + 3 * SC_WIDTH
Z_OFF = GATE_OFF + 3 * D_MODEL
QKV_OFF = Z_OFF + SSM_D_INNER
PROJ_WIDTH = QKV_OFF + 3 * ATTN_WIDTH
PROJ_TN = 1024

_O_Q = 0
_O_Z = 3 * ATTN_WIDTH
_O_XBC = _O_Z + SSM_D_INNER
_O_DT = _O_XBC + SSM_CONV_DIM
_O_SC = _O_DT + SSM_HEADS
_O_GATE = _O_SC + 3 * SC_WIDTH
_O_END = _O_GATE + 3 * D_MODEL

_NT = (((1,), (1,)), ((), ()))


def _params(*sem):
    return pltpu.CompilerParams(dimension_semantics=sem, vmem_limit_bytes=VMEM_LIMIT_BYTES)


def _rms(x, g):
    return x * lax.rsqrt(jnp.mean(x * x, axis=-1, keepdims=True) + NORM_EPS) * g


def _sigmoid(x):
    return lax.logistic(x)


def _row_chunks(n, parts=2):
    step = n // parts
    return [slice(k * step, (k + 1) * step) for k in range(parts)]


_T = PROJ_TN
_N_XBC, _N_TAIL, _N_Z, _N_QKV = SSM_CONV_DIM // _T, (_O_END - _O_SC) // _T, SSM_D_INNER // _T, 3 * ATTN_WIDTH // _T
_J_TAIL, _J_Z, _J_QKV = _N_XBC, _N_XBC + _N_TAIL, _N_XBC + _N_TAIL + _N_Z
assert (_J_QKV + _N_QKV) * _T == PROJ_WIDTH
assert all(o % SUBLANES == 0 for o in (_O_XBC, _O_SC, _O_Z, _O_Q, _O_DT)) and _T % SUBLANES == 0


def _src_row(j):
    s, ts = SUBLANES, _T // SUBLANES
    groups = jnp.where(j < _J_TAIL, _O_XBC // s + ts * j,
                       jnp.where(j < _J_Z, _O_SC // s + ts * (j - _J_TAIL),
                                 jnp.where(j < _J_QKV, _O_Z // s + ts * (j - _J_Z), _O_Q // s + ts * (j - _J_QKV))))
    return groups * s


def _norm_body(x_ref, g_ref, o_ref):
    o_ref[...] = _rms(x_ref[...], g_ref[...]).astype(o_ref.dtype)


def pre_norm(x, g, layer, tm=512):
    t, d = x.shape
    tm = min(tm, t)
    return pl.pallas_call(
        _norm_body,
        grid=(t // tm,),
        in_specs=[pl.BlockSpec((tm, d), lambda i: (i, 0)),
                  pl.BlockSpec((None, 1, d), lambda i: (layer, 0, 0))],
        out_specs=pl.BlockSpec((tm, d), lambda i: (i, 0)),
        out_shape=jax.ShapeDtypeStruct((t, d), BF16),
        compiler_params=_params("parallel"),
        name="pre_norm",
    )(x, g)


def _inproj_body(h_ref, wt_ref, o_ref):
    o_ref[...] = lax.dot_general(h_ref[...], wt_ref[0].astype(BF16), _NT, preferred_element_type=F32)


def in_projection(h, w_t, layer, tm=2048):
    t, d = h.shape
    tn = _T
    tm = min(tm, t)
    return pl.pallas_call(
        _inproj_body,
        grid=(t // tm, PROJ_WIDTH // tn),
        in_specs=[
            pl.BlockSpec((tm, d), lambda i, j: (i, 0)),
            pl.BlockSpec((pl.Element(1), pl.Element(tn), pl.Element(d)), lambda i, j: (layer, _src_row(j), 0)),
        ],
        out_specs=pl.BlockSpec((tm, tn), lambda i, j: (i, j)),
        out_shape=jax.ShapeDtypeStruct((t, PROJ_WIDTH), F32),
        compiler_params=_params("parallel", "arbitrary"),
        name="in_projection",
    )(h, w_t)


def dt_projection(h, w_t, layer, tm=2048):
    t, d = h.shape
    tm = min(tm, t)
    return pl.pallas_call(
        _inproj_body,
        grid=(t // tm,),
        in_specs=[
            pl.BlockSpec((tm, d), lambda i: (i, 0)),
            pl.BlockSpec((pl.Element(1), pl.Element(LANES), pl.Element(d)), lambda i: (layer, _O_DT, 0)),
        ],
        out_specs=pl.BlockSpec((tm, LANES), lambda i: (i, 0)),
        out_shape=jax.ShapeDtypeStruct((t, LANES), F32),
        compiler_params=_params("parallel"),
        name="dt_projection",
    )(h, w_t)


def _bias_body(tbl_ref, o_ref):
    h = pl.program_id(0)
    blk = MOBA_BLOCK
    key = lax.broadcasted_iota(jnp.int32, (blk, blk), 0)
    qry = lax.broadcasted_iota(jnp.int32, (blk, blk), 1)
    max_exact = REL_BUCKETS // 2
    for t in range(3):
        n = jnp.maximum(t * blk + qry - key, 0)
        nf = jnp.maximum(n, 1).astype(F32)
        large = max_exact + (jnp.log(nf / max_exact) / math.log(REL_MAX_DISTANCE / max_exact)
                             * (REL_BUCKETS - max_exact)).astype(jnp.int32)
        large = jnp.minimum(large, REL_BUCKETS - 1)
        bucket = jnp.where(n < max_exact, n, large)
        acc = jnp.zeros((blk, blk), F32)
        for b in range(REL_BUCKETS):
            acc = jnp.where(bucket == b, tbl_ref[b, h], acc)
        o_ref[t] = acc


def bias_tiles(rel_bias):
    nh = rel_bias.shape[1]
    return pl.pallas_call(
        _bias_body,
        grid=(nh,),
        in_specs=[pl.BlockSpec(memory_space=pltpu.SMEM)],
        out_specs=pl.BlockSpec((None, 3, MOBA_BLOCK, MOBA_BLOCK), lambda h: (h, 0, 0, 0)),
        out_shape=jax.ShapeDtypeStruct((nh, 3, MOBA_BLOCK, MOBA_BLOCK), F32),
        compiler_params=_params("arbitrary"),
        name="bias_tiles",
    )(rel_bias)


def _attn_body(q_ref, k_ref, v_ref, bias_ref, o_ref, kmean_ref, *, nb):
    blk = MOBA_BLOCK
    scale = ATTN_HEAD_DIM ** -0.5

    k = k_ref[...]
    kb = k.astype(BF16)
    v_t = v_ref[...].T.astype(BF16)
    for j in range(nb):
        kmean_ref[j:j + 1, :] = jnp.mean(k[j * blk:(j + 1) * blk, :], axis=0, keepdims=True)
    q_all = q_ref[...]
    gate_all = lax.dot_general(kmean_ref[...], q_all, _NT, precision=lax.Precision.HIGHEST,
                               preferred_element_type=F32)
    qb_all = (q_all * (scale * LOG2E)).astype(BF16)

    key = lax.broadcasted_iota(jnp.int32, (blk, blk), 0)
    qry = lax.broadcasted_iota(jnp.int32, (blk, blk), 1)
    causal = key <= qry
    bias_own = bias_ref[0] * LOG2E
    bias_prev = bias_ref[1] * LOG2E
    bias_far = bias_ref[2, 0:1, 0:1] * LOG2E
    bidx = lax.broadcasted_iota(jnp.int32, (gate_all.shape[0], blk), 0)

    def scores(qi):
        return lax.dot_general(kb[0:(qi + 1) * blk, :], qb_all[qi * blk:(qi + 1) * blk, :], _NT,
                               preferred_element_type=F32)

    def finish(cols, nk, p_t, denom):
        o_t = jnp.dot(v_t[:, 0:nk], p_t, preferred_element_type=F32)
        o_ref[cols, :] = (o_t / denom).T.astype(o_ref.dtype)

    pending = None
    s_next = scores(0)
    for qi in range(nb):
        cols = slice(qi * blk, (qi + 1) * blk)
        nk = (qi + 1) * blk
        s = s_next
        if qi + 1 < nb:
            s_next = scores(qi + 1)

        keep = None
        if qi > MOBA_TOPK:
            gate = gate_all[:, cols]
            past = bidx < qi
            keep = []
            for j in range(qi):
                g_j = gate[j:j + 1, :]
                beats = ((gate > g_j) | ((gate == g_j) & (bidx < j))) & past
                keep.append(jnp.sum(beats.astype(F32), axis=0, keepdims=True) < MOBA_TOPK)

        pieces = []
        for j in range(qi + 1):
            s_j = s[j * blk:(j + 1) * blk, :]
            if j == qi:
                s_j = jnp.where(causal, s_j + bias_own, NEG_INF)
            elif j == qi - 1:
                s_j = s_j + bias_prev
                if keep is not None:
                    s_j = jnp.where(keep[j], s_j, NEG_INF)
            elif keep is not None:
                s_j = s_j + jnp.where(keep[j], bias_far, NEG_INF)
            else:
                s_j = s_j + bias_far
            pieces.append(s_j)

        m = pieces[0]
        for s_j in pieces[1:]:
            m = jnp.maximum(m, s_j)
        m = jnp.max(m, axis=0, keepdims=True)
        probs = [jnp.exp2(s_j - m) for s_j in pieces]
        total = probs[0]
        for p_j in probs[1:]:
            total = total + p_j
        denom = jnp.sum(total, axis=0, keepdims=True)
        p_t = jnp.concatenate([p_j.astype(BF16) for p_j in probs], axis=0)
        if pending is not None:
            finish(*pending)
        pending = (cols, nk, p_t, denom)
    finish(*pending)


def moba_attention(proj, bias, batch, seq, qkv_off):
    t = proj.shape[0]
    blk, dh, nh = MOBA_BLOCK, ATTN_HEAD_DIM, ATTN_HEADS
    nb = seq // blk
    qc = qkv_off // dh
    return pl.pallas_call(
        functools.partial(_attn_body, nb=nb),
        grid=(batch, nh),
        in_specs=[
            pl.BlockSpec((seq, dh), lambda b, h: (b, qc + h)),
            pl.BlockSpec((seq, dh), lambda b, h: (b, qc + nh + h)),
            pl.BlockSpec((seq, dh), lambda b, h: (b, qc + 2 * nh + h)),
            pl.BlockSpec((None, 3, blk, blk), lambda b, h: (h, 0, 0, 0)),
        ],
        out_specs=pl.BlockSpec((seq, dh), lambda b, h: (b, h)),
        out_shape=jax.ShapeDtypeStruct((t, nh * dh), BF16),
        scratch_shapes=[pltpu.VMEM((max(nb, SUBLANES), dh), F32)],
        compiler_params=_params("parallel", "parallel"),
        name="moba_attention",
    )(proj, proj, proj, bias)


def _hi_lo(v):
    hi = v.astype(BF16)
    lo = (v - hi.astype(F32)).astype(BF16)
    return jnp.concatenate([hi, lo], axis=1)


def _causal_taps(ext_ref, cs, w_ref, rows):
    width = w_ref.shape[0]
    cols = w_ref[:, cs].shape[1]
    shape = (rows // SUBLANES, SUBLANES, cols)
    prev = ext_ref[0:rows, cs].reshape(shape)
    cur = ext_ref[SUBLANES:SUBLANES + rows, cs].reshape(shape)
    sub = lax.broadcasted_iota(jnp.int32, shape, 1)
    acc = w_ref[width - 1:width, cs].reshape(1, 1, cols) * cur
    for j in range(1, width):
        shifted = pltpu.roll(jnp.where(sub >= SUBLANES - j, prev, cur), j, 1)
        acc = acc + w_ref[width - 1 - j:width - j, cs].reshape(1, 1, cols) * shifted
    return acc.reshape(rows, cols)


def _mixer_body(xbc_ref, sc_ref, z_ref, dt_ref, cw_ref, cb_ref, dtb_ref, alog_ref, dskip_ref, ng_ref,
                scw_ref, e_ref, yssm_ref, ysc_ref, ext_ref, ext2_ref, xc_ref, state_ref):
    c = pl.program_id(1)
    L = SSM_CHUNK
    sub = SSD_SUB
    halo = SUBLANES
    gw = SSM_D_INNER // SSM_GROUPS
    hpg = SSM_HEADS // SSM_GROUPS
    n_state = SSM_STATE

    @pl.when(c == 0)
    def _():
        ext_ref[0:halo, :] = jnp.zeros((halo, ext_ref.shape[1]), F32)
        ext2_ref[0:halo, :] = jnp.zeros((halo, ext2_ref.shape[1]), F32)
        state_ref[...] = jnp.zeros(state_ref.shape, F32)

    @pl.when(c > 0)
    def _():
        ext_ref[0:halo, :] = ext_ref[L:L + halo, :]
        ext2_ref[0:halo, :] = ext2_ref[L:L + halo, :]

    ext_ref[halo:halo + L, :] = xbc_ref[...]
    cchunk = 512
    for ci in range(SSM_CONV_DIM // cchunk):
        cs = slice(ci * cchunk, (ci + 1) * cchunk)
        v = _causal_taps(ext_ref, cs, cw_ref, L) + cb_ref[:, cs]
        xc_ref[:, cs] = v * _sigmoid(v)

    ext2_ref[halo:halo + L, :] = sc_ref[:, SC_WIDTH:2 * SC_WIDTH] * sc_ref[:, 2 * SC_WIDTH:3 * SC_WIDTH]
    for ci in range(SC_WIDTH // cchunk):
        cs = slice(ci * cchunk, (ci + 1) * cchunk)
        ysc_ref[:, cs] = (sc_ref[:, cs] * _causal_taps(ext2_ref, cs, scw_ref, L)).astype(ysc_ref.dtype)

    rows = lax.broadcasted_iota(jnp.int32, (L, LANES), 0)
    head_lane = lax.broadcasted_iota(jnp.int32, (L, LANES), 1) < SSM_HEADS
    x_dt = jnp.where(head_lane, dt_ref[...], 0.0) + dtb_ref[...]
    dt = jnp.maximum(x_dt, 0.0) + jnp.log1p(jnp.exp(-jnp.abs(x_dt)))
    a = dt * (-LOG2E * jnp.exp(alog_ref[...]))
    pos = rows % sub
    a_cum = a
    sh = 1
    while sh < sub:
        a_cum = a_cum + jnp.where(pos >= sh, pltpu.roll(a_cum, sh, axis=0), 0.0)
        sh *= 2
    a_cum_t = a_cum.T
    ea = jnp.exp2(a_cum)
    a_end = a_cum[L - 1:L, :]
    for s in range(L // sub - 2, -1, -1):
        a_end = jnp.where(rows < (s + 1) * sub, a_cum[(s + 1) * sub - 1:(s + 1) * sub, :], a_end)
    ds = jnp.exp2(a_end - a_cum)
    dt_hl, ea_hl, ds_hl = _hi_lo(dt), _hi_lo(ea), _hi_lo(ds)

    row = lax.broadcasted_iota(jnp.int32, (sub, sub), 0)
    col = lax.broadcasted_iota(jnp.int32, (sub, sub), 1)
    causal = row >= col
    low_half = lax.broadcasted_iota(jnp.int32, (sub, LANES), 1) < SSM_HEAD_DIM

    for g in range(SSM_GROUPS):
        cs = slice(g * gw, (g + 1) * gw)
        b_cols = slice(SSM_D_INNER + g * n_state, SSM_D_INNER + (g + 1) * n_state)
        c_cols = slice(SSM_D_INNER + (SSM_GROUPS + g) * n_state, SSM_D_INNER + (SSM_GROUPS + g + 1) * n_state)
        e_g = e_ref[:, cs]
        xs_g = xc_ref[:, cs]
        ea_e = jnp.dot(ea_hl, e_g, preferred_element_type=F32)
        xdt = xs_g * jnp.dot(dt_hl, e_g, preferred_element_type=F32)
        xdt_b = xdt.astype(BF16)
        xds_b = (xdt * jnp.dot(ds_hl, e_g, preferred_element_type=F32)).astype(BF16)

        st = state_ref[g]
        y_rows = []
        for s in range(L // sub):
            rs = slice(s * sub, (s + 1) * sub)
            b_s = xc_ref[rs, b_cols]
            c_sb = xc_ref[rs, c_cols].astype(BF16)
            cb = lax.dot_general(c_sb, b_s.astype(BF16), _NT, preferred_element_type=F32)
            y = jnp.dot(c_sb, st.astype(BF16), preferred_element_type=F32) * ea_e[rs, :]

            pieces = []
            for pe in range(hpg // 2):
                x_pair = xdt_b[rs, pe * LANES:(pe + 1) * LANES]
                acc = None
                for half in range(2):
                    h = g * hpg + 2 * pe + half
                    seg = a_cum[rs, h:h + 1] - a_cum_t[h:h + 1, rs]
                    decay = jnp.exp2(jnp.where(causal, seg, -jnp.inf))
                    m_h = (cb * decay).astype(BF16)
                    keep = low_half if half == 0 else jnp.logical_not(low_half)
                    x_h = jnp.where(keep, x_pair, jnp.zeros_like(x_pair))
                    term = jnp.dot(m_h, x_h, preferred_element_type=F32)
                    acc = term if acc is None else acc + term
                pieces.append(acc)
            y_rows.append(y + jnp.concatenate(pieces, axis=1))

            upd = jnp.dot(b_s.T.astype(BF16), xds_b[rs, :], preferred_element_type=F32)
            st = st * ea_e[(s + 1) * sub - 1:(s + 1) * sub, :] + upd
        state_ref[g] = st

        y = jnp.concatenate(y_rows, axis=0) + dskip_ref[:, cs] * xs_g
        z_g = z_ref[:, cs]
        y = y * (z_g * _sigmoid(z_g))
        y = y * lax.rsqrt(jnp.mean(y * y, axis=-1, keepdims=True) + NORM_EPS)
        yssm_ref[:, cs] = (y * ng_ref[:, cs]).astype(yssm_ref.dtype)


def _head_expand_matrix():
    e = np.zeros((2 * LANES, SSM_D_INNER), np.float32)
    for h in range(SSM_HEADS):
        e[h, h * SSM_HEAD_DIM:(h + 1) * SSM_HEAD_DIM] = 1.0
        e[LANES + h, h * SSM_HEAD_DIM:(h + 1) * SSM_HEAD_DIM] = 1.0
    return jnp.asarray(e, BF16)


def ssd_and_short_conv(proj, dt_raw, conv_w, conv_b, dt_bias, a_log, d_skip, norm_g, sc_w, layer, batch, seq):
    t = proj.shape[0]
    L = SSM_CHUNK
    nc = seq // L
    row_map = lambda b, c: (b * nc + c, 0)

    def col_block(off, width):
        return pl.BlockSpec((L, width), lambda b, c: (b * nc + c, off // width))

    def param(rows, cols):
        return pl.BlockSpec((None, rows, cols), lambda b, c: (layer, 0, 0))

    return pl.pallas_call(
        _mixer_body,
        grid=(batch, nc),
        in_specs=[
            col_block(XBC_OFF, SSM_CONV_DIM),
            col_block(SC_OFF, 3 * SC_WIDTH),
            col_block(Z_OFF, SSM_D_INNER),
            pl.BlockSpec((L, LANES), row_map),
            param(SSM_CONV, SSM_CONV_DIM),
            param(1, SSM_CONV_DIM),
            param(1, LANES),
            param(1, LANES),
            param(1, SSM_D_INNER),
            param(1, SSM_D_INNER),
            param(SC_CONV, SC_WIDTH),
            pl.BlockSpec((2 * LANES, SSM_D_INNER), lambda b, c: (0, 0)),
        ],
        out_specs=[
            pl.BlockSpec((L, SSM_D_INNER), row_map),
            pl.BlockSpec((L, SC_WIDTH), row_map),
        ],
        out_shape=[
            jax.ShapeDtypeStruct((t, SSM_D_INNER), BF16),
            jax.ShapeDtypeStruct((t, SC_WIDTH), BF16),
        ],
        scratch_shapes=[
            pltpu.VMEM((L + 2 * SUBLANES, SSM_CONV_DIM), F32),
            pltpu.VMEM((L + 2 * SUBLANES, SC_WIDTH), F32),
            pltpu.VMEM((L, SSM_CONV_DIM), F32),
            pltpu.VMEM((SSM_GROUPS, SSM_STATE, SSM_D_INNER // SSM_GROUPS), F32),
        ],
        compiler_params=_params("parallel", "arbitrary"),
        name="ssd_short_conv",
    )(proj, proj, proj, dt_raw, conv_w, conv_b, dt_bias, a_log, d_skip, norm_g, sc_w, _head_expand_matrix())


def _merge_body(ya_ref, ym_ref, yc_ref, ga_ref, gm_ref, gc_ref, wa_ref, wm_ref, wc_ref, o_ref):
    tn = o_ref.shape[1]
    cols = pl.ds(pl.multiple_of(pl.program_id(1) * tn, tn), tn)
    a = jnp.dot(ya_ref[...], wa_ref[:, cols], preferred_element_type=F32)
    m = jnp.dot(ym_ref[...], wm_ref[:, cols], preferred_element_type=F32)
    c = jnp.dot(yc_ref[...], wc_ref[:, cols], preferred_element_type=F32)
    o = _sigmoid(ga_ref[...]) * a + _sigmoid(gm_ref[...]) * m + _sigmoid(gc_ref[...]) * c
    o_ref[...] = o.astype(o_ref.dtype)


def branch_merge(y_attn, y_ssm, y_sc, proj, w_a, w_m, w_c, layer, tm=1024, tn=512):
    t = y_attn.shape[0]
    d = w_a.shape[-1]
    tm = min(tm, t)
    gc = GATE_OFF // tn
    gs = d // tn

    def act(width):
        return pl.BlockSpec((tm, width), lambda i, j: (i, 0))

    def gate(k):
        return pl.BlockSpec((tm, tn), lambda i, j: (i, gc + k * gs + j))

    def weight(rows):
        return pl.BlockSpec((None, rows, d), lambda i, j: (layer, 0, 0), pipeline_mode=pl.Buffered(1))

    return pl.pallas_call(
        _merge_body,
        grid=(t // tm, d // tn),
        in_specs=[act(y_attn.shape[1]), act(y_ssm.shape[1]), act(y_sc.shape[1]),
                  gate(0), gate(1), gate(2),
                  weight(w_a.shape[1]), weight(w_m.shape[1]), weight(w_c.shape[1])],
        out_specs=pl.BlockSpec((tm, tn), lambda i, j: (i, j)),
        out_shape=jax.ShapeDtypeStruct((t, d), BF16),
        compiler_params=_params("parallel", "arbitrary"),
        name="branch_merge",
    )(y_attn, y_ssm, y_sc, proj, proj, proj, w_a, w_m, w_c)


def _outproj_body(m_ref, x_ref, w_ref, gpost_ref, gnext_ref, o_ref, h_ref, wb_ref):
    @pl.when(pl.program_id(0) == 0)
    def _():
        wb_ref[...] = w_ref[...].astype(BF16)

    for rows in _row_chunks(m_ref.shape[0]):
        o = jnp.dot(m_ref[rows, :], wb_ref[...], preferred_element_type=F32)
        x = x_ref[rows, :] + _rms(o, gpost_ref[...])
        o_ref[rows, :] = x
        h_ref[rows, :] = _rms(x, gnext_ref[...]).astype(h_ref.dtype)


def out_projection(merged, x, w, g_post, g_next, layer, tm=512):
    t, d = x.shape
    tm = min(tm, t)
    rows = pl.BlockSpec((tm, d), lambda i: (i, 0))
    gain = pl.BlockSpec((None, 1, d), lambda i: (layer, 0, 0))
    return pl.pallas_call(
        _outproj_body,
        grid=(t // tm,),
        in_specs=[rows, rows, pl.BlockSpec((None, d, d), lambda i: (layer, 0, 0), pipeline_mode=pl.Buffered(1)),
                  gain, gain],
        out_specs=[rows, rows],
        out_shape=[jax.ShapeDtypeStruct((t, d), F32), jax.ShapeDtypeStruct((t, d), BF16)],
        scratch_shapes=[pltpu.VMEM((d, d), BF16)],
        compiler_params=_params("arbitrary"),
        name="out_projection",
    )(merged, x, w, g_post, g_next)


def _mlp_body(h_ref, w1_ref, w2_ref, gpost_ref, o_ref):
    f = pl.program_id(1)

    @pl.when(f == 0)
    def _():
        o_ref[...] = jnp.zeros(o_ref.shape, F32)

    u = jnp.maximum(jnp.dot(h_ref[...], w1_ref[...].astype(BF16), preferred_element_type=F32), 0.0)
    o_ref[...] += jnp.dot((u * u).astype(BF16), w2_ref[...].astype(BF16), preferred_element_type=F32)

    @pl.when(f == pl.num_programs(1) - 1)
    def _():
        o_ref[...] = _rms(o_ref[...], gpost_ref[...])


def mlp(h, w1, w2, g_post, layer, tm=1024, tf=512):
    t, d = h.shape
    ff = w1.shape[-1]
    tm = min(tm, t)
    return pl.pallas_call(
        _mlp_body,
        grid=(t // tm, ff // tf),
        in_specs=[
            pl.BlockSpec((tm, d), lambda i, f: (i, 0)),
            pl.BlockSpec((None, d, tf), lambda i, f: (layer, 0, f)),
            pl.BlockSpec((None, tf, d), lambda i, f: (layer, f, 0)),
            pl.BlockSpec((None, 1, d), lambda i, f: (layer, 0, 0)),
        ],
        out_specs=pl.BlockSpec((tm, d), lambda i, f: (i, 0)),
        out_shape=jax.ShapeDtypeStruct((t, d), F32),
        compiler_params=_params("parallel", "arbitrary"),
        name="mlp",
    )(h, w1, w2, g_post)


def _ple_body(x_ref, dx_ref, p_ref, gpre_ref, wg_ref, wp_ref, gpost_ref, *rest):
    wpb_ref = rest[-1]

    @pl.when(pl.program_id(0) == 0)
    def _():
        wpb_ref[...] = wp_ref[...].astype(BF16)

    for rows in _row_chunks(x_ref.shape[0]):
        x = x_ref[rows, :] + dx_ref[rows, :]
        h = _rms(x, gpre_ref[...]).astype(BF16)
        gate = _sigmoid(jnp.dot(h, wg_ref[...], preferred_element_type=F32))
        emb = jnp.dot(p_ref[rows, :].astype(BF16), wpb_ref[...], preferred_element_type=F32)
        x = x + _rms(gate * emb, gpost_ref[...])
        if len(rest) == 2:
            o_ref = rest[0]
        else:
            gnext_ref, o_ref, h_ref = rest[:3]
            h_ref[rows, :] = _rms(x, gnext_ref[...]).astype(h_ref.dtype)
        o_ref[rows, :] = x


def per_layer_embedding(x, dx, p, g_pre, w_gate, w_proj, g_post, layer, g_next=None, tm=512):
    t, d = x.shape
    pd = p.shape[-1]
    tm = min(tm, t)
    rows = pl.BlockSpec((tm, d), lambda i: (i, 0))
    gain = pl.BlockSpec((None, 1, d), lambda i: (layer, 0, 0))
    in_specs = [
        rows,
        rows,
        pl.BlockSpec((None, tm, pd), lambda i: (layer, i, 0)),
        gain,
        pl.BlockSpec((None, d, d), lambda i: (layer, 0, 0), pipeline_mode=pl.Buffered(1)),
        pl.BlockSpec((None, pd, d), lambda i: (layer, 0, 0), pipeline_mode=pl.Buffered(1)),
        gain,
    ]
    args = [x, dx, p, g_pre, w_gate, w_proj, g_post]
    out_specs = [rows]
    out_shape = [jax.ShapeDtypeStruct((t, d), F32)]
    if g_next is not None:
        in_specs.append(pl.BlockSpec((None, 1, d), lambda i: (layer + 1, 0, 0)))
        args.append(g_next)
        out_specs.append(rows)
        out_shape.append(jax.ShapeDtypeStruct((t, d), BF16))
    return pl.pallas_call(
        _ple_body,
        grid=(t // tm,),
        in_specs=in_specs,
        out_specs=out_specs,
        out_shape=out_shape,
        scratch_shapes=[pltpu.VMEM((pd, d), BF16)],
        compiler_params=_params("arbitrary"),
        name="per_layer_embedding",
    )(*args)


def _row(v):
    return v[:, None, :]


def _lane_pad(v):
    return jnp.pad(v, ((0, 0), (0, LANES - v.shape[-1])))[:, None, :]


def kernel(x, p, rel_bias, g_mix_pre, w_in, ssm_conv_w, ssm_conv_b, ssm_dt_bias, ssm_a_log, ssm_d, ssm_norm_g, sc_conv_w, w_br_attn, w_br_ssm, w_br_conv, w_out, g_mix_post, g_mlp_pre, w_mlp_up, w_mlp_down, g_mlp_post, g_ple_pre, w_ple_gate, w_ple_proj, g_ple_post):
    batch, seq, d = x.shape
    depth = w_in.shape[0]
    t = batch * seq

    w_in_t = jnp.swapaxes(w_in, 1, 2)
    w_a, w_m, w_c = w_br_attn.astype(BF16), w_br_ssm.astype(BF16), w_br_conv.astype(BF16)
    w_pg = w_ple_gate.astype(BF16)
    d_skip = jnp.repeat(ssm_d, SSM_HEAD_DIM, axis=-1)[:, None, :]
    dt_bias, a_log = _lane_pad(ssm_dt_bias), _lane_pad(ssm_a_log)
    conv_b, norm_g = _row(ssm_conv_b), _row(ssm_norm_g)
    g_mix_pre, g_mix_post = _row(g_mix_pre), _row(g_mix_post)
    g_mlp_pre, g_mlp_post = _row(g_mlp_pre), _row(g_mlp_post)
    g_ple_pre, g_ple_post = _row(g_ple_pre), _row(g_ple_post)

    bias = bias_tiles(rel_bias)
    xf = x.reshape(t, d)
    pf = p.reshape(depth, t, p.shape[-1])
    h = pre_norm(xf, g_mix_pre, 0)
    for i in range(depth):
        proj = in_projection(h, w_in_t, i)
        y_attn = moba_attention(proj, bias, batch, seq, QKV_OFF)
        dt_raw = dt_projection(h, w_in_t, i)
        y_ssm, y_sc = ssd_and_short_conv(proj, dt_raw, ssm_conv_w, conv_b, dt_bias, a_log, d_skip, norm_g,
                                         sc_conv_w, i, batch, seq)
        merged = branch_merge(y_attn, y_ssm, y_sc, proj, w_a, w_m, w_c, i)
        xf, h_mlp = out_projection(merged, xf, w_out, g_mix_post, g_mlp_pre, i)
        dx = mlp(h_mlp, w_mlp_up, w_mlp_down, g_mlp_post, i)
        if i + 1 < depth:
            xf, h = per_layer_embedding(xf, dx, pf, g_ple_pre, w_pg, w_ple_proj, g_ple_post, i, g_next=g_mix_pre)
        else:
            (xf,) = per_layer_embedding(xf, dx, pf, g_ple_pre, w_pg, w_ple_proj, g_ple_post, i)
    return xf.reshape(batch, seq, d)
```

```python
import functools
import math

import numpy as np
import jax
import jax.numpy as jnp
from jax import lax
from jax.experimental import pallas as pl
from jax.experimental.pallas import tpu as pltpu

F32 = jnp.float32
BF16 = jnp.bfloat16

NORM_EPS = 1e-6
NEG_INF = -1e30
LOG2E = math.log2(math.e)

D_MODEL = 2048
PLE_DIM = 256
ATTN_HEADS = 8
ATTN_HEAD_DIM = 128
ATTN_WIDTH = ATTN_HEADS * ATTN_HEAD_DIM
MOBA_BLOCK = 256
MOBA_TOPK = 3
ATTN_HEADS_PER_STEP = 2
REL_BUCKETS = 32
REL_MAX_DISTANCE = 128
SSM_D_INNER = D_MODEL
SSM_HEAD_DIM = 64
SSM_HEADS = SSM_D_INNER // SSM_HEAD_DIM
SSM_GROUPS = 4
SSM_STATE = 128
SSM_CONV = 4
SSM_CHUNK = 256
SSD_SUB = 128
SSM_CONV_DIM = SSM_D_INNER + 2 * SSM_GROUPS * SSM_STATE
SC_WIDTH = D_MODEL // 2
SC_CONV = 3
D_FF = 4 * D_MODEL

LANES = 128
SUBLANES = 8
VMEM_LIMIT_BYTES = 60 * 1024 * 1024

XBC_OFF = 0
SC_OFF = XBC_OFF + SSM_CONV_DIM
GATE_OFF = SC_OFF + 3 * SC_WIDTH
Z_OFF = GATE_OFF + 3 * D_MODEL
QKV_OFF = Z_OFF + SSM_D_INNER
PROJ_WIDTH = QKV_OFF + 3 * ATTN_WIDTH
PROJ_TN = 1024

_O_Q = 0
_O_Z = 3 * ATTN_WIDTH
_O_XBC = _O_Z + SSM_D_INNER
_O_DT = _O_XBC + SSM_CONV_DIM
_O_SC = _O_DT + SSM_HEADS
_O_GATE = _O_SC + 3 * SC_WIDTH
_O_END = _O_GATE + 3 * D_MODEL

_NT = (((1,), (1,)), ((), ()))


def _params(*sem):
    return pltpu.CompilerParams(dimension_semantics=sem, vmem_limit_bytes=VMEM_LIMIT_BYTES)


def _rms(x, g):
    return x * lax.rsqrt(jnp.mean(x * x, axis=-1, keepdims=True) + NORM_EPS) * g


def _sigmoid(x):
    return lax.logistic(x)


def _row_chunks(n, parts=2):
    step = n // parts
    return [slice(k * step, (k + 1) * step) for k in range(parts)]


_T = PROJ_TN
_N_XBC, _N_TAIL, _N_Z, _N_QKV = SSM_CONV_DIM // _T, (_O_END - _O_SC) // _T, SSM_D_INNER // _T, 3 * ATTN_WIDTH // _T
_J_TAIL, _J_Z, _J_QKV = _N_XBC, _N_XBC + _N_TAIL, _N_XBC + _N_TAIL + _N_Z
assert (_J_QKV + _N_QKV) * _T == PROJ_WIDTH
assert all(o % SUBLANES == 0 for o in (_O_XBC, _O_SC, _O_Z, _O_Q, _O_DT)) and _T % SUBLANES == 0


def _src_row(j):
    s, ts = SUBLANES, _T // SUBLANES
    groups = jnp.where(j < _J_TAIL, _O_XBC // s + ts * j,
                       jnp.where(j < _J_Z, _O_SC // s + ts * (j - _J_TAIL),
                                 jnp.where(j < _J_QKV, _O_Z // s + ts * (j - _J_Z), _O_Q // s + ts * (j - _J_QKV))))
    return groups * s


def _norm_body(x_ref, g_ref, o_ref):
    o_ref[...] = _rms(x_ref[...], g_ref[...]).astype(o_ref.dtype)


def pre_norm(x, g, layer, tm=512):
    t, d = x.shape
    tm = min(tm, t)
    return pl.pallas_call(
        _norm_body,
        grid=(t // tm,),
        in_specs=[pl.BlockSpec((tm, d), lambda i: (i, 0)),
                  pl.BlockSpec((None, 1, d), lambda i: (layer, 0, 0))],
        out_specs=pl.BlockSpec((tm, d), lambda i: (i, 0)),
        out_shape=jax.ShapeDtypeStruct((t, d), BF16),
        compiler_params=_params("parallel"),
        name="pre_norm",
    )(x, g)


def _inproj_body(h_ref, wt_ref, o_ref):
    o_ref[...] = lax.dot_general(h_ref[...], wt_ref[0].astype(BF16), _NT, preferred_element_type=F32)


def in_projection(h, w_t, layer, tm=2048):
    t, d = h.shape
    tn = _T
    tm = min(tm, t)
    return pl.pallas_call(
        _inproj_body,
        grid=(t // tm, PROJ_WIDTH // tn),
        in_specs=[
            pl.BlockSpec((tm, d), lambda i, j: (i, 0)),
            pl.BlockSpec((pl.Element(1), pl.Element(tn), pl.Element(d)), lambda i, j: (layer, _src_row(j), 0)),
        ],
        out_specs=pl.BlockSpec((tm, tn), lambda i, j: (i, j)),
        out_shape=jax.ShapeDtypeStruct((t, PROJ_WIDTH), F32),
        compiler_params=_params("parallel", "arbitrary"),
        name="in_projection",
    )(h, w_t)


def dt_projection(h, w_t, layer, tm=2048):
    t, d = h.shape
    tm = min(tm, t)
    return pl.pallas_call(
        _inproj_body,
        grid=(t // tm,),
        in_specs=[
            pl.BlockSpec((tm, d), lambda i: (i, 0)),
            pl.BlockSpec((pl.Element(1), pl.Element(LANES), pl.Element(d)), lambda i: (layer, _O_DT, 0)),
        ],
        out_specs=pl.BlockSpec((tm, LANES), lambda i: (i, 0)),
        out_shape=jax.ShapeDtypeStruct((t, LANES), F32),
        compiler_params=_params("parallel"),
        name="dt_projection",
    )(h, w_t)


def _bias_body(tbl_ref, o_ref):
    h = pl.program_id(0)
    blk = MOBA_BLOCK
    key = lax.broadcasted_iota(jnp.int32, (blk, blk), 0)
    qry = lax.broadcasted_iota(jnp.int32, (blk, blk), 1)
    max_exact = REL_BUCKETS // 2
    for t in range(3):
        n = jnp.maximum(t * blk + qry - key, 0)
        nf = jnp.maximum(n, 1).astype(F32)
        large = max_exact + (jnp.log(nf / max_exact) / math.log(REL_MAX_DISTANCE / max_exact)
                             * (REL_BUCKETS - max_exact)).astype(jnp.int32)
        large = jnp.minimum(large, REL_BUCKETS - 1)
        bucket = jnp.where(n < max_exact, n, large)
        acc = jnp.zeros((blk, blk), F32)
        for b in range(REL_BUCKETS):
            acc = jnp.where(bucket == b, tbl_ref[b, h], acc)
        o_ref[t] = acc


def bias_tiles(rel_bias):
    nh = rel_bias.shape[1]
    return pl.pallas_call(
        _bias_body,
        grid=(nh,),
        in_specs=[pl.BlockSpec(memory_space=pltpu.SMEM)],
        out_specs=pl.BlockSpec((None, 3, MOBA_BLOCK, MOBA_BLOCK), lambda h: (h, 0, 0, 0)),
        out_shape=jax.ShapeDtypeStruct((nh, 3, MOBA_BLOCK, MOBA_BLOCK), F32),
        compiler_params=_params("arbitrary"),
        name="bias_tiles",
    )(rel_bias)


def _attn_body(q_ref, k_ref, v_ref, bias_ref, o_ref, kmean_ref, *, nb, heads):
    dh = ATTN_HEAD_DIM
    chains = []
    for hh in range(heads):
        lanes = slice(hh * dh, (hh + 1) * dh)
        chains.append(_attn_head(q_ref.at[:, lanes], k_ref.at[:, lanes], v_ref.at[:, lanes], bias_ref.at[hh],
                                 o_ref.at[:, lanes], kmean_ref.at[hh], nb=nb))
    while chains:
        chains = [c for c in chains if next(c, _DONE) is not _DONE]


_DONE = object()


def _attn_head(q_ref, k_ref, v_ref, bias_ref, o_ref, kmean_ref, *, nb):
    blk = MOBA_BLOCK
    scale = ATTN_HEAD_DIM ** -0.5

    k = k_ref[...]
    kb = k.astype(BF16)
    v_t = v_ref[...].T.astype(BF16)
    for j in range(nb):
        kmean_ref[j:j + 1, :] = jnp.mean(k[j * blk:(j + 1) * blk, :], axis=0, keepdims=True)
    q_all = q_ref[...]
    gate_all = lax.dot_general(kmean_ref[...], q_all, _NT, precision=lax.Precision.HIGHEST,
                               preferred_element_type=F32)
    qb_all = (q_all * (scale * LOG2E)).astype(BF16)

    key = lax.broadcasted_iota(jnp.int32, (blk, blk), 0)
    qry = lax.broadcasted_iota(jnp.int32, (blk, blk), 1)
    causal = key <= qry
    bias_own = bias_ref[0] * LOG2E
    bias_prev = bias_ref[1] * LOG2E
    bias_far = bias_ref[2, 0:1, 0:1] * LOG2E
    bidx = lax.broadcasted_iota(jnp.int32, (gate_all.shape[0], blk), 0)

    def scores(qi):
        return lax.dot_general(kb[0:(qi + 1) * blk, :], qb_all[qi * blk:(qi + 1) * blk, :], _NT,
                               preferred_element_type=F32)

    def finish(cols, nk, p_t, denom):
        o_t = jnp.dot(v_t[:, 0:nk], p_t, preferred_element_type=F32)
        o_ref[cols, :] = (o_t / denom).T.astype(o_ref.dtype)

    pending = None
    s_next = scores(0)
    yield
    for qi in range(nb):
        cols = slice(qi * blk, (qi + 1) * blk)
        nk = (qi + 1) * blk
        s = s_next
        if qi + 1 < nb:
            s_next = scores(qi + 1)

        keep = None
        if qi > MOBA_TOPK:
            gate = gate_all[:, cols]
            past = bidx < qi
            keep = []
            for j in range(qi):
                g_j = gate[j:j + 1, :]
                beats = ((gate > g_j) | ((gate == g_j) & (bidx < j))) & past
                keep.append(jnp.sum(beats.astype(F32), axis=0, keepdims=True) < MOBA_TOPK)

        pieces = []
        for j in range(qi + 1):
            s_j = s[j * blk:(j + 1) * blk, :]
            if j == qi:
                s_j = jnp.where(causal, s_j + bias_own, NEG_INF)
            elif j == qi - 1:
                s_j = s_j + bias_prev
                if keep is not None:
                    s_j = jnp.where(keep[j], s_j, NEG_INF)
            elif keep is not None:
                s_j = s_j + jnp.where(keep[j], bias_far, NEG_INF)
            else:
                s_j = s_j + bias_far
            pieces.append(s_j)

        m = pieces[0]
        for s_j in pieces[1:]:
            m = jnp.maximum(m, s_j)
        m = jnp.max(m, axis=0, keepdims=True)
        probs = [jnp.exp2(s_j - m) for s_j in pieces]
        total = probs[0]
        for p_j in probs[1:]:
            total = total + p_j
        denom = jnp.sum(total, axis=0, keepdims=True)
        p_t = jnp.concatenate([p_j.astype(BF16) for p_j in probs], axis=0)
        if pending is not None:
            finish(*pending)
        pending = (cols, nk, p_t, denom)
        yield
    finish(*pending)


def moba_attention(proj, bias, batch, seq, qkv_off):
    t = proj.shape[0]
    blk, dh, nh = MOBA_BLOCK, ATTN_HEAD_DIM, ATTN_HEADS
    nb = seq // blk
    hps = ATTN_HEADS_PER_STEP
    w = hps * dh
    qc = qkv_off // w
    return pl.pallas_call(
        functools.partial(_attn_body, nb=nb, heads=hps),
        grid=(batch, nh // hps),
        in_specs=[
            pl.BlockSpec((seq, w), lambda b, h: (b, qc + h)),
            pl.BlockSpec((seq, w), lambda b, h: (b, qc + nh // hps + h)),
            pl.BlockSpec((seq, w), lambda b, h: (b, qc + 2 * (nh // hps) + h)),
            pl.BlockSpec((hps, 3, blk, blk), lambda b, h: (h, 0, 0, 0)),
        ],
        out_specs=pl.BlockSpec((seq, w), lambda b, h: (b, h)),
        out_shape=jax.ShapeDtypeStruct((t, nh * dh), BF16),
        scratch_shapes=[pltpu.VMEM((hps, max(nb, SUBLANES), dh), F32)],
        compiler_params=_params("parallel", "parallel"),
        name="moba_attention",
    )(proj, proj, proj, bias)


def _hi_lo(v):
    hi = v.astype(BF16)
    lo = (v - hi.astype(F32)).astype(BF16)
    return jnp.concatenate([hi, lo], axis=1)


def _causal_taps(ext_ref, cs, w_ref, rows):
    width = w_ref.shape[0]
    cols = w_ref[:, cs].shape[1]
    shape = (rows // SUBLANES, SUBLANES, cols)
    prev = ext_ref[0:rows, cs].reshape(shape)
    cur = ext_ref[SUBLANES:SUBLANES + rows, cs].reshape(shape)
    sub = lax.broadcasted_iota(jnp.int32, shape, 1)
    acc = w_ref[width - 1:width, cs].reshape(1, 1, cols) * cur
    for j in range(1, width):
        shifted = pltpu.roll(jnp.where(sub >= SUBLANES - j, prev, cur), j, 1)
        acc = acc + w_ref[width - 1 - j:width - j, cs].reshape(1, 1, cols) * shifted
    return acc.reshape(rows, cols)


def _mixer_body(xbc_ref, sc_ref, z_ref, dt_ref, cw_ref, cb_ref, dtb_ref, alog_ref, dskip_ref, ng_ref,
                scw_ref, e_ref, yssm_ref, ysc_ref, ext_ref, ext2_ref, xc_ref, state_ref):
    c = pl.program_id(1)
    L = SSM_CHUNK
    sub = SSD_SUB
    halo = SUBLANES
    gw = SSM_D_INNER // SSM_GROUPS
    hpg = SSM_HEADS // SSM_GROUPS
    n_state = SSM_STATE

    @pl.when(c == 0)
    def _():
        ext_ref[0:halo, :] = jnp.zeros((halo, ext_ref.shape[1]), F32)
        ext2_ref[0:halo, :] = jnp.zeros((halo, ext2_ref.shape[1]), F32)
        state_ref[...] = jnp.zeros(state_ref.shape, F32)

    @pl.when(c > 0)
    def _():
        ext_ref[0:halo, :] = ext_ref[L:L + halo, :]
        ext2_ref[0:halo, :] = ext2_ref[L:L + halo, :]

    ext_ref[halo:halo + L, :] = xbc_ref[...]
    cchunk = 512
    for ci in range(SSM_CONV_DIM // cchunk):
        cs = slice(ci * cchunk, (ci + 1) * cchunk)
        v = _causal_taps(ext_ref, cs, cw_ref, L) + cb_ref[:, cs]
        xc_ref[:, cs] = v * _sigmoid(v)

    ext2_ref[halo:halo + L, :] = sc_ref[:, SC_WIDTH:2 * SC_WIDTH] * sc_ref[:, 2 * SC_WIDTH:3 * SC_WIDTH]
    for ci in range(SC_WIDTH // cchunk):
        cs = slice(ci * cchunk, (ci + 1) * cchunk)
        ysc_ref[:, cs] = (sc_ref[:, cs] * _causal_taps(ext2_ref, cs, scw_ref, L)).astype(ysc_ref.dtype)

    rows = lax.broadcasted_iota(jnp.int32, (L, LANES), 0)
    head_lane = lax.broadcasted_iota(jnp.int32, (L, LANES), 1) < SSM_HEADS
    x_dt = jnp.where(head_lane, dt_ref[...], 0.0) + dtb_ref[...]
    dt = jnp.maximum(x_dt, 0.0) + jnp.log1p(jnp.exp(-jnp.abs(x_dt)))
    a = dt * (-LOG2E * jnp.exp(alog_ref[...]))
    pos = rows % sub
    a_cum = a
    sh = 1
    while sh < sub:
        a_cum = a_cum + jnp.where(pos >= sh, pltpu.roll(a_cum, sh, axis=0), 0.0)
        sh *= 2
    a_cum_t = a_cum.T
    ea = jnp.exp2(a_cum)
    a_end = a_cum[L - 1:L, :]
    for s in range(L // sub - 2, -1, -1):
        a_end = jnp.where(rows < (s + 1) * sub, a_cum[(s + 1) * sub - 1:(s + 1) * sub, :], a_end)
    ds = jnp.exp2(a_end - a_cum)
    dt_hl, ea_hl, ds_hl = _hi_lo(dt), _hi_lo(ea), _hi_lo(ds)

    row = lax.broadcasted_iota(jnp.int32, (sub, sub), 0)
    col = lax.broadcasted_iota(jnp.int32, (sub, sub), 1)
    causal = row >= col
    low_half = lax.broadcasted_iota(jnp.int32, (sub, LANES), 1) < SSM_HEAD_DIM

    for g in range(SSM_GROUPS):
        cs = slice(g * gw, (g + 1) * gw)
        b_cols = slice(SSM_D_INNER + g * n_state, SSM_D_INNER + (g + 1) * n_state)
        c_cols = slice(SSM_D_INNER + (SSM_GROUPS + g) * n_state, SSM_D_INNER + (SSM_GROUPS + g + 1) * n_state)
        e_g = e_ref[:, cs]
        xs_g = xc_ref[:, cs]
        ea_e = jnp.dot(ea_hl, e_g, preferred_element_type=F32)
        xdt = xs_g * jnp.dot(dt_hl, e_g, preferred_element_type=F32)
        xdt_b = xdt.astype(BF16)
        xds_b = (xdt * jnp.dot(ds_hl, e_g, preferred_element_type=F32)).astype(BF16)

        st = state_ref[g]
        y_rows = []
        for s in range(L // sub):
            rs = slice(s * sub, (s + 1) * sub)
            b_s = xc_ref[rs, b_cols]
            c_sb = xc_ref[rs, c_cols].astype(BF16)
            cb = lax.dot_general(c_sb, b_s.astype(BF16), _NT, preferred_element_type=F32)
            y = jnp.dot(c_sb, st.astype(BF16), preferred_element_type=F32) * ea_e[rs, :]

            pieces = []
            for pe in range(hpg // 2):
                x_pair = xdt_b[rs, pe * LANES:(pe + 1) * LANES]
                acc = None
                for half in range(2):
                    h = g * hpg + 2 * pe + half
                    seg = a_cum[rs, h:h + 1] - a_cum_t[h:h + 1, rs]
                    decay = jnp.exp2(jnp.where(causal, seg, -jnp.inf))
                    m_h = (cb * decay).astype(BF16)
                    keep = low_half if half == 0 else jnp.logical_not(low_half)
                    x_h = jnp.where(keep, x_pair, jnp.zeros_like(x_pair))
                    term = jnp.dot(m_h, x_h, preferred_element_type=F32)
                    acc = term if acc is None else acc + term
                pieces.append(acc)
            y_rows.append(y + jnp.concatenate(pieces, axis=1))

            upd = jnp.dot(b_s.T.astype(BF16), xds_b[rs, :], preferred_element_type=F32)
            st = st * ea_e[(s + 1) * sub - 1:(s + 1) * sub, :] + upd
        state_ref[g] = st

        y = jnp.concatenate(y_rows, axis=0) + dskip_ref[:, cs] * xs_g
        z_g = z_ref[:, cs]
        y = y * (z_g * _sigmoid(z_g))
        y = y * lax.rsqrt(jnp.mean(y * y, axis=-1, keepdims=True) + NORM_EPS)
        yssm_ref[:, cs] = (y * ng_ref[:, cs]).astype(yssm_ref.dtype)


def _head_expand_matrix():
    e = np.zeros((2 * LANES, SSM_D_INNER), np.float32)
    for h in range(SSM_HEADS):
        e[h, h * SSM_HEAD_DIM:(h + 1) * SSM_HEAD_DIM] = 1.0
        e[LANES + h, h * SSM_HEAD_DIM:(h + 1) * SSM_HEAD_DIM] = 1.0
    return jnp.asarray(e, BF16)


def ssd_and_short_conv(proj, dt_raw, conv_w, conv_b, dt_bias, a_log, d_skip, norm_g, sc_w, layer, batch, seq):
    t = proj.shape[0]
    L = SSM_CHUNK
    nc = seq // L
    row_map = lambda b, c: (b * nc + c, 0)

    def col_block(off, width):
        return pl.BlockSpec((L, width), lambda b, c: (b * nc + c, off // width))

    def param(rows, cols):
        return pl.BlockSpec((None, rows, cols), lambda b, c: (layer, 0, 0))

    return pl.pallas_call(
        _mixer_body,
        grid=(batch, nc),
        in_specs=[
            col_block(XBC_OFF, SSM_CONV_DIM),
            col_block(SC_OFF, 3 * SC_WIDTH),
            col_block(Z_OFF, SSM_D_INNER),
            pl.BlockSpec((L, LANES), row_map),
            param(SSM_CONV, SSM_CONV_DIM),
            param(1, SSM_CONV_DIM),
            param(1, LANES),
            param(1, LANES),
            param(1, SSM_D_INNER),
            param(1, SSM_D_INNER),
            param(SC_CONV, SC_WIDTH),
            pl.BlockSpec((2 * LANES, SSM_D_INNER), lambda b, c: (0, 0)),
        ],
        out_specs=[
            pl.BlockSpec((L, SSM_D_INNER), row_map),
            pl.BlockSpec((L, SC_WIDTH), row_map),
        ],
        out_shape=[
            jax.ShapeDtypeStruct((t, SSM_D_INNER), BF16),
            jax.ShapeDtypeStruct((t, SC_WIDTH), BF16),
        ],
        scratch_shapes=[
            pltpu.VMEM((L + 2 * SUBLANES, SSM_CONV_DIM), F32),
            pltpu.VMEM((L + 2 * SUBLANES, SC_WIDTH), F32),
            pltpu.VMEM((L, SSM_CONV_DIM), F32),
            pltpu.VMEM((SSM_GROUPS, SSM_STATE, SSM_D_INNER // SSM_GROUPS), F32),
        ],
        compiler_params=_params("parallel", "arbitrary"),
        name="ssd_short_conv",
    )(proj, proj, proj, dt_raw, conv_w, conv_b, dt_bias, a_log, d_skip, norm_g, sc_w, _head_expand_matrix())


def _merge_body(ya_ref, ym_ref, yc_ref, ga_ref, gm_ref, gc_ref, wa_ref, wm_ref, wc_ref, o_ref):
    tn = o_ref.shape[1]
    cols = pl.ds(pl.multiple_of(pl.program_id(1) * tn, tn), tn)
    a = jnp.dot(ya_ref[...], wa_ref[:, cols], preferred_element_type=F32)
    m = jnp.dot(ym_ref[...], wm_ref[:, cols], preferred_element_type=F32)
    c = jnp.dot(yc_ref[...], wc_ref[:, cols], preferred_element_type=F32)
    o = _sigmoid(ga_ref[...]) * a + _sigmoid(gm_ref[...]) * m + _sigmoid(gc_ref[...]) * c
    o_ref[...] = o.astype(o_ref.dtype)


def branch_merge(y_attn, y_ssm, y_sc, proj, w_a, w_m, w_c, layer, tm=1024, tn=512):
    t = y_attn.shape[0]
    d = w_a.shape[-1]
    tm = min(tm, t)
    gc = GATE_OFF // tn
    gs = d // tn

    def act(width):
        return pl.BlockSpec((tm, width), lambda i, j: (i, 0))

    def gate(k):
        return pl.BlockSpec((tm, tn), lambda i, j: (i, gc + k * gs + j))

    def weight(rows):
        return pl.BlockSpec((None, rows, d), lambda i, j: (layer, 0, 0), pipeline_mode=pl.Buffered(1))

    return pl.pallas_call(
        _merge_body,
        grid=(t // tm, d // tn),
        in_specs=[act(y_attn.shape[1]), act(y_ssm.shape[1]), act(y_sc.shape[1]),
                  gate(0), gate(1), gate(2),
                  weight(w_a.shape[1]), weight(w_m.shape[1]), weight(w_c.shape[1])],
        out_specs=pl.BlockSpec((tm, tn), lambda i, j: (i, j)),
        out_shape=jax.ShapeDtypeStruct((t, d), BF16),
        compiler_params=_params("parallel", "arbitrary"),
        name="branch_merge",
    )(y_attn, y_ssm, y_sc, proj, proj, proj, w_a, w_m, w_c)


def _outproj_body(m_ref, x_ref, w_ref, gpost_ref, gnext_ref, o_ref, h_ref, wb_ref):
    @pl.when(pl.program_id(0) == 0)
    def _():
        wb_ref[...] = w_ref[...].astype(BF16)

    for rows in _row_chunks(m_ref.shape[0]):
        o = jnp.dot(m_ref[rows, :], wb_ref[...], preferred_element_type=F32)
        x = x_ref[rows, :] + _rms(o, gpost_ref[...])
        o_ref[rows, :] = x
        h_ref[rows, :] = _rms(x, gnext_ref[...]).astype(h_ref.dtype)


def out_projection(merged, x, w, g_post, g_next, layer, tm=512):
    t, d = x.shape
    tm = min(tm, t)
    rows = pl.BlockSpec((tm, d), lambda i: (i, 0))
    gain = pl.BlockSpec((None, 1, d), lambda i: (layer, 0, 0))
    return pl.pallas_call(
        _outproj_body,
        grid=(t // tm,),
        in_specs=[rows, rows, pl.BlockSpec((None, d, d), lambda i: (layer, 0, 0), pipeline_mode=pl.Buffered(1)),
                  gain, gain],
        out_specs=[rows, rows],
        out_shape=[jax.ShapeDtypeStruct((t, d), F32), jax.ShapeDtypeStruct((t, d), BF16)],
        scratch_shapes=[pltpu.VMEM((d, d), BF16)],
        compiler_params=_params("arbitrary"),
        name="out_projection",
    )(merged, x, w, g_post, g_next)


def _mlp_body(h_ref, w1_ref, w2_ref, gpost_ref, o_ref):
    f = pl.program_id(1)

    @pl.when(f == 0)
    def _():
        o_ref[...] = jnp.zeros(o_ref.shape, F32)

    u = jnp.maximum(jnp.dot(h_ref[...], w1_ref[...].astype(BF16), preferred_element_type=F32), 0.0)
    o_ref[...] += jnp.dot((u * u).astype(BF16), w2_ref[...].astype(BF16), preferred_element_type=F32)

    @pl.when(f == pl.num_programs(1) - 1)
    def _():
        o_ref[...] = _rms(o_ref[...], gpost_ref[...])


def mlp(h, w1, w2, g_post, layer, tm=1024, tf=512):
    t, d = h.shape
    ff = w1.shape[-1]
    tm = min(tm, t)
    return pl.pallas_call(
        _mlp_body,
        grid=(t // tm, ff // tf),
        in_specs=[
            pl.BlockSpec((tm, d), lambda i, f: (i, 0)),
            pl.BlockSpec((None, d, tf), lambda i, f: (layer, 0, f)),
            pl.BlockSpec((None, tf, d), lambda i, f: (layer, f, 0)),
            pl.BlockSpec((None, 1, d), lambda i, f: (layer, 0, 0)),
        ],
        out_specs=pl.BlockSpec((tm, d), lambda i, f: (i, 0)),
        out_shape=jax.ShapeDtypeStruct((t, d), F32),
        compiler_params=_params("parallel", "arbitrary"),
        name="mlp",
    )(h, w1, w2, g_post)


def _ple_body(x_ref, dx_ref, p_ref, gpre_ref, wg_ref, wp_ref, gpost_ref, *rest):
    wpb_ref = rest[-1]

    @pl.when(pl.program_id(0) == 0)
    def _():
        wpb_ref[...] = wp_ref[...].astype(BF16)

    for rows in _row_chunks(x_ref.shape[0]):
        x = x_ref[rows, :] + dx_ref[rows, :]
        h = _rms(x, gpre_ref[...]).astype(BF16)
        gate = _sigmoid(jnp.dot(h, wg_ref[...], preferred_element_type=F32))
        emb = jnp.dot(p_ref[rows, :].astype(BF16), wpb_ref[...], preferred_element_type=F32)
        x = x + _rms(gate * emb, gpost_ref[...])
        if len(rest) == 2:
            o_ref = rest[0]
        else:
            gnext_ref, o_ref, h_ref = rest[:3]
            h_ref[rows, :] = _rms(x, gnext_ref[...]).astype(h_ref.dtype)
        o_ref[rows, :] = x


def per_layer_embedding(x, dx, p, g_pre, w_gate, w_proj, g_post, layer, g_next=None, tm=512):
    t, d = x.shape
    pd = p.shape[-1]
    tm = min(tm, t)
    rows = pl.BlockSpec((tm, d), lambda i: (i, 0))
    gain = pl.BlockSpec((None, 1, d), lambda i: (layer, 0, 0))
    in_specs = [
        rows,
        rows,
        pl.BlockSpec((None, tm, pd), lambda i: (layer, i, 0)),
        gain,
        pl.BlockSpec((None, d, d), lambda i: (layer, 0, 0), pipeline_mode=pl.Buffered(1)),
        pl.BlockSpec((None, pd, d), lambda i: (layer, 0, 0), pipeline_mode=pl.Buffered(1)),
        gain,
    ]
    args = [x, dx, p, g_pre, w_gate, w_proj, g_post]
    out_specs = [rows]
    out_shape = [jax.ShapeDtypeStruct((t, d), F32)]
    if g_next is not None:
        in_specs.append(pl.BlockSpec((None, 1, d), lambda i: (layer + 1, 0, 0)))
        args.append(g_next)
        out_specs.append(rows)
        out_shape.append(jax.ShapeDtypeStruct((t, d), BF16))
    return pl.pallas_call(
        _ple_body,
        grid=(t // tm,),
        in_specs=in_specs,
        out_specs=out_specs,
        out_shape=out_shape,
        scratch_shapes=[pltpu.VMEM((pd, d), BF16)],
        compiler_params=_params("arbitrary"),
        name="per_layer_embedding",
    )(*args)


def _row(v):
    return v[:, None, :]


def _lane_pad(v):
    return jnp.pad(v, ((0, 0), (0, LANES - v.shape[-1])))[:, None, :]


def kernel(x, p, rel_bias, g_mix_pre, w_in, ssm_conv_w, ssm_conv_b, ssm_dt_bias, ssm_a_log, ssm_d, ssm_norm_g, sc_conv_w, w_br_attn, w_br_ssm, w_br_conv, w_out, g_mix_post, g_mlp_pre, w_mlp_up, w_mlp_down, g_mlp_post, g_ple_pre, w_ple_gate, w_ple_proj, g_ple_post):
    batch, seq, d = x.shape
    depth = w_in.shape[0]
    t = batch * seq

    w_in_t = jnp.swapaxes(w_in, 1, 2)
    w_a, w_m, w_c = w_br_attn.astype(BF16), w_br_ssm.astype(BF16), w_br_conv.astype(BF16)
    w_pg = w_ple_gate.astype(BF16)
    d_skip = jnp.repeat(ssm_d, SSM_HEAD_DIM, axis=-1)[:, None, :]
    dt_bias, a_log = _lane_pad(ssm_dt_bias), _lane_pad(ssm_a_log)
    conv_b, norm_g = _row(ssm_conv_b), _row(ssm_norm_g)
    g_mix_pre, g_mix_post = _row(g_mix_pre), _row(g_mix_post)
    g_mlp_pre, g_mlp_post = _row(g_mlp_pre), _row(g_mlp_post)
    g_ple_pre, g_ple_post = _row(g_ple_pre), _row(g_ple_post)

    bias = bias_tiles(rel_bias)
    xf = x.reshape(t, d)
    pf = p.reshape(depth, t, p.shape[-1])
    h = pre_norm(xf, g_mix_pre, 0)
    for i in range(depth):
        proj = in_projection(h, w_in_t, i)
        y_attn = moba_attention(proj, bias, batch, seq, QKV_OFF)
        dt_raw = dt_projection(h, w_in_t, i)
        y_ssm, y_sc = ssd_and_short_conv(proj, dt_raw, ssm_conv_w, conv_b, dt_bias, a_log, d_skip, norm_g,
                                         sc_conv_w, i, batch, seq)
        merged = branch_merge(y_attn, y_ssm, y_sc, proj, w_a, w_m, w_c, i)
        xf, h_mlp = out_projection(merged, xf, w_out, g_mix_post, g_mlp_pre, i)
        dx = mlp(h_mlp, w_mlp_up, w_mlp_down, g_mlp_post, i)
        if i + 1 < depth:
            xf, h = per_layer_embedding(xf, dx, pf, g_ple_pre, w_pg, w_ple_proj, g_ple_post, i, g_next=g_mix_pre)
        else:
            (xf,) = per_layer_embedding(xf, dx, pf, g_ple_pre, w_pg, w_ple_proj, g_ple_post, i)
    return xf.reshape(batch, seq, d)
```
